```python
import math
import jax
import jax.numpy as jnp
from jax import lax
import numpy as np

D_MODEL = 4096
BATCH = 1
SEQ = 8192
DEPTH = 2

SSD_HEADS = 32
SSD_HEAD_DIM = 64
SSD_INNER = SSD_HEADS * SSD_HEAD_DIM
SSD_GROUPS = 4
SSD_STATE = 128
SSD_CHUNK = 128
SSD_CONV_DIM = SSD_INNER + 2 * SSD_GROUPS * SSD_STATE
CONV_WIDTH = 4
GMLP_WIDTH = 2048
GMLP_GROUPS = 8
GMLP_CHUNK = 128
LRU_WIDTH = 2048
LRU_BLOCKS = 8
LRU_C = 8.0
N_BRANCHES = 3
N_EXPERTS = 32
TOP_K = 4
D_EXPERT = 512
SWIGLU_LIMIT = 7.0
SWIGLU_ALPHA = 1.702
EXPERT_BLOCK = 128
DN_ALPHA = (2 * DEPTH) ** 0.25
DN_BETA = (8 * DEPTH) ** -0.25
LN_EPS = 1e-5
RMS_EPS = 1e-5

IN_SIZES = (SSD_INNER, SSD_CONV_DIM, SSD_HEADS, GMLP_WIDTH, GMLP_WIDTH, LRU_WIDTH, LRU_WIDTH, N_BRANCHES * D_MODEL)
IN_OFFSETS = tuple(sum(IN_SIZES[: i + 1]) for i in range(len(IN_SIZES) - 1))
D_IN = sum(IN_SIZES)

kernel_name = "hybrid_ssd_gmlp_rglru_moe_deepnorm"


def layer_norm(x, g, b):
    xf = x.astype(jnp.float32)
    mu = jnp.mean(xf, axis=-1, keepdims=True)
    var = jnp.mean(jnp.square(xf - mu), axis=-1, keepdims=True)
    return ((xf - mu) * lax.rsqrt(var + LN_EPS) * g + b).astype(x.dtype)


def causal_conv(x, w, b):
    k_w = w.shape[0]
    s = x.shape[1]
    xp = jnp.pad(x, ((0, 0), (k_w - 1, 0), (0, 0)))
    y = xp[:, 0:s] * w[0]
    for k in range(1, k_w):
        y = y + xp[:, k:k + s] * w[k]
    return y + b


def segsum(a):
    n = a.shape[-1]
    cs = jnp.cumsum(a, axis=-1)
    diff = cs[..., :, None] - cs[..., None, :]
    mask = jnp.tril(jnp.ones((n, n), dtype=bool))
    return jnp.where(mask, diff, -jnp.inf)


def ssd_scan(xh, dt, a, bm, cm):
    b, s, h, p = xh.shape
    g, n = bm.shape[-2:]
    kh = h // g
    nc = s // SSD_CHUNK
    lc = SSD_CHUNK
    xdt = (xh.astype(jnp.float32) * dt[..., None]).reshape(b, nc, lc, g, kh, p)
    da = jnp.moveaxis((dt * a).reshape(b, nc, lc, g, kh), 2, -1)
    bc = bm.astype(jnp.float32).reshape(b, nc, lc, g, n)
    cc = cm.astype(jnp.float32).reshape(b, nc, lc, g, n)
    cum = jnp.cumsum(da, axis=-1)
    cb = jnp.einsum("bclgn,bcsgn->bcgls", cc, bc)
    scores = cb[:, :, :, None] * jnp.exp(segsum(da))
    y_diag = jnp.einsum("bcgkls,bcsgkp->bclgkp", scores, xdt)
    decay_states = jnp.exp(cum[..., -1:] - cum)
    states = jnp.einsum("bclgn,bcgkl,bclgkp->bcgkpn", bc, decay_states, xdt)
    chunk_decay = jnp.exp(cum[..., -1])

    def step(state, inp):
        s_c, d_c = inp
        return state * d_c[..., None, None] + s_c, state

    h0 = jnp.zeros_like(states[:, 0])
    _, prev = lax.scan(step, h0, (jnp.moveaxis(states, 1, 0), jnp.moveaxis(chunk_decay, 1, 0)))
    prev = jnp.moveaxis(prev, 0, 1)
    y_off = jnp.einsum("bclgn,bcgkpn,bcgkl->bclgkp", cc, prev, jnp.exp(cum))
    return (y_diag + y_off).reshape(b, s, h, p)


def ssd_mixer(z, xbc, dt_raw, conv_w, conv_b, dt_bias, a_log, d_skip, norm_g):
    b, s, _ = z.shape
    xbc = jax.nn.silu(causal_conv(xbc, conv_w, conv_b))
    xs, bm, cm = jnp.split(xbc, [SSD_INNER, SSD_INNER + SSD_GROUPS * SSD_STATE], axis=-1)
    xh = xs.reshape(b, s, SSD_HEADS, SSD_HEAD_DIM)
    dt = jax.nn.softplus(dt_raw.astype(jnp.float32) + dt_bias.astype(jnp.float32))
    a = -jnp.exp(a_log.astype(jnp.float32))
    y = ssd_scan(xh, dt, a, bm.reshape(b, s, SSD_GROUPS, SSD_STATE), cm.reshape(b, s, SSD_GROUPS, SSD_STATE))
    y = y + xh.astype(jnp.float32) * d_skip.astype(jnp.float32)[:, None]
    yg = (y.reshape(b, s, SSD_INNER) * jax.nn.silu(z.astype(jnp.float32))).reshape(b, s, SSD_GROUPS, -1)
    yg = yg * lax.rsqrt(jnp.mean(jnp.square(yg), axis=-1, keepdims=True) + RMS_EPS)
    return (yg.reshape(b, s, SSD_INNER) * norm_g).astype(z.dtype)


def gmlp_mixer(u, v, ln_g, ln_b, w_s, b_s):
    u = jax.nn.gelu(u)
    v = layer_norm(jax.nn.gelu(v), ln_g, ln_b)
    b, s, w = v.shape
    nc = s // GMLP_CHUNK
    vc = v.reshape(b, nc, GMLP_CHUNK, GMLP_GROUPS, w // GMLP_GROUPS)
    w_causal = jnp.tril(w_s)
    mixed = jnp.einsum("gts,bcsgd->bctgd", w_causal, vc) + b_s.T[None, None, :, :, None]
    return u * mixed.reshape(b, s, w).astype(u.dtype)


def rglru_mixer(gate, xr, conv_w, conv_b, w_a, b_a, w_x, b_x, lam):
    xr = causal_conv(xr, conv_w, conv_b)
    b, s, w = xr.shape
    xblk = xr.reshape(b, s, LRU_BLOCKS, w // LRU_BLOCKS)
    r = jax.nn.sigmoid(jnp.einsum("bshi,hij->bshj", xblk, w_a).reshape(b, s, w) + b_a)
    i = jax.nn.sigmoid(jnp.einsum("bshi,hij->bshj", xblk, w_x).reshape(b, s, w) + b_x)
    log_a = -LRU_C * r.astype(jnp.float32) * jax.nn.softplus(-lam.astype(jnp.float32))
    a = jnp.exp(log_a)
    mult = jnp.sqrt(-jnp.expm1(2.0 * log_a))
    bx = mult * (i * xr).astype(jnp.float32)

    def combine(p, q):
        a1, b1 = p
        a2, b2 = q
        return a1 * a2, a2 * b1 + b2

    _, h = lax.associative_scan(combine, (a, bx), axis=1)
    return (h * jax.nn.gelu(gate.astype(jnp.float32))).astype(gate.dtype)


def moe(h, w_router, b_router, w1, b1, w2, b2):
    b, s, d = h.shape
    t = b * s
    tk = t * TOP_K
    hf = h.reshape(t, d)
    logits = (hf @ w_router).astype(jnp.float32) + b_router.astype(jnp.float32)
    top_logit, top_e = lax.top_k(logits, TOP_K)
    top_w = jax.nn.softmax(top_logit, axis=-1).astype(h.dtype)
    flat_e = top_e.reshape(tk)
    flat_tok = jnp.arange(tk, dtype=jnp.int32) // TOP_K
    order = jnp.argsort(flat_e, stable=True)
    sorted_e = flat_e[order]
    counts = jnp.bincount(flat_e, length=N_EXPERTS)
    padded = (counts + EXPERT_BLOCK - 1) // EXPERT_BLOCK * EXPERT_BLOCK
    pad_end = jnp.cumsum(padded)
    pad_start = pad_end - padded
    start = jnp.cumsum(counts) - counts
    dest = pad_start[sorted_e] + jnp.arange(tk, dtype=jnp.int32) - start[sorted_e]
    n_blocks = tk // EXPERT_BLOCK + N_EXPERTS
    n_rows = n_blocks * EXPERT_BLOCK
    row_tok = jnp.full((n_rows,), t, dtype=jnp.int32).at[dest].set(flat_tok[order])
    row_w = jnp.zeros((n_rows,), h.dtype).at[dest].set(top_w.reshape(tk)[order])
    block_start = jnp.arange(n_blocks, dtype=jnp.int32) * EXPERT_BLOCK
    block_e = jnp.minimum(jnp.searchsorted(pad_end, block_start, side="right"), N_EXPERTS - 1)
    h_pad = jnp.concatenate([hf, jnp.zeros((1, d), hf.dtype)], axis=0)

    def expert_block(args):
        tok, e = args
        xb = h_pad[tok]
        gu = xb @ w1[e] + b1[e]
        glu = jnp.minimum(gu[:, 0::2], SWIGLU_LIMIT)
        lin = jnp.clip(gu[:, 1::2], -SWIGLU_LIMIT, SWIGLU_LIMIT)
        act = glu * jax.nn.sigmoid(SWIGLU_ALPHA * glu) * (lin + 1.0)
        return act @ w2[e] + b2[e]

    out = lax.map(expert_block, (row_tok.reshape(n_blocks, EXPERT_BLOCK), block_e))
    out = out.reshape(n_rows, d) * row_w[:, None]
    y = jnp.zeros((t + 1, d), out.dtype).at[row_tok].add(out)[:t]
    return y.reshape(b, s, d).astype(h.dtype)


def setup_inputs(seed: int = 0) -> dict:
    key = jax.random.key(seed)
    ks = iter(jax.random.split(key, 48))
    f32 = jnp.float32
    L = DEPTH

    def nrm(shape, scale):
        return jax.random.normal(next(ks), shape, f32) * scale

    def gain(shape):
        return 1.0 + nrm(shape, 0.02)

    x = nrm((BATCH, SEQ, D_MODEL), 1.0)
    ln_emb_g = gain((D_MODEL,))
    ln_emb_b = nrm((D_MODEL,), 0.02)
    w_in = nrm((L, D_MODEL, D_IN), D_MODEL ** -0.5)
    b_merge = nrm((L, N_BRANCHES * D_MODEL), 0.1)
    conv_a_w = nrm((L, CONV_WIDTH, SSD_CONV_DIM), CONV_WIDTH ** -0.5)
    conv_a_b = nrm((L, SSD_CONV_DIM), 0.02)
    dt0 = jnp.exp(jax.random.uniform(next(ks), (L, SSD_HEADS), f32, math.log(1e-3), math.log(1e-1)))
    dt_bias = dt0 + jnp.log(-jnp.expm1(-dt0))
    a_log = jnp.log(jax.random.uniform(next(ks), (L, SSD_HEADS), f32, 1.0, 16.0))
    d_skip = gain((L, SSD_HEADS))
    norm_a_g = gain((L, SSD_INNER))
    ln_v_g = gain((L, GMLP_WIDTH))
    ln_v_b = nrm((L, GMLP_WIDTH), 0.02)
    w_spatial = nrm((L, GMLP_GROUPS, GMLP_CHUNK, GMLP_CHUNK), GMLP_CHUNK ** -0.5)
    b_spatial = 1.0 + nrm((L, GMLP_GROUPS, GMLP_CHUNK), 0.02)
    conv_c_w = nrm((L, CONV_WIDTH, LRU_WIDTH), CONV_WIDTH ** -0.5)
    conv_c_b = nrm((L, LRU_WIDTH), 0.02)
    blk = LRU_WIDTH // LRU_BLOCKS
    w_rg_a = nrm((L, LRU_BLOCKS, blk, blk), blk ** -0.5)
    b_rg_a = nrm((L, LRU_WIDTH), 0.02)
    w_rg_x = nrm((L, LRU_BLOCKS, blk, blk), blk ** -0.5)
    b_rg_x = nrm((L, LRU_WIDTH), 0.02)
    a0 = jax.random.uniform(next(ks), (L, LRU_WIDTH), f32, 0.9, 0.999)
    a_c = a0 ** (1.0 / LRU_C)
    lam = jnp.log(a_c) - jnp.log1p(-a_c)
    p_a = nrm((L, SSD_INNER, D_MODEL), SSD_INNER ** -0.5 * DN_BETA)
    p_b = nrm((L, GMLP_WIDTH, D_MODEL), GMLP_WIDTH ** -0.5 * DN_BETA)
    p_c = nrm((L, LRU_WIDTH, D_MODEL), LRU_WIDTH ** -0.5 * DN_BETA)
    w_o = nrm((L, D_MODEL, D_MODEL), D_MODEL ** -0.5 * DN_BETA)
    ln_mix_g = gain((L, D_MODEL))
    ln_mix_b = nrm((L, D_MODEL), 0.02)
    w_router = nrm((L, D_MODEL, N_EXPERTS), D_MODEL ** -0.5)
    b_router = nrm((L, N_EXPERTS), 0.01)
    w1 = nrm((L, N_EXPERTS, D_MODEL, 2 * D_EXPERT), D_MODEL ** -0.5 * DN_BETA)
    b1 = nrm((L, N_EXPERTS, 2 * D_EXPERT), 0.02)
    w2 = nrm((L, N_EXPERTS, D_EXPERT, D_MODEL), D_EXPERT ** -0.5 * DN_BETA)
    b2 = nrm((L, N_EXPERTS, D_MODEL), 0.02)
    ln_ffn_g = gain((L, D_MODEL))
    ln_ffn_b = nrm((L, D_MODEL), 0.02)
    return {"x": x, "ln_emb_g": ln_emb_g, "ln_emb_b": ln_emb_b, "w_in": w_in, "b_merge": b_merge,
            "conv_a_w": conv_a_w, "conv_a_b": conv_a_b, "dt_bias": dt_bias, "a_log": a_log,
            "d_skip": d_skip, "norm_a_g": norm_a_g, "ln_v_g": ln_v_g, "ln_v_b": ln_v_b,
            "w_spatial": w_spatial, "b_spatial": b_spatial, "conv_c_w": conv_c_w, "conv_c_b": conv_c_b,
            "w_rg_a": w_rg_a, "b_rg_a": b_rg_a, "w_rg_x": w_rg_x, "b_rg_x": b_rg_x, "lam": lam,
            "p_a": p_a, "p_b": p_b, "p_c": p_c, "w_o": w_o, "ln_mix_g": ln_mix_g, "ln_mix_b": ln_mix_b,
            "w_router": w_router, "b_router": b_router, "w1": w1, "b1": b1, "w2": w2, "b2": b2,
            "ln_ffn_g": ln_ffn_g, "ln_ffn_b": ln_ffn_b}


def reference(x, ln_emb_g, ln_emb_b, w_in, b_merge, conv_a_w, conv_a_b, dt_bias, a_log, d_skip, norm_a_g,
              ln_v_g, ln_v_b, w_spatial, b_spatial, conv_c_w, conv_c_b, w_rg_a, b_rg_a, w_rg_x, b_rg_x, lam,
              p_a, p_b, p_c, w_o, ln_mix_g, ln_mix_b, w_router, b_router, w1, b1, w2, b2, ln_ffn_g, ln_ffn_b):
    x = layer_norm(x, ln_emb_g, ln_emb_b)
    for l in range(DEPTH):
        proj = x @ w_in[l]
        z_a, xbc_a, dt_a, u_b, v_b, gate_c, x_c, gate_logits = jnp.split(proj, IN_OFFSETS, axis=-1)
        y_a = ssd_mixer(z_a, xbc_a, dt_a, conv_a_w[l], conv_a_b[l], dt_bias[l], a_log[l], d_skip[l], norm_a_g[l])
        y_b = gmlp_mixer(u_b, v_b, ln_v_g[l], ln_v_b[l], w_spatial[l], b_spatial[l])
        y_c = rglru_mixer(gate_c, x_c, conv_c_w[l], conv_c_b[l], w_rg_a[l], b_rg_a[l], w_rg_x[l], b_rg_x[l], lam[l])
        g_a, g_b, g_c = jnp.split(jax.nn.sigmoid(gate_logits + b_merge[l]), N_BRANCHES, axis=-1)
        merged = g_a * (y_a @ p_a[l]) + g_b * (y_b @ p_b[l]) + g_c * (y_c @ p_c[l])
        x = layer_norm(DN_ALPHA * x + merged @ w_o[l], ln_mix_g[l], ln_mix_b[l])
        ffn = moe(x, w_router[l], b_router[l], w1[l], b1[l], w2[l], b2[l])
        x = layer_norm(DN_ALPHA * x + ffn, ln_ffn_g[l], ln_ffn_b[l])
    return x
```

```python
import functools
import math

import jax
import jax.numpy as jnp
from jax import lax
from jax.experimental import pallas as pl
from jax.experimental.pallas import tpu as pltpu

F32 = jnp.float32
BF16 = jnp.bfloat16
U32 = jnp.uint32
I32 = jnp.int32

SSD_GROUPS = 4
SSD_STATE = 128
CHUNK = 128
GMLP_GROUPS = 8
LRU_C = 8.0
TOP_K = 4
SWIGLU_LIMIT = 7.0
SWIGLU_ALPHA = 1.702
LN_EPS = 1e-5
RMS_EPS = 1e-5

LANES = 128
SUBLANES = 8
VMEM_LIMIT = 56 * 1024 * 1024
EXPERT_ROWS = 256
NEG_BIG = -1e30


def _cparams(*sem):
    return pltpu.CompilerParams(dimension_semantics=sem, vmem_limit_bytes=VMEM_LIMIT)


def _dot(a, b):
    return jnp.dot(a, b, preferred_element_type=F32)


def _split3(x):
    hi = x.astype(BF16)
    r = x - hi.astype(F32)
    mid = r.astype(BF16)
    lo = (r - mid.astype(F32)).astype(BF16)
    return hi, mid, lo


def _exact_right(x, m):
    hi, mid, lo = _split3(x)
    return _dot(hi, m) + (_dot(mid, m) + _dot(lo, m))


def _exact_left(m, x):
    hi, mid, lo = _split3(x)
    return _dot(m, hi) + (_dot(m, mid) + _dot(m, lo))


def _sigmoid(x):
    return 1.0 / (1.0 + jnp.exp(-x))


def _silu(x):
    return x * _sigmoid(x)


def _softplus(x):
    return jnp.maximum(x, 0.0) + jnp.log1p(jnp.exp(-jnp.abs(x)))


def _gelu(x):
    c = math.sqrt(2.0 / math.pi)
    return x * (0.5 * (1.0 + jnp.tanh(c * (x + 0.044715 * (x * x * x)))))


def _ln_rows(x, g, b):
    mu = jnp.mean(x, axis=-1, keepdims=True)
    xc = x - mu
    var = jnp.mean(xc * xc, axis=-1, keepdims=True)
    return xc * lax.rsqrt(var + LN_EPS) * g + b


def _causal_conv(x, carry, w, b):
    kw = w.shape[0]
    rows = x.shape[0]
    xp = jnp.concatenate([carry, x], axis=0)
    y = x * w[kw - 1:kw, :] + b
    for k in range(kw - 1):
        s = SUBLANES - (kw - 1) + k
        y = y + xp[s:s + rows, :] * w[k:k + 1, :]
    return y


def _pack_halves(y):
    half = y.shape[1] // 2
    lo = pltpu.bitcast(y[:, :half].astype(BF16).astype(F32), U32)
    hi = pltpu.bitcast(y[:, half:].astype(BF16).astype(F32), U32)
    return (hi & jnp.uint32(0xFFFF0000)) | (lo >> 16)


def _unpack_halves(u):
    lo = pltpu.bitcast(u << 16, F32).astype(BF16)
    hi = pltpu.bitcast(u & jnp.uint32(0xFFFF0000), F32).astype(BF16)
    return lo, hi


def _ln0_kernel(x_ref, g_ref, b_ref, o_ref, ob_ref):
    y = _ln_rows(x_ref[...], g_ref[...], b_ref[...])
    o_ref[...] = y
    ob_ref[...] = y.astype(BF16)


def _ln0(x, g, b):
    s, d = x.shape
    tm = min(256, s)
    return pl.pallas_call(
        _ln0_kernel,
        out_shape=(jax.ShapeDtypeStruct((s, d), F32), jax.ShapeDtypeStruct((s, d), BF16)),
        grid=(s // tm,),
        in_specs=[pl.BlockSpec((tm, d), lambda i: (i, 0)),
                  pl.BlockSpec((1, d), lambda i: (0, 0)),
                  pl.BlockSpec((1, d), lambda i: (0, 0))],
        out_specs=(pl.BlockSpec((tm, d), lambda i: (i, 0)),
                   pl.BlockSpec((tm, d), lambda i: (i, 0))),
        compiler_params=_cparams("parallel"),
        name="ln0",
    )(x, g.reshape(1, d), b.reshape(1, d))


def _mm_kernel(a_ref, b_ref, o_ref):
    o_ref[...] = _dot(a_ref[...], b_ref[...]).astype(o_ref.dtype)


def _mm_res_kernel(a_ref, b_ref, r_ref, o_ref, *, alpha):
    o_ref[...] = (alpha * r_ref[...] + _dot(a_ref[...], b_ref[...])).astype(o_ref.dtype)


def _matmul(a, b, out_dtype, *, tm=1024, tn=512, res=None, alpha=1.0, name="mm"):
    m, k = a.shape
    n = b.shape[1]
    tm = min(tm, m)
    tn = min(tn, n)
    in_specs = [pl.BlockSpec((tm, k), lambda j, i: (i, 0)),
                pl.BlockSpec((k, tn), lambda j, i: (0, j))]
    args = [a, b]
    if res is None:
        kern = _mm_kernel
    else:
        kern = functools.partial(_mm_res_kernel, alpha=alpha)
        in_specs.append(pl.BlockSpec((tm, tn), lambda j, i: (i, j)))
        args.append(res)
    return pl.pallas_call(
        kern,
        out_shape=jax.ShapeDtypeStruct((m, n), out_dtype),
        grid=(n // tn, m // tm),
        in_specs=in_specs,
        out_specs=pl.BlockSpec((tm, tn), lambda j, i: (i, j)),
        compiler_params=_cparams("parallel", "parallel"),
        name=name,
    )(*args)


def _ssd_kernel(z_ref, xs_ref, bc_ref, dt_ref, cwx_ref, cbx_ref, cwb_ref, cbb_ref, dtb_ref, alog_ref,
                dskip_ref, ng_ref, e_ref, o_ref, cx_ref, cb_ref, st_ref, *, heads, head_dim):
    i = pl.program_id(0)

    @pl.when(i == 0)
    def _():
        cx_ref[...] = jnp.zeros_like(cx_ref)
        cb_ref[...] = jnp.zeros_like(cb_ref)
        st_ref[...] = jnp.zeros_like(st_ref)

    n = SSD_STATE
    gw = (heads // SSD_GROUPS) * head_dim
    hpg = heads // SSD_GROUPS
    pair = LANES // head_dim

    xs_raw = xs_ref[...].astype(F32)
    bc_raw = bc_ref[...].astype(F32)
    xs = _silu(_causal_conv(xs_raw, cx_ref[...], cwx_ref[...], cbx_ref[...]))
    bc = _silu(_causal_conv(bc_raw, cb_ref[...], cwb_ref[...], cbb_ref[...]))
    cx_ref[...] = xs_raw[CHUNK - SUBLANES:, :]
    cb_ref[...] = bc_raw[CHUNK - SUBLANES:, :]

    dt = _softplus(dt_ref[...] + dtb_ref[...])
    da = dt * (-jnp.exp(alog_ref[...]))
    row = lax.broadcasted_iota(I32, (CHUNK, CHUNK), 0)
    col = lax.broadcasted_iota(I32, (CHUNK, CHUNK), 1)
    causal = row >= col
    ltri = jnp.where(causal, 1.0, 0.0).astype(BF16)
    cum = _exact_left(ltri, da)
    cum_t = cum.T
    expand = e_ref[...]
    cum_e = _exact_right(cum, expand)
    dt_e = _exact_right(dt, expand)
    last_e = cum_e[CHUNK - 1:CHUNK, :]
    decay_out = jnp.exp(cum_e)
    decay_st = jnp.exp(last_e - cum_e)
    chunk_decay = jnp.exp(last_e)
    xdt = xs * dt_e
    xdt_b = xdt.astype(BF16)
    xst_b = (xdt * decay_st).astype(BF16)
    lane = lax.broadcasted_iota(I32, (CHUNK, LANES), 1)

    for g in range(SSD_GROUPS):
        bg = bc[:, g * n:(g + 1) * n].astype(BF16)
        cg = bc[:, (SSD_GROUPS + g) * n:(SSD_GROUPS + g + 1) * n].astype(BF16)
        cb = lax.dot_general(cg, bg, (((1,), (1,)), ((), ())), preferred_element_type=F32)
        slabs = []
        for j in range(hpg // pair):
            h0 = g * hpg + j * pair
            sc = []
            for h in range(h0, h0 + pair):
                seg = cum[:, h:h + 1] - cum_t[h:h + 1, :]
                dec = jnp.exp(jnp.where(causal, seg, -jnp.inf))
                sc.append((cb * dec).astype(BF16))
            xp = xdt_b[:, h0 * head_dim:h0 * head_dim + LANES]
            blocks = [jnp.where((lane >= q * head_dim) & (lane < (q + 1) * head_dim), xp, jnp.zeros_like(xp))
                      for q in range(pair)]
            slabs.append(_dot(jnp.concatenate(sc, axis=1), jnp.concatenate(blocks, axis=0)))
        y_diag = jnp.concatenate(slabs, axis=1)
        cs = slice(g * gw, (g + 1) * gw)
        st = st_ref[g]
        y_off = _dot(cg, st.astype(BF16)) * decay_out[:, cs]
        st_new = lax.dot_general(bg, xst_b[:, cs], (((0,), (0,)), ((), ())), preferred_element_type=F32)
        st_ref[g] = st * chunk_decay[:, cs] + st_new
        y = y_diag + y_off + xs[:, cs] * dskip_ref[:, cs]
        yg = y * _silu(z_ref[:, cs].astype(F32))
        ms = jnp.mean(yg * yg, axis=-1, keepdims=True)
        o_ref[:, cs] = (yg * lax.rsqrt(ms + RMS_EPS) * ng_ref[:, cs]).astype(o_ref.dtype)


def _ssd(proj, dt_raw, lay, conv_w, conv_b, dt_bias, a_log, d_skip, norm_g):
    s = proj.shape[0]
    heads = dt_bias.shape[0]
    inner = norm_g.shape[0]
    head_dim = inner // heads
    bcw = 2 * SSD_GROUPS * SSD_STATE
    gw = inner // SSD_GROUPS
    pad = LANES - heads
    expand = (jnp.arange(LANES)[:, None] == (jnp.arange(inner) // head_dim)[None, :]).astype(BF16)
    kern = functools.partial(_ssd_kernel, heads=heads, head_dim=head_dim)
    const = lambda shape: pl.BlockSpec(shape, lambda i: (0,) * len(shape))
    return pl.pallas_call(
        kern,
        out_shape=jax.ShapeDtypeStruct((s, inner), BF16),
        grid=(s // CHUNK,),
        in_specs=[pl.BlockSpec((CHUNK, inner), lambda i: (i, lay["z"] // inner)),
                  pl.BlockSpec((CHUNK, inner), lambda i: (i, lay["xs"] // inner)),
                  pl.BlockSpec((CHUNK, bcw), lambda i: (i, lay["bc"] // bcw)),
                  pl.BlockSpec((CHUNK, LANES), lambda i: (i, 0)),
                  const((conv_w.shape[0], inner)), const((1, inner)),
                  const((conv_w.shape[0], bcw)), const((1, bcw)),
                  const((1, LANES)), const((1, LANES)),
                  const((1, inner)), const((1, inner)), const((LANES, inner))],
        out_specs=pl.BlockSpec((CHUNK, inner), lambda i: (i, 0)),
        scratch_shapes=[pltpu.VMEM((SUBLANES, inner), F32), pltpu.VMEM((SUBLANES, bcw), F32),
                        pltpu.VMEM((SSD_GROUPS, SSD_STATE, gw), F32)],
        compiler_params=_cparams("arbitrary"),
        name="ssd",
    )(proj, proj, proj, dt_raw,
      conv_w[:, :inner], conv_b[:inner].reshape(1, inner),
      conv_w[:, inner:], conv_b[inner:].reshape(1, bcw),
      jnp.pad(dt_bias, (0, pad)).reshape(1, LANES), jnp.pad(a_log, (0, pad)).reshape(1, LANES),
      jnp.repeat(d_skip, head_dim).reshape(1, inner), norm_g.reshape(1, inner), expand)


def _gmlp_kernel(u_ref, v_ref, lg_ref, lb_ref, ws_ref, be_ref, o_ref, *, chunks):
    width = u_ref.shape[1]
    gw = width // GMLP_GROUPS
    row = lax.broadcasted_iota(I32, (CHUNK, CHUNK), 0)
    col = lax.broadcasted_iota(I32, (CHUNK, CHUNK), 1)
    tril = row >= col
    ws = [jnp.where(tril, ws_ref[g], 0.0).astype(BF16) for g in range(GMLP_GROUPS)]
    for c in range(chunks):
        rs = slice(c * CHUNK, (c + 1) * CHUNK)
        v = _ln_rows(_gelu(v_ref[rs, :].astype(F32)), lg_ref[...], lb_ref[...]).astype(BF16)
        for g in range(GMLP_GROUPS):
            cs = slice(g * gw, (g + 1) * gw)
            mixed = _dot(ws[g], v[:, cs]) + be_ref[:, cs]
            o_ref[rs, cs] = (_gelu(u_ref[rs, cs].astype(F32)) * mixed).astype(o_ref.dtype)


def _gmlp(proj, lay, ln_g, ln_b, w_s, b_s, *, rows=256):
    s = proj.shape[0]
    width = ln_g.shape[0]
    gw = width // GMLP_GROUPS
    rows = min(rows, s)
    bias_e = jnp.repeat(b_s.T, gw, axis=1)
    kern = functools.partial(_gmlp_kernel, chunks=rows // CHUNK)
    const = lambda shape: pl.BlockSpec(shape, lambda i: (0,) * len(shape))
    return pl.pallas_call(
        kern,
        out_shape=jax.ShapeDtypeStruct((s, width), BF16),
        grid=(s // rows,),
        in_specs=[pl.BlockSpec((rows, width), lambda i: (i, lay["u"] // width)),
                  pl.BlockSpec((rows, width), lambda i: (i, lay["v"] // width)),
                  const((1, width)), const((1, width)),
                  const((GMLP_GROUPS, CHUNK, CHUNK)), const((CHUNK, width))],
        out_specs=pl.BlockSpec((rows, width), lambda i: (i, 0)),
        compiler_params=_cparams("parallel"),
        name="gmlp",
    )(proj, proj, ln_g.reshape(1, width), ln_b.reshape(1, width), w_s, bias_e)


def _lru_kernel(gate_ref, x_ref, cw_ref, cb_ref, wa_ref, ba_ref, wx_ref, bx_ref, lam_ref, o_ref,
                carry_ref, h_ref):
    i = pl.program_id(0)

    @pl.when(i == 0)
    def _():
        carry_ref[...] = jnp.zeros_like(carry_ref)
        h_ref[...] = jnp.zeros_like(h_ref)

    rows, width = x_ref.shape
    nblk, blk = wa_ref.shape[0], wa_ref.shape[1]
    x_raw = x_ref[...].astype(F32)
    xr = _causal_conv(x_raw, carry_ref[...], cw_ref[...], cb_ref[...])
    carry_ref[...] = x_raw[rows - SUBLANES:, :]
    xb = xr.astype(BF16)
    ra = jnp.concatenate([_dot(xb[:, q * blk:(q + 1) * blk], wa_ref[q]) for q in range(nblk)], axis=1)
    ia = jnp.concatenate([_dot(xb[:, q * blk:(q + 1) * blk], wx_ref[q]) for q in range(nblk)], axis=1)
    r = _sigmoid(ra + ba_ref[...])
    ig = _sigmoid(ia + bx_ref[...])
    log_a = (-LRU_C) * r * _softplus(-lam_ref[...])
    a = jnp.exp(log_a)
    b = jnp.sqrt((1.0 - a) * (1.0 + a)) * (ig * xr)
    ridx = lax.broadcasted_iota(I32, (rows, width), 0)
    d = 1
    while d < rows:
        keep = ridx >= d
        a_s = pltpu.roll(a, d, 0)
        b_s = pltpu.roll(b, d, 0)
        b = jnp.where(keep, a * b_s + b, b)
        a = jnp.where(keep, a * a_s, a)
        d *= 2
    h = a * h_ref[...] + b
    h_ref[...] = h[rows - 1:rows, :]
    o_ref[...] = (h * _gelu(gate_ref[...].astype(F32))).astype(o_ref.dtype)


def _lru(proj, lay, conv_w, conv_b, w_a, b_a, w_x, b_x, lam, *, rows=128):
    s = proj.shape[0]
    width = lam.shape[0]
    nblk, blk = w_a.shape[0], w_a.shape[1]
    rows = min(rows, s)
    const = lambda shape: pl.BlockSpec(shape, lambda i: (0,) * len(shape))
    return pl.pallas_call(
        _lru_kernel,
        out_shape=jax.ShapeDtypeStruct((s, width), BF16),
        grid=(s // rows,),
        in_specs=[pl.BlockSpec((rows, width), lambda i: (i, lay["gate_c"] // width)),
                  pl.BlockSpec((rows, width), lambda i: (i, lay["x_c"] // width)),
                  const((conv_w.shape[0], width)), const((1, width)),
                  const((nblk, blk, blk)), const((1, width)),
                  const((nblk, blk, blk)), const((1, width)), const((1, width))],
        out_specs=pl.BlockSpec((rows, width), lambda i: (i, 0)),
        scratch_shapes=[pltpu.VMEM((SUBLANES, width), F32), pltpu.VMEM((1, width), F32)],
        compiler_params=_cparams("arbitrary"),
        name="rglru",
    )(proj, proj, conv_w, conv_b.reshape(1, width), w_a.astype(BF16), b_a.reshape(1, width),
      w_x.astype(BF16), b_x.reshape(1, width), lam.reshape(1, width))


def _merge_kernel(ya_ref, yb_ref, yc_ref, pa_ref, pb_ref, pc_ref, ga_ref, gb_ref, gc_ref,
                  ba_ref, bb_ref, bc_ref, o_ref):
    acc = _sigmoid(ga_ref[...].astype(F32) + ba_ref[...]) * _dot(ya_ref[...], pa_ref[...])
    acc = acc + _sigmoid(gb_ref[...].astype(F32) + bb_ref[...]) * _dot(yb_ref[...], pb_ref[...])
    acc = acc + _sigmoid(gc_ref[...].astype(F32) + bc_ref[...]) * _dot(yc_ref[...], pc_ref[...])
    o_ref[...] = acc.astype(o_ref.dtype)


def _merge(y_a, y_b, y_c, p_a, p_b, p_c, proj, lay, b_merge, *, tm=512, tn=512):
    s, w = y_a.shape
    d = p_a.shape[1]
    tm = min(tm, s)
    tn = min(tn, d)
    g0 = lay["gates"] // tn
    nd = d // tn
    bm = b_merge.reshape(1, 3 * d)
    y_spec = pl.BlockSpec((tm, w), lambda j, i: (i, 0))
    p_spec = pl.BlockSpec((w, tn), lambda j, i: (0, j))

    def g_spec(br):
        return pl.BlockSpec((tm, tn), lambda j, i: (i, g0 + br * nd + j))

    def b_spec(br):
        return pl.BlockSpec((1, tn), lambda j, i: (0, br * nd + j))

    return pl.pallas_call(
        _merge_kernel,
        out_shape=jax.ShapeDtypeStruct((s, d), BF16),
        grid=(d // tn, s // tm),
        in_specs=[y_spec, y_spec, y_spec, p_spec, p_spec, p_spec,
                  g_spec(0), g_spec(1), g_spec(2), b_spec(0), b_spec(1), b_spec(2)],
        out_specs=pl.BlockSpec((tm, tn), lambda j, i: (i, j)),
        compiler_params=_cparams("parallel", "parallel"),
        name="merge",
    )(y_a, y_b, y_c, p_a, p_b, p_c, proj, proj, proj, bm, bm, bm)


def _ln_router_kernel(h_ref, g_ref, b_ref, wr_ref, br_ref, x_ref, xp_ref, te_ref, tw_ref, rk_ref, cnt_ref,
                      carry_ref):
    i = pl.program_id(0)

    @pl.when(i == 0)
    def _():
        carry_ref[...] = jnp.zeros_like(carry_ref)

    rows = h_ref.shape[0]
    y = _ln_rows(h_ref[...], g_ref[...], b_ref[...])
    x_ref[...] = y
    xp_ref[...] = _pack_halves(y)
    yh = y.astype(BF16)
    yl = (y - yh.astype(F32)).astype(BF16)
    wr = wr_ref[...]
    wh = wr.astype(BF16)
    wl = (wr - wh.astype(F32)).astype(BF16)
    logits = _dot(yh, wh) + (_dot(yh, wl) + _dot(yl, wh)) + br_ref[...]
    lane = lax.broadcasted_iota(I32, (rows, LANES), 1)
    lane_f = lane.astype(F32)
    cur = logits
    vals, sels, idxs = [], [], []
    for _ in range(TOP_K):
        m = jnp.max(cur, axis=-1, keepdims=True)
        idx = jnp.min(jnp.where(cur == m, lane_f, float(LANES)), axis=-1, keepdims=True)
        sel = lane_f == idx
        vals.append(m)
        sels.append(sel)
        idxs.append(idx)
        cur = jnp.where(sel, -jnp.inf, cur)
    ex = [jnp.exp(v - vals[0]) for v in vals]
    den = ex[0]
    for e in ex[1:]:
        den = den + e
    multi = jnp.zeros((rows, LANES), F32)
    for sel in sels:
        multi = multi + jnp.where(sel, 1.0, 0.0)
    r2 = lax.broadcasted_iota(I32, (rows, rows), 0)
    c2 = lax.broadcasted_iota(I32, (rows, rows), 1)
    lower = jnp.where(r2 > c2, 1.0, 0.0).astype(BF16)
    before = _dot(lower, multi.astype(BF16)) + carry_ref[...]
    te = jnp.zeros((rows, LANES), F32)
    tw = jnp.zeros((rows, LANES), F32)
    rk = jnp.zeros((rows, LANES), F32)
    for k in range(TOP_K):
        rank_k = jnp.sum(jnp.where(sels[k], before, 0.0), axis=-1, keepdims=True)
        te = jnp.where(lane == k, idxs[k], te)
        tw = jnp.where(lane == k, ex[k] / den, tw)
        rk = jnp.where(lane == k, rank_k, rk)
    te_ref[...] = te.astype(I32)
    tw_ref[...] = tw
    rk_ref[...] = rk.astype(I32)
    carry_ref[...] = carry_ref[...] + jnp.sum(multi, axis=0, keepdims=True)
    cnt_ref[...] = carry_ref[...].astype(I32)


def _ln_router(h, g, b, w_router, b_router, *, tm=256):
    s, d = h.shape
    ne = w_router.shape[1]
    tm = min(tm, s)
    wr = jnp.pad(w_router, ((0, 0), (0, LANES - ne)))
    br = jnp.pad(b_router, (0, LANES - ne), constant_values=NEG_BIG).reshape(1, LANES)
    row = lambda width: pl.BlockSpec((tm, width), lambda i: (i, 0))
    const = lambda shape: pl.BlockSpec(shape, lambda i: (0,) * len(shape))
    return pl.pallas_call(
        _ln_router_kernel,
        out_shape=(jax.ShapeDtypeStruct((s, d), F32), jax.ShapeDtypeStruct((s, d // 2), U32),
                   jax.ShapeDtypeStruct((s, LANES), I32), jax.ShapeDtypeStruct((s, LANES), F32),
                   jax.ShapeDtypeStruct((s, LANES), I32), jax.ShapeDtypeStruct((1, LANES), I32)),
        grid=(s // tm,),
        in_specs=[row(d), const((1, d)), const((1, d)), const((d, LANES)), const((1, LANES))],
        out_specs=(row(d), row(d // 2), row(LANES), row(LANES), row(LANES), const((1, LANES))),
        scratch_shapes=[pltpu.VMEM((1, LANES), F32)],
        compiler_params=_cparams("arbitrary"),
        name="ln_router",
    )(h, g.reshape(1, d), b.reshape(1, d), wr, br)


def _row_copy(src_hbm, dst_hbm, sem, src_row, dst_row):
    return pltpu.make_async_copy(src_hbm.at[pl.ds(src_row, 1)], dst_hbm.at[pl.ds(dst_row, 1)], sem)


def _dispatch_kernel(dest_ref, x_hbm, init_hbm, o_hbm, sem, *, tokens):
    del init_hbm
    base = pl.program_id(0) * tokens

    def issue(t, c):
        for k in range(TOP_K):
            _row_copy(x_hbm, o_hbm, sem, base + t, dest_ref[0, 0, t * TOP_K + k]).start()
        return c

    lax.fori_loop(0, tokens, issue, 0)

    def drain(t, c):
        for k in range(TOP_K):
            _row_copy(x_hbm, o_hbm, sem, 0, 0).wait()
        return c

    lax.fori_loop(0, tokens, drain, 0)


def _dispatch(xp, dest, n_rows, *, tokens=256):
    s, w = xp.shape
    tokens = min(tokens, s)
    steps = s // tokens
    kern = functools.partial(_dispatch_kernel, tokens=tokens)
    return pl.pallas_call(
        kern,
        out_shape=jax.ShapeDtypeStruct((n_rows, w), xp.dtype),
        grid=(steps,),
        in_specs=[pl.BlockSpec((1, 1, tokens * TOP_K), lambda i: (i, 0, 0), memory_space=pltpu.SMEM),
                  pl.BlockSpec(memory_space=pl.ANY),
                  pl.BlockSpec(memory_space=pl.ANY)],
        out_specs=pl.BlockSpec(memory_space=pl.ANY),
        scratch_shapes=[pltpu.SemaphoreType.DMA],
        input_output_aliases={2: 0},
        compiler_params=_cparams("arbitrary"),
        name="dispatch",
    )(dest.reshape(steps, 1, tokens * TOP_K), xp, jnp.zeros((n_rows, w), xp.dtype))


def _expert_kernel(be_ref, nu_ref, x_ref, w1g_ref, w1l_ref, b1g_ref, b1l_ref, w2_ref, b2_ref, o_ref):
    del be_ref

    @pl.when(pl.program_id(0) < nu_ref[0])
    def _():
        lo, hi = _unpack_halves(x_ref[...])
        half = lo.shape[1]
        glu = _dot(lo, w1g_ref[0, :half, :]) + _dot(hi, w1g_ref[0, half:, :]) + b1g_ref[0]
        lin = _dot(lo, w1l_ref[0, :half, :]) + _dot(hi, w1l_ref[0, half:, :]) + b1l_ref[0]
        glu = jnp.minimum(glu, SWIGLU_LIMIT)
        lin = jnp.clip(lin, -SWIGLU_LIMIT, SWIGLU_LIMIT)
        act = glu * _sigmoid(SWIGLU_ALPHA * glu) * (lin + 1.0)
        o_ref[...] = _dot(act.astype(BF16), w2_ref[0]) + b2_ref[0]

    @pl.when(pl.program_id(0) >= nu_ref[0])
    def _():
        o_ref[...] = jnp.zeros_like(o_ref)


def _experts(xs, block_e, n_used, w1g, w1l, b1g, b1l, w2, b2):
    n_rows, half = xs.shape
    ne, d, f = w1g.shape
    n_blocks = n_rows // EXPERT_ROWS

    def blk(b, be, nu):
        return jnp.minimum(b, nu[0] - 1)

    grid_spec = pltpu.PrefetchScalarGridSpec(
        num_scalar_prefetch=2,
        grid=(n_blocks,),
        in_specs=[pl.BlockSpec((EXPERT_ROWS, half), lambda b, be, nu: (blk(b, be, nu), 0)),
                  pl.BlockSpec((1, d, f), lambda b, be, nu: (be[blk(b, be, nu)], 0, 0)),
                  pl.BlockSpec((1, d, f), lambda b, be, nu: (be[blk(b, be, nu)], 0, 0)),
                  pl.BlockSpec((1, 1, f), lambda b, be, nu: (be[blk(b, be, nu)], 0, 0)),
                  pl.BlockSpec((1, 1, f), lambda b, be, nu: (be[blk(b, be, nu)], 0, 0)),
                  pl.BlockSpec((1, f, d), lambda b, be, nu: (be[blk(b, be, nu)], 0, 0)),
                  pl.BlockSpec((1, 1, d), lambda b, be, nu: (be[blk(b, be, nu)], 0, 0))],
        out_specs=pl.BlockSpec((EXPERT_ROWS, d), lambda b, be, nu: (b, 0)),
    )
    return pl.pallas_call(
        _expert_kernel,
        out_shape=jax.ShapeDtypeStruct((n_rows, d), F32),
        grid_spec=grid_spec,
        compiler_params=_cparams("arbitrary"),
        name="experts",
    )(block_e, n_used, xs, w1g, w1l, b1g.reshape(ne, 1, f), b1l.reshape(ne, 1, f), w2, b2.reshape(ne, 1, d))


def _combine_kernel(dest_ref, tw_ref, x_ref, g_ref, b_ref, ys_hbm, o_ref, ob_ref, buf_ref, sem, *, alpha):
    tokens = x_ref.shape[0]

    def copy(t, k, dst_row):
        return pltpu.make_async_copy(ys_hbm.at[pl.ds(dst_row, 1)], buf_ref.at[k, pl.ds(t, 1)], sem)

    def issue(t, c):
        for k in range(TOP_K):
            copy(t, k, dest_ref[0, 0, t * TOP_K + k]).start()
        return c

    lax.fori_loop(0, tokens, issue, 0)

    def drain(t, c):
        for k in range(TOP_K):
            copy(0, k, 0).wait()
        return c

    lax.fori_loop(0, tokens, drain, 0)
    tw = tw_ref[...]
    y = alpha * x_ref[...]
    for k in range(TOP_K):
        y = y + tw[:, k:k + 1] * buf_ref[k]
    out = _ln_rows(y, g_ref[...], b_ref[...])
    o_ref[...] = out
    ob_ref[...] = out.astype(BF16)


def _combine(ys, dest, tw, x, g, b, alpha, *, tokens=128):
    s, d = x.shape
    tokens = min(tokens, s)
    steps = s // tokens
    kern = functools.partial(_combine_kernel, alpha=alpha)
    row = lambda width: pl.BlockSpec((tokens, width), lambda i: (i, 0))
    const = lambda shape: pl.BlockSpec(shape, lambda i: (0,) * len(shape))
    return pl.pallas_call(
        kern,
        out_shape=(jax.ShapeDtypeStruct((s, d), F32), jax.ShapeDtypeStruct((s, d), BF16)),
        grid=(steps,),
        in_specs=[pl.BlockSpec((1, 1, tokens * TOP_K), lambda i: (i, 0, 0), memory_space=pltpu.SMEM),
                  row(LANES), row(d), const((1, d)), const((1, d)),
                  pl.BlockSpec(memory_space=pl.ANY)],
        out_specs=(row(d), row(d)),
        scratch_shapes=[pltpu.VMEM((TOP_K, tokens, d), F32), pltpu.SemaphoreType.DMA],
        compiler_params=_cparams("arbitrary"),
        name="combine",
    )(dest.reshape(steps, 1, tokens * TOP_K), tw, x, g.reshape(1, d), b.reshape(1, d), ys)


def _in_layout(sizes):
    z, xbc, dt, u, v, gate_c, x_c, gates = sizes
    inner = z
    lay = {"z": 0, "xs": inner, "u": 2 * inner, "v": 3 * inner, "gate_c": 4 * inner, "x_c": 5 * inner,
           "bc": 6 * inner}
    lay["gates"] = lay["bc"] + (xbc - inner)
    lay["total"] = lay["gates"] + gates
    return lay


def _rearrange_w_in(w, sizes):
    offs = [0]
    for sz in sizes:
        offs.append(offs[-1] + sz)
    z, xbc, dt, u, v, gate_c, x_c, gates = [w[:, offs[i]:offs[i + 1]] for i in range(8)]
    inner = sizes[0]
    main = jnp.concatenate([z, xbc[:, :inner], u, v, gate_c, x_c, xbc[:, inner:], gates], axis=1).astype(BF16)
    dtw = jnp.pad(dt, ((0, 0), (0, LANES - sizes[2]))).astype(BF16)
    return main, dtw


def _moe_plan(te, rk, cnt, ne, n_blocks):
    counts = cnt[0, :ne]
    padded = (counts + EXPERT_ROWS - 1) // EXPERT_ROWS * EXPERT_ROWS
    pad_end = jnp.cumsum(padded)
    pad_start = pad_end - padded
    dest = (pad_start[te[:, :TOP_K]] + rk[:, :TOP_K]).astype(I32).reshape(-1)
    block_start = jnp.arange(n_blocks, dtype=I32) * EXPERT_ROWS
    block_e = jnp.minimum(jnp.searchsorted(pad_end, block_start, side="right"), ne - 1).astype(I32)
    n_used = (pad_end[-1] // EXPERT_ROWS).astype(I32).reshape(1)
    return dest, block_e, n_used


def kernel(x, ln_emb_g, ln_emb_b, w_in, b_merge, conv_a_w, conv_a_b, dt_bias, a_log, d_skip, norm_a_g,
           ln_v_g, ln_v_b, w_spatial, b_spatial, conv_c_w, conv_c_b, w_rg_a, b_rg_a, w_rg_x, b_rg_x, lam,
           p_a, p_b, p_c, w_o, ln_mix_g, ln_mix_b, w_router, b_router, w1, b1, w2, b2, ln_ffn_g, ln_ffn_b):
    bsz, seq, d = x.shape
    depth = w_in.shape[0]
    alpha = (2 * depth) ** 0.25
    heads = dt_bias.shape[1]
    inner = norm_a_g.shape[1]
    sizes = (inner, conv_a_w.shape[2], heads, ln_v_g.shape[1], ln_v_g.shape[1], lam.shape[1], lam.shape[1],
             3 * d)
    lay = _in_layout(sizes)
    ne = w_router.shape[2]
    s = bsz * seq
    n_rows = s * TOP_K + ne * EXPERT_ROWS
    n_blocks = n_rows // EXPERT_ROWS

    xf, xb = _ln0(x.reshape(s, d), ln_emb_g, ln_emb_b)
    for l in range(depth):
        w_main, w_dt = _rearrange_w_in(w_in[l], sizes)
        proj = _matmul(xb, w_main, F32, name="in_proj")
        dt_raw = _matmul(xb, w_dt, F32, name="dt_proj")
        y_a = _ssd(proj, dt_raw, lay, conv_a_w[l], conv_a_b[l], dt_bias[l], a_log[l], d_skip[l], norm_a_g[l])
        y_b = _gmlp(proj, lay, ln_v_g[l], ln_v_b[l], w_spatial[l], b_spatial[l])
        y_c = _lru(proj, lay, conv_c_w[l], conv_c_b[l], w_rg_a[l], b_rg_a[l], w_rg_x[l], b_rg_x[l], lam[l])
        merged = _merge(y_a, y_b, y_c, p_a[l].astype(BF16), p_b[l].astype(BF16), p_c[l].astype(BF16),
                        proj, lay, b_merge[l])
        h = _matmul(merged, w_o[l].astype(BF16), F32, res=xf, alpha=alpha, name="w_o")
        xm, xp, te, tw, rk, cnt = _ln_router(h, ln_mix_g[l], ln_mix_b[l], w_router[l], b_router[l])
        dest, block_e, n_used = _moe_plan(te, rk, cnt, ne, n_blocks)
        xs = _dispatch(xp, dest, n_rows)
        ys = _experts(xs, block_e, n_used,
                      w1[l][:, :, 0::2].astype(BF16), w1[l][:, :, 1::2].astype(BF16),
                      b1[l][:, 0::2], b1[l][:, 1::2], w2[l].astype(BF16), b2[l])
        xf, xb = _combine(ys, dest, tw, xm, ln_ffn_g[l], ln_ffn_b[l], alpha)
    return xf.reshape(bsz, seq, d)
```

```python
import functools
import math

import jax
import jax.numpy as jnp
from jax import lax
from jax.experimental import pallas as pl
from jax.experimental.pallas import tpu as pltpu

F32 = jnp.float32
BF16 = jnp.bfloat16
U32 = jnp.uint32
I32 = jnp.int32

SSD_GROUPS = 4
SSD_STATE = 128
CHUNK = 128
GMLP_GROUPS = 8
LRU_C = 8.0
TOP_K = 4
SWIGLU_LIMIT = 7.0
SWIGLU_ALPHA = 1.702
LN_EPS = 1e-5
RMS_EPS = 1e-5

LANES = 128
SUBLANES = 8
VMEM_LIMIT = 56 * 1024 * 1024
EXPERT_ROWS = 256
WINDOW_BLOCK = 1024
NEG_BIG = -1e30


def _cparams(*sem):
    return pltpu.CompilerParams(dimension_semantics=sem, vmem_limit_bytes=VMEM_LIMIT)


def _dot(a, b):
    return jnp.dot(a, b, preferred_element_type=F32)


def _split3(x):
    hi = x.astype(BF16)
    r = x - hi.astype(F32)
    mid = r.astype(BF16)
    lo = (r - mid.astype(F32)).astype(BF16)
    return hi, mid, lo


def _exact_right(x, m):
    hi, mid, lo = _split3(x)
    return _dot(hi, m) + (_dot(mid, m) + _dot(lo, m))


def _exact_left(m, x):
    hi, mid, lo = _split3(x)
    return _dot(m, hi) + (_dot(m, mid) + _dot(m, lo))


def _sigmoid(x):
    return 1.0 / (1.0 + jnp.exp(-x))


def _silu(x):
    return x * _sigmoid(x)


def _softplus(x):
    return jnp.maximum(x, 0.0) + jnp.log1p(jnp.exp(-jnp.abs(x)))


def _gelu(x):
    c = math.sqrt(2.0 / math.pi)
    return x * (0.5 * (1.0 + jnp.tanh(c * (x + 0.044715 * (x * x * x)))))


def _ln_rows(x, g, b):
    mu = jnp.mean(x, axis=-1, keepdims=True)
    xc = x - mu
    var = jnp.mean(xc * xc, axis=-1, keepdims=True)
    return xc * lax.rsqrt(var + LN_EPS) * g + b


def _causal_conv(x, carry, w, b):
    kw = w.shape[0]
    rows = x.shape[0]
    xp = jnp.concatenate([carry, x], axis=0)
    y = x * w[kw - 1:kw, :] + b
    for k in range(kw - 1):
        s = SUBLANES - (kw - 1) + k
        y = y + xp[s:s + rows, :] * w[k:k + 1, :]
    return y


def _pack_halves(y):
    half = y.shape[1] // 2
    lo = pltpu.bitcast(y[:, :half].astype(BF16).astype(F32), U32)
    hi = pltpu.bitcast(y[:, half:].astype(BF16).astype(F32), U32)
    return (hi & jnp.uint32(0xFFFF0000)) | (lo >> 16)


def _unpack_halves(u):
    lo = pltpu.bitcast(u << 16, F32).astype(BF16)
    hi = pltpu.bitcast(u & jnp.uint32(0xFFFF0000), F32).astype(BF16)
    return lo, hi


def _window_specs(off, width, rows, blk):
    first, shift = divmod(off, blk)
    nblk = pl.cdiv(shift + width, blk)
    specs = [pl.BlockSpec((rows, blk), lambda i, q=q: (i, first + q)) for q in range(nblk)]
    return specs, shift


def _window_load(refs, shift, width, rows=slice(None)):
    blk = refs[0].shape[1]
    first, lane_shift = divmod(shift, LANES)
    n_out = width // LANES

    def lane_block(k):
        q, o = divmod(k * LANES, blk)
        return refs[q][rows, o:o + LANES].astype(F32)

    if lane_shift == 0:
        return jnp.concatenate([lane_block(first + k) for k in range(n_out)], axis=1)
    rolled = [pltpu.roll(lane_block(first + k), LANES - lane_shift, 1) for k in range(n_out + 1)]
    lane = lax.broadcasted_iota(I32, rolled[0].shape, 1)
    outs = [jnp.where(lane < LANES - lane_shift, rolled[k], rolled[k + 1]) for k in range(n_out)]
    return jnp.concatenate(outs, axis=1)


def _ln0_kernel(x_ref, g_ref, b_ref, o_ref, ob_ref):
    y = _ln_rows(x_ref[...], g_ref[...], b_ref[...])
    o_ref[...] = y
    ob_ref[...] = y.astype(BF16)


def _ln0(x, g, b):
    s, d = x.shape
    tm = min(256, s)
    return pl.pallas_call(
        _ln0_kernel,
        out_shape=(jax.ShapeDtypeStruct((s, d), F32), jax.ShapeDtypeStruct((s, d), BF16)),
        grid=(s // tm,),
        in_specs=[pl.BlockSpec((tm, d), lambda i: (i, 0)),
                  pl.BlockSpec((1, d), lambda i: (0, 0)),
                  pl.BlockSpec((1, d), lambda i: (0, 0))],
        out_specs=(pl.BlockSpec((tm, d), lambda i: (i, 0)),
                   pl.BlockSpec((tm, d), lambda i: (i, 0))),
        compiler_params=_cparams("parallel"),
        name="ln0",
    )(x, g.reshape(1, d), b.reshape(1, d))


def _cast_weight_tile(w_ref, wb_ref, n_valid):
    tn = w_ref.shape[1]
    col = pl.program_id(0) * tn + lax.broadcasted_iota(I32, w_ref.shape, 1)
    wb_ref[...] = jnp.where(col < n_valid, w_ref[...], 0.0).astype(BF16)


def _mm_kernel(a_ref, w_ref, o_ref, wb_ref, *, n_valid):
    @pl.when(pl.program_id(1) == 0)
    def _():
        _cast_weight_tile(w_ref, wb_ref, n_valid)

    o_ref[...] = _dot(a_ref[...], wb_ref[...]).astype(o_ref.dtype)


def _mm_res_kernel(a_ref, w_ref, r_ref, o_ref, wb_ref, *, n_valid, alpha):
    @pl.when(pl.program_id(1) == 0)
    def _():
        _cast_weight_tile(w_ref, wb_ref, n_valid)

    o_ref[...] = (alpha * r_ref[...] + _dot(a_ref[...], wb_ref[...])).astype(o_ref.dtype)


def _matmul(a, w, out_dtype, *, tm=1024, tn=512, res=None, alpha=1.0, name="mm"):
    m, k = a.shape
    n = w.shape[1]
    tm = min(tm, m)
    tn = min(tn, n)
    n_tiles = pl.cdiv(n, tn)
    in_specs = [pl.BlockSpec((tm, k), lambda j, i: (i, 0)),
                pl.BlockSpec((k, tn), lambda j, i: (0, j))]
    args = [a, w]
    if res is None:
        kern = functools.partial(_mm_kernel, n_valid=n)
    else:
        kern = functools.partial(_mm_res_kernel, n_valid=n, alpha=alpha)
        in_specs.append(pl.BlockSpec((tm, tn), lambda j, i: (i, j)))
        args.append(res)
    return pl.pallas_call(
        kern,
        out_shape=jax.ShapeDtypeStruct((m, n_tiles * tn), out_dtype),
        grid=(n_tiles, m // tm),
        in_specs=in_specs,
        out_specs=pl.BlockSpec((tm, tn), lambda j, i: (i, j)),
        scratch_shapes=[pltpu.VMEM((k, tn), BF16)],
        compiler_params=_cparams("parallel", "arbitrary"),
        name=name,
    )(*args)


def _ssd_kernel(z_ref, xs_ref, bc_ref, dt_ref, cwx_ref, cbx_ref, cwb_ref, cbb_ref, dtb_ref, alog_ref,
                dskip_ref, ng_ref, e_ref, o_ref, cx_ref, cb_ref, st_ref, *, heads, head_dim):
    i = pl.program_id(0)

    @pl.when(i == 0)
    def _():
        cx_ref[...] = jnp.zeros_like(cx_ref)
        cb_ref[...] = jnp.zeros_like(cb_ref)
        st_ref[...] = jnp.zeros_like(st_ref)

    n = SSD_STATE
    gw = (heads // SSD_GROUPS) * head_dim
    hpg = heads // SSD_GROUPS
    pair = LANES // head_dim

    xs_raw = xs_ref[...].astype(F32)
    bc_raw = bc_ref[...].astype(F32)
    xs = _silu(_causal_conv(xs_raw, cx_ref[...], cwx_ref[...], cbx_ref[...]))
    bc = _silu(_causal_conv(bc_raw, cb_ref[...], cwb_ref[...], cbb_ref[...]))
    cx_ref[...] = xs_raw[CHUNK - SUBLANES:, :]
    cb_ref[...] = bc_raw[CHUNK - SUBLANES:, :]

    row = lax.broadcasted_iota(I32, (CHUNK, CHUNK), 0)
    col = lax.broadcasted_iota(I32, (CHUNK, CHUNK), 1)
    dt = _softplus(jnp.where(col < heads, dt_ref[...], 0.0) + dtb_ref[...])
    da = dt * (-jnp.exp(alog_ref[...]))
    causal = row >= col
    ltri = jnp.where(causal, 1.0, 0.0).astype(BF16)
    cum = _exact_left(ltri, da)
    cum_t = cum.T
    expand = e_ref[...]
    cum_e = _exact_right(cum, expand)
    dt_e = _exact_right(dt, expand)
    last_e = cum_e[CHUNK - 1:CHUNK, :]
    decay_out = jnp.exp(cum_e)
    decay_st = jnp.exp(last_e - cum_e)
    chunk_decay = jnp.exp(last_e)
    xdt = xs * dt_e
    xdt_b = xdt.astype(BF16)
    xst_b = (xdt * decay_st).astype(BF16)
    lane = lax.broadcasted_iota(I32, (CHUNK, LANES), 1)

    for g in range(SSD_GROUPS):
        bg = bc[:, g * n:(g + 1) * n].astype(BF16)
        cg = bc[:, (SSD_GROUPS + g) * n:(SSD_GROUPS + g + 1) * n].astype(BF16)
        cb = lax.dot_general(cg, bg, (((1,), (1,)), ((), ())), preferred_element_type=F32)
        slabs = []
        for j in range(hpg // pair):
            h0 = g * hpg + j * pair
            sc = []
            for h in range(h0, h0 + pair):
                seg = cum[:, h:h + 1] - cum_t[h:h + 1, :]
                dec = jnp.exp(jnp.where(causal, seg, -jnp.inf))
                sc.append((cb * dec).astype(BF16))
            xp = xdt_b[:, h0 * head_dim:h0 * head_dim + LANES]
            blocks = [jnp.where((lane >= q * head_dim) & (lane < (q + 1) * head_dim), xp, jnp.zeros_like(xp))
                      for q in range(pair)]
            slabs.append(_dot(jnp.concatenate(sc, axis=1), jnp.concatenate(blocks, axis=0)))
        y_diag = jnp.concatenate(slabs, axis=1)
        cs = slice(g * gw, (g + 1) * gw)
        st = st_ref[g]
        y_off = _dot(cg, st.astype(BF16)) * decay_out[:, cs]
        st_new = lax.dot_general(bg, xst_b[:, cs], (((0,), (0,)), ((), ())), preferred_element_type=F32)
        st_ref[g] = st * chunk_decay[:, cs] + st_new
        y = y_diag + y_off + xs[:, cs] * dskip_ref[:, cs]
        yg = y * _silu(z_ref[:, cs].astype(F32))
        ms = jnp.mean(yg * yg, axis=-1, keepdims=True)
        o_ref[:, cs] = (yg * lax.rsqrt(ms + RMS_EPS) * ng_ref[:, cs]).astype(o_ref.dtype)


def _ssd(proj, lay, conv_w, conv_b, dt_bias, a_log, d_skip, norm_g):
    s = proj.shape[0]
    heads = dt_bias.shape[0]
    inner = norm_g.shape[0]
    head_dim = inner // heads
    bcw = 2 * SSD_GROUPS * SSD_STATE
    gw = inner // SSD_GROUPS
    pad = LANES - heads
    assert lay["z"] % inner == 0 and lay["xs"] % inner == 0 and lay["bc"] % bcw == 0 and lay["dt"] % LANES == 0
    expand = (jnp.arange(LANES)[:, None] == (jnp.arange(inner) // head_dim)[None, :]).astype(BF16)
    kern = functools.partial(_ssd_kernel, heads=heads, head_dim=head_dim)
    const = lambda shape: pl.BlockSpec(shape, lambda i: (0,) * len(shape))
    return pl.pallas_call(
        kern,
        out_shape=jax.ShapeDtypeStruct((s, inner), BF16),
        grid=(s // CHUNK,),
        in_specs=[pl.BlockSpec((CHUNK, inner), lambda i: (i, lay["z"] // inner)),
                  pl.BlockSpec((CHUNK, inner), lambda i: (i, lay["xs"] // inner)),
                  pl.BlockSpec((CHUNK, bcw), lambda i: (i, lay["bc"] // bcw)),
                  pl.BlockSpec((CHUNK, LANES), lambda i: (i, lay["dt"] // LANES)),
                  const((conv_w.shape[0], inner)), const((1, inner)),
                  const((conv_w.shape[0], bcw)), const((1, bcw)),
                  const((1, LANES)), const((1, LANES)),
                  const((1, inner)), const((1, inner)), const((LANES, inner))],
        out_specs=pl.BlockSpec((CHUNK, inner), lambda i: (i, 0)),
        scratch_shapes=[pltpu.VMEM((SUBLANES, inner), F32), pltpu.VMEM((SUBLANES, bcw), F32),
                        pltpu.VMEM((SSD_GROUPS, SSD_STATE, gw), F32)],
        compiler_params=_cparams("arbitrary"),
        name="ssd",
    )(proj, proj, proj, proj,
      conv_w[:, :inner], conv_b[:inner].reshape(1, inner),
      conv_w[:, inner:], conv_b[inner:].reshape(1, bcw),
      jnp.pad(dt_bias, (0, pad)).reshape(1, LANES), jnp.pad(a_log, (0, pad)).reshape(1, LANES),
      jnp.repeat(d_skip, head_dim).reshape(1, inner), norm_g.reshape(1, inner), expand)


def _gmlp_kernel(*refs, chunks, nu, nv, su, sv):
    u_refs, v_refs = refs[:nu], refs[nu:nu + nv]
    lg_ref, lb_ref, ws_ref, be_ref, o_ref = refs[nu + nv:]
    width = o_ref.shape[1]
    gw = width // GMLP_GROUPS
    row = lax.broadcasted_iota(I32, (CHUNK, CHUNK), 0)
    col = lax.broadcasted_iota(I32, (CHUNK, CHUNK), 1)
    tril = row >= col
    ws = [jnp.where(tril, ws_ref[g], 0.0).astype(BF16) for g in range(GMLP_GROUPS)]
    for c in range(chunks):
        rs = slice(c * CHUNK, (c + 1) * CHUNK)
        v = _ln_rows(_gelu(_window_load(v_refs, sv, width, rs)), lg_ref[...], lb_ref[...]).astype(BF16)
        u = _gelu(_window_load(u_refs, su, width, rs))
        for g in range(GMLP_GROUPS):
            cs = slice(g * gw, (g + 1) * gw)
            mixed = _dot(ws[g], v[:, cs]) + be_ref[:, cs]
            o_ref[rs, cs] = (u[:, cs] * mixed).astype(o_ref.dtype)


def _gmlp(proj, lay, ln_g, ln_b, w_s, b_s, *, rows=256):
    s = proj.shape[0]
    width = ln_g.shape[0]
    gw = width // GMLP_GROUPS
    rows = min(rows, s)
    bias_e = jnp.repeat(b_s.T, gw, axis=1)
    u_specs, su = _window_specs(lay["u"], width, rows, WINDOW_BLOCK)
    v_specs, sv = _window_specs(lay["v"], width, rows, WINDOW_BLOCK)
    kern = functools.partial(_gmlp_kernel, chunks=rows // CHUNK, nu=len(u_specs), nv=len(v_specs), su=su, sv=sv)
    const = lambda shape: pl.BlockSpec(shape, lambda i: (0,) * len(shape))
    return pl.pallas_call(
        kern,
        out_shape=jax.ShapeDtypeStruct((s, width), BF16),
        grid=(s // rows,),
        in_specs=u_specs + v_specs + [const((1, width)), const((1, width)),
                                      const((GMLP_GROUPS, CHUNK, CHUNK)), const((CHUNK, width))],
        out_specs=pl.BlockSpec((rows, width), lambda i: (i, 0)),
        compiler_params=_cparams("parallel"),
        name="gmlp",
    )(*([proj] * (len(u_specs) + len(v_specs))), ln_g.reshape(1, width), ln_b.reshape(1, width), w_s, bias_e)


def _lru_kernel(*refs, ng, nx, sg, sx):
    gate_refs, x_refs = refs[:ng], refs[ng:ng + nx]
    cw_ref, cb_ref, wa_ref, ba_ref, wx_ref, bx_ref, lam_ref, o_ref, carry_ref, h_ref = refs[ng + nx:]
    i = pl.program_id(0)

    @pl.when(i == 0)
    def _():
        carry_ref[...] = jnp.zeros_like(carry_ref)
        h_ref[...] = jnp.zeros_like(h_ref)

    rows, width = o_ref.shape
    nblk, blk = wa_ref.shape[0], wa_ref.shape[1]
    x_raw = _window_load(x_refs, sx, width)
    xr = _causal_conv(x_raw, carry_ref[...], cw_ref[...], cb_ref[...])
    carry_ref[...] = x_raw[rows - SUBLANES:, :]
    xb = xr.astype(BF16)
    ra = jnp.concatenate([_dot(xb[:, q * blk:(q + 1) * blk], wa_ref[q]) for q in range(nblk)], axis=1)
    ia = jnp.concatenate([_dot(xb[:, q * blk:(q + 1) * blk], wx_ref[q]) for q in range(nblk)], axis=1)
    r = _sigmoid(ra + ba_ref[...])
    ig = _sigmoid(ia + bx_ref[...])
    log_a = (-LRU_C) * r * _softplus(-lam_ref[...])
    a = jnp.exp(log_a)
    b = jnp.sqrt((1.0 - a) * (1.0 + a)) * (ig * xr)
    ridx = lax.broadcasted_iota(I32, (rows, width), 0)
    d = 1
    while d < rows:
        keep = ridx >= d
        a_s = pltpu.roll(a, d, 0)
        b_s = pltpu.roll(b, d, 0)
        b = jnp.where(keep, a * b_s + b, b)
        a = jnp.where(keep, a * a_s, a)
        d *= 2
    h = a * h_ref[...] + b
    h_ref[...] = h[rows - 1:rows, :]
    o_ref[...] = (h * _gelu(_window_load(gate_refs, sg, width))).astype(o_ref.dtype)


def _lru(proj, lay, conv_w, conv_b, w_a, b_a, w_x, b_x, lam, *, rows=128):
    s = proj.shape[0]
    width = lam.shape[0]
    nblk, blk = w_a.shape[0], w_a.shape[1]
    rows = min(rows, s)
    g_specs, sg = _window_specs(lay["gate_c"], width, rows, WINDOW_BLOCK)
    x_specs, sx = _window_specs(lay["x_c"], width, rows, WINDOW_BLOCK)
    kern = functools.partial(_lru_kernel, ng=len(g_specs), nx=len(x_specs), sg=sg, sx=sx)
    const = lambda shape: pl.BlockSpec(shape, lambda i: (0,) * len(shape))
    return pl.pallas_call(
        kern,
        out_shape=jax.ShapeDtypeStruct((s, width), BF16),
        grid=(s // rows,),
        in_specs=g_specs + x_specs + [const((conv_w.shape[0], width)), const((1, width)),
                                      const((nblk, blk, blk)), const((1, width)),
                                      const((nblk, blk, blk)), const((1, width)), const((1, width))],
        out_specs=pl.BlockSpec((rows, width), lambda i: (i, 0)),
        scratch_shapes=[pltpu.VMEM((SUBLANES, width), F32), pltpu.VMEM((1, width), F32)],
        compiler_params=_cparams("arbitrary"),
        name="rglru",
    )(*([proj] * (len(g_specs) + len(x_specs))), conv_w, conv_b.reshape(1, width), w_a.astype(BF16),
      b_a.reshape(1, width), w_x.astype(BF16), b_x.reshape(1, width), lam.reshape(1, width))


def _merge_kernel(*refs, shift, ng):
    y_refs, p_refs = refs[:3], refs[3:6]
    gate_refs = [refs[6 + br * ng:6 + (br + 1) * ng] for br in range(3)]
    b_refs = refs[6 + 3 * ng:9 + 3 * ng]
    o_ref = refs[9 + 3 * ng]
    tn = o_ref.shape[1]
    acc = None
    for br in range(3):
        gate = _sigmoid(_window_load(gate_refs[br], shift, tn) + b_refs[br][...])
        term = gate * _dot(y_refs[br][...], p_refs[br][...])
        acc = term if acc is None else acc + term
    o_ref[...] = acc.astype(o_ref.dtype)


def _merge(y_a, y_b, y_c, p_a, p_b, p_c, proj, lay, b_merge, *, tm=512, tn=512):
    s, w = y_a.shape
    d = p_a.shape[1]
    tm = min(tm, s)
    tn = min(tn, d)
    g0, shift = divmod(lay["gates"], tn)
    ng = 2 if shift else 1
    nd = d // tn
    bm = b_merge.reshape(1, 3 * d)
    y_spec = pl.BlockSpec((tm, w), lambda j, i: (i, 0))
    p_spec = pl.BlockSpec((w, tn), lambda j, i: (0, j))
    g_specs = [pl.BlockSpec((tm, tn), lambda j, i, br=br, q=q: (i, g0 + br * nd + j + q))
               for br in range(3) for q in range(ng)]
    b_specs = [pl.BlockSpec((1, tn), lambda j, i, br=br: (0, br * nd + j)) for br in range(3)]
    kern = functools.partial(_merge_kernel, shift=shift, ng=ng)
    return pl.pallas_call(
        kern,
        out_shape=jax.ShapeDtypeStruct((s, d), BF16),
        grid=(d // tn, s // tm),
        in_specs=[y_spec, y_spec, y_spec, p_spec, p_spec, p_spec] + g_specs + b_specs,
        out_specs=pl.BlockSpec((tm, tn), lambda j, i: (i, j)),
        compiler_params=_cparams("parallel", "parallel"),
        name="merge",
    )(y_a, y_b, y_c, p_a, p_b, p_c, *([proj] * (3 * ng)), bm, bm, bm)


def _ln_router_kernel(h_ref, g_ref, b_ref, wr_ref, br_ref, x_ref, xp_ref, te_ref, tw_ref, rk_ref, cnt_ref,
                      carry_ref):
    i = pl.program_id(0)

    @pl.when(i == 0)
    def _():
        carry_ref[...] = jnp.zeros_like(carry_ref)

    rows = h_ref.shape[0]
    y = _ln_rows(h_ref[...], g_ref[...], b_ref[...])
    x_ref[...] = y
    xp_ref[...] = _pack_halves(y)
    yh = y.astype(BF16)
    yl = (y - yh.astype(F32)).astype(BF16)
    wr = wr_ref[...]
    wh = wr.astype(BF16)
    wl = (wr - wh.astype(F32)).astype(BF16)
    logits = _dot(yh, wh) + (_dot(yh, wl) + _dot(yl, wh)) + br_ref[...]
    lane = lax.broadcasted_iota(I32, (rows, LANES), 1)
    lane_f = lane.astype(F32)
    cur = logits
    vals, sels, idxs = [], [], []
    for _ in range(TOP_K):
        m = jnp.max(cur, axis=-1, keepdims=True)
        idx = jnp.min(jnp.where(cur == m, lane_f, float(LANES)), axis=-1, keepdims=True)
        sel = lane_f == idx
        vals.append(m)
        sels.append(sel)
        idxs.append(idx)
        cur = jnp.where(sel, -jnp.inf, cur)
    ex = [jnp.exp(v - vals[0]) for v in vals]
    den = ex[0]
    for e in ex[1:]:
        den = den + e
    multi = jnp.zeros((rows, LANES), F32)
    for sel in sels:
        multi = multi + jnp.where(sel, 1.0, 0.0)
    r2 = lax.broadcasted_iota(I32, (rows, rows), 0)
    c2 = lax.broadcasted_iota(I32, (rows, rows), 1)
    lower = jnp.where(r2 > c2, 1.0, 0.0).astype(BF16)
    before = _dot(lower, multi.astype(BF16)) + carry_ref[...]
    te = jnp.zeros((rows, LANES), F32)
    tw = jnp.zeros((rows, LANES), F32)
    rk = jnp.zeros((rows, LANES), F32)
    for k in range(TOP_K):
        rank_k = jnp.sum(jnp.where(sels[k], before, 0.0), axis=-1, keepdims=True)
        te = jnp.where(lane == k, idxs[k], te)
        tw = jnp.where(lane == k, ex[k] / den, tw)
        rk = jnp.where(lane == k, rank_k, rk)
    te_ref[...] = te.astype(I32)
    tw_ref[...] = tw
    rk_ref[...] = rk.astype(I32)
    carry_ref[...] = carry_ref[...] + jnp.sum(multi, axis=0, keepdims=True)
    cnt_ref[...] = carry_ref[...].astype(I32)


def _ln_router(h, g, b, w_router, b_router, *, tm=256):
    s, d = h.shape
    ne = w_router.shape[1]
    tm = min(tm, s)
    wr = jnp.pad(w_router, ((0, 0), (0, LANES - ne)))
    br = jnp.pad(b_router, (0, LANES - ne), constant_values=NEG_BIG).reshape(1, LANES)
    row = lambda width: pl.BlockSpec((tm, width), lambda i: (i, 0))
    const = lambda shape: pl.BlockSpec(shape, lambda i: (0,) * len(shape))
    return pl.pallas_call(
        _ln_router_kernel,
        out_shape=(jax.ShapeDtypeStruct((s, d), F32), jax.ShapeDtypeStruct((s, d // 2), U32),
                   jax.ShapeDtypeStruct((s, LANES), I32), jax.ShapeDtypeStruct((s, LANES), F32),
                   jax.ShapeDtypeStruct((s, LANES), I32), jax.ShapeDtypeStruct((1, LANES), I32)),
        grid=(s // tm,),
        in_specs=[row(d), const((1, d)), const((1, d)), const((d, LANES)), const((1, LANES))],
        out_specs=(row(d), row(d // 2), row(LANES), row(LANES), row(LANES), const((1, LANES))),
        scratch_shapes=[pltpu.VMEM((1, LANES), F32)],
        compiler_params=_cparams("arbitrary"),
        name="ln_router",
    )(h, g.reshape(1, d), b.reshape(1, d), wr, br)


def _row_copy(src_hbm, dst_hbm, sem, src_row, dst_row):
    return pltpu.make_async_copy(src_hbm.at[pl.ds(src_row, 1)], dst_hbm.at[pl.ds(dst_row, 1)], sem)


def _dispatch_kernel(dest_ref, x_ref, init_hbm, o_hbm, sem, *, tokens):
    del init_hbm

    def issue(t, c):
        for k in range(TOP_K):
            _row_copy(x_ref, o_hbm, sem, t, dest_ref[0, 0, t * TOP_K + k]).start()
        return c

    lax.fori_loop(0, tokens, issue, 0)

    def drain(t, c):
        for k in range(TOP_K):
            _row_copy(x_ref, o_hbm, sem, 0, 0).wait()
        return c

    lax.fori_loop(0, tokens, drain, 0)


def _dispatch(xp, dest, n_rows, *, tokens=256):
    s, w = xp.shape
    tokens = min(tokens, s)
    steps = s // tokens
    kern = functools.partial(_dispatch_kernel, tokens=tokens)
    return pl.pallas_call(
        kern,
        out_shape=jax.ShapeDtypeStruct((n_rows, w), xp.dtype),
        grid=(steps,),
        in_specs=[pl.BlockSpec((1, 1, tokens * TOP_K), lambda i: (i, 0, 0), memory_space=pltpu.SMEM),
                  pl.BlockSpec((tokens, w), lambda i: (i, 0)),
                  pl.BlockSpec(memory_space=pl.ANY)],
        out_specs=pl.BlockSpec(memory_space=pl.ANY),
        scratch_shapes=[pltpu.SemaphoreType.DMA],
        input_output_aliases={2: 0},
        compiler_params=_cparams("arbitrary"),
        name="dispatch",
    )(dest.reshape(steps, 1, tokens * TOP_K), xp, jnp.zeros((n_rows, w), xp.dtype))


def _w1_prep_kernel(w_ref, o_ref):
    c = lax.broadcasted_iota(I32, (2 * LANES, 2 * LANES), 0)
    j = lax.broadcasted_iota(I32, (2 * LANES, 2 * LANES), 1)
    src = jnp.where(j < LANES, 2 * j, 2 * (j - LANES) + 1)
    perm = jnp.where(c == src, 1.0, 0.0).astype(BF16)
    o_ref[0] = _dot(w_ref[0].astype(BF16), perm).astype(BF16)


def _w1_prep(w1):
    ne, d, f2 = w1.shape
    gw = 2 * LANES
    return pl.pallas_call(
        _w1_prep_kernel,
        out_shape=jax.ShapeDtypeStruct((ne, d, f2), BF16),
        grid=(ne, f2 // gw),
        in_specs=[pl.BlockSpec((1, d, gw), lambda e, j: (e, 0, j))],
        out_specs=pl.BlockSpec((1, d, gw), lambda e, j: (e, 0, j)),
        compiler_params=_cparams("parallel", "parallel"),
        name="w1_prep",
    )(w1)


def _regroup_bias(b1):
    ne, f2 = b1.shape
    return b1.reshape(ne, f2 // (2 * LANES), LANES, 2).transpose(0, 1, 3, 2).reshape(ne, 1, f2)


def _expert_kernel(be_ref, nu_ref, x_ref, w1_ref, b1_ref, w2_ref, b2_ref, o_ref):
    del be_ref

    @pl.when(pl.program_id(0) < nu_ref[0])
    def _():
        lo, hi = _unpack_halves(x_ref[...])
        half = lo.shape[1]
        gu = _dot(lo, w1_ref[0, :half, :]) + _dot(hi, w1_ref[0, half:, :]) + b1_ref[0]
        acts = []
        for j in range(gu.shape[1] // (2 * LANES)):
            glu = jnp.minimum(gu[:, 2 * j * LANES:(2 * j + 1) * LANES], SWIGLU_LIMIT)
            lin = jnp.clip(gu[:, (2 * j + 1) * LANES:(2 * j + 2) * LANES], -SWIGLU_LIMIT, SWIGLU_LIMIT)
            acts.append(glu * _sigmoid(SWIGLU_ALPHA * glu) * (lin + 1.0))
        act = jnp.concatenate(acts, axis=1)
        o_ref[...] = _dot(act.astype(BF16), w2_ref[0]) + b2_ref[0]

    @pl.when(pl.program_id(0) >= nu_ref[0])
    def _():
        o_ref[...] = jnp.zeros_like(o_ref)


def _experts(xs, block_e, n_used, w1p, b1p, w2, b2):
    n_rows, half = xs.shape
    ne, f, d = w2.shape
    n_blocks = n_rows // EXPERT_ROWS

    def blk(b, be, nu):
        return jnp.minimum(b, nu[0] - 1)

    grid_spec = pltpu.PrefetchScalarGridSpec(
        num_scalar_prefetch=2,
        grid=(n_blocks,),
        in_specs=[pl.BlockSpec((EXPERT_ROWS, half), lambda b, be, nu: (blk(b, be, nu), 0)),
                  pl.BlockSpec((1, d, 2 * f), lambda b, be, nu: (be[blk(b, be, nu)], 0, 0)),
                  pl.BlockSpec((1, 1, 2 * f), lambda b, be, nu: (be[blk(b, be, nu)], 0, 0)),
                  pl.BlockSpec((1, f, d), lambda b, be, nu: (be[blk(b, be, nu)], 0, 0)),
                  pl.BlockSpec((1, 1, d), lambda b, be, nu: (be[blk(b, be, nu)], 0, 0))],
        out_specs=pl.BlockSpec((EXPERT_ROWS, d), lambda b, be, nu: (b, 0)),
    )
    return pl.pallas_call(
        _expert_kernel,
        out_shape=jax.ShapeDtypeStruct((n_rows, d), F32),
        grid_spec=grid_spec,
        compiler_params=_cparams("arbitrary"),
        name="experts",
    )(block_e, n_used, xs, w1p, b1p, w2, b2.reshape(ne, 1, d))


def _combine_kernel(dest_ref, tw_ref, x_ref, g_ref, b_ref, ys_hbm, o_ref, ob_ref, buf_ref, sem, *, alpha):
    tokens = x_ref.shape[0]

    def copy(t, k, dst_row):
        return pltpu.make_async_copy(ys_hbm.at[pl.ds(dst_row, 1)], buf_ref.at[k, pl.ds(t, 1)], sem)

    def issue(t, c):
        for k in range(TOP_K):
            copy(t, k, dest_ref[0, 0, t * TOP_K + k]).start()
        return c

    lax.fori_loop(0, tokens, issue, 0)

    def drain(t, c):
        for k in range(TOP_K):
            copy(0, k, 0).wait()
        return c

    lax.fori_loop(0, tokens, drain, 0)
    tw = tw_ref[...]
    y = alpha * x_ref[...]
    for k in range(TOP_K):
        y = y + tw[:, k:k + 1] * buf_ref[k]
    out = _ln_rows(y, g_ref[...], b_ref[...])
    o_ref[...] = out
    ob_ref[...] = out.astype(BF16)


def _combine(ys, dest, tw, x, g, b, alpha, *, tokens=128):
    s, d = x.shape
    tokens = min(tokens, s)
    steps = s // tokens
    kern = functools.partial(_combine_kernel, alpha=alpha)
    row = lambda width: pl.BlockSpec((tokens, width), lambda i: (i, 0))
    const = lambda shape: pl.BlockSpec(shape, lambda i: (0,) * len(shape))
    return pl.pallas_call(
        kern,
        out_shape=(jax.ShapeDtypeStruct((s, d), F32), jax.ShapeDtypeStruct((s, d), BF16)),
        grid=(steps,),
        in_specs=[pl.BlockSpec((1, 1, tokens * TOP_K), lambda i: (i, 0, 0), memory_space=pltpu.SMEM),
                  row(LANES), row(d), const((1, d)), const((1, d)),
                  pl.BlockSpec(memory_space=pl.ANY)],
        out_specs=(row(d), row(d)),
        scratch_shapes=[pltpu.VMEM((TOP_K, tokens, d), F32), pltpu.SemaphoreType.DMA],
        compiler_params=_cparams("arbitrary"),
        name="combine",
    )(dest.reshape(steps, 1, tokens * TOP_K), tw, x, g.reshape(1, d), b.reshape(1, d), ys)


def _in_layout(sizes):
    z, xbc, dt, u, v, gate_c, x_c, gates = sizes
    lay = {"z": 0, "xs": z, "bc": 2 * z, "dt": z + xbc}
    lay["u"] = lay["dt"] + dt
    lay["v"] = lay["u"] + u
    lay["gate_c"] = lay["v"] + v
    lay["x_c"] = lay["gate_c"] + gate_c
    lay["gates"] = lay["x_c"] + x_c
    return lay


def _moe_plan(te, rk, cnt, ne, n_blocks):
    counts = cnt[0, :ne]
    padded = (counts + EXPERT_ROWS - 1) // EXPERT_ROWS * EXPERT_ROWS
    pad_end = jnp.cumsum(padded)
    pad_start = pad_end - padded
    picked = te[:, :TOP_K, None] == jnp.arange(ne, dtype=I32)[None, None, :]
    dest = (jnp.sum(jnp.where(picked, pad_start[None, None, :], 0), axis=-1) + rk[:, :TOP_K]).astype(I32)
    dest = dest.reshape(-1)
    block_start = jnp.arange(n_blocks, dtype=I32) * EXPERT_ROWS
    block_e = jnp.sum(pad_end[None, :] <= block_start[:, None], axis=1)
    block_e = jnp.minimum(block_e, ne - 1).astype(I32)
    n_used = (pad_end[-1] // EXPERT_ROWS).astype(I32).reshape(1)
    return dest, block_e, n_used


def kernel(x, ln_emb_g, ln_emb_b, w_in, b_merge, conv_a_w, conv_a_b, dt_bias, a_log, d_skip, norm_a_g,
           ln_v_g, ln_v_b, w_spatial, b_spatial, conv_c_w, conv_c_b, w_rg_a, b_rg_a, w_rg_x, b_rg_x, lam,
           p_a, p_b, p_c, w_o, ln_mix_g, ln_mix_b, w_router, b_router, w1, b1, w2, b2, ln_ffn_g, ln_ffn_b):
    bsz, seq, d = x.shape
    depth = w_in.shape[0]
    alpha = (2 * depth) ** 0.25
    heads = dt_bias.shape[1]
    inner = norm_a_g.shape[1]
    sizes = (inner, conv_a_w.shape[2], heads, ln_v_g.shape[1], ln_v_g.shape[1], lam.shape[1], lam.shape[1],
             3 * d)
    lay = _in_layout(sizes)
    ne = w_router.shape[2]
    s = bsz * seq
    n_rows = s * TOP_K + ne * EXPERT_ROWS
    n_blocks = n_rows // EXPERT_ROWS

    xf, xb = _ln0(x.reshape(s, d), ln_emb_g, ln_emb_b)
    for l in range(depth):
        proj = _matmul(xb, w_in[l], F32, name="in_proj")
        y_a = _ssd(proj, lay, conv_a_w[l], conv_a_b[l], dt_bias[l], a_log[l], d_skip[l], norm_a_g[l])
        y_b = _gmlp(proj, lay, ln_v_g[l], ln_v_b[l], w_spatial[l], b_spatial[l])
        y_c = _lru(proj, lay, conv_c_w[l], conv_c_b[l], w_rg_a[l], b_rg_a[l], w_rg_x[l], b_rg_x[l], lam[l])
        merged = _merge(y_a, y_b, y_c, p_a[l].astype(BF16), p_b[l].astype(BF16), p_c[l].astype(BF16),
                        proj, lay, b_merge[l])
        h = _matmul(merged, w_o[l], F32, res=xf, alpha=alpha, name="w_o")
        xm, xp, te, tw, rk, cnt = _ln_router(h, ln_mix_g[l], ln_mix_b[l], w_router[l], b_router[l])
        dest, block_e, n_used = _moe_plan(te, rk, cnt, ne, n_blocks)
        xs = _dispatch(xp, dest, n_rows)
        ys = _experts(xs, block_e, n_used, _w1_prep(w1[l]), _regroup_bias(b1[l]), w2[l].astype(BF16), b2[l])
        xf, xb = _combine(ys, dest, tw, xm, ln_ffn_g[l], ln_ffn_b[l], alpha)
    return xf.reshape(bsz, seq, d)
```

```python
import functools
import math

import jax
import jax.numpy as jnp
from jax import lax
from jax.experimental import pallas as pl
from jax.experimental.pallas import tpu as pltpu

F32 = jnp.float32
BF16 = jnp.bfloat16
U32 = jnp.uint32
I32 = jnp.int32

SSD_GROUPS = 4
SSD_STATE = 128
CHUNK = 128
GMLP_GROUPS = 8
LRU_C = 8.0
TOP_K = 4
SWIGLU_LIMIT = 7.0
SWIGLU_ALPHA = 1.702
LN_EPS = 1e-5
RMS_EPS = 1e-5

LANES = 128
SUBLANES = 8
VMEM_LIMIT = 56 * 1024 * 1024
EXPERT_ROWS = 256
WINDOW_BLOCK = 1024
NEG_BIG = -1e30


def _cparams(*sem):
    return pltpu.CompilerParams(dimension_semantics=sem, vmem_limit_bytes=VMEM_LIMIT)


def _dot(a, b):
    return jnp.dot(a, b, preferred_element_type=F32)


def _split3(x):
    hi = x.astype(BF16)
    r = x - hi.astype(F32)
    mid = r.astype(BF16)
    lo = (r - mid.astype(F32)).astype(BF16)
    return hi, mid, lo


def _exact_right(x, m):
    hi, mid, lo = _split3(x)
    return _dot(hi, m) + (_dot(mid, m) + _dot(lo, m))


def _exact_left(m, x):
    hi, mid, lo = _split3(x)
    return _dot(m, hi) + (_dot(m, mid) + _dot(m, lo))


def _sigmoid(x):
    return 1.0 / (1.0 + jnp.exp(-x))


def _silu(x):
    return x * _sigmoid(x)


def _softplus(x):
    return jnp.maximum(x, 0.0) + jnp.log1p(jnp.exp(-jnp.abs(x)))


def _gelu(x):
    c = math.sqrt(2.0 / math.pi)
    return x * (0.5 * (1.0 + jnp.tanh(c * (x + 0.044715 * (x * x * x)))))


def _ln_rows(x, g, b):
    mu = jnp.mean(x, axis=-1, keepdims=True)
    xc = x - mu
    var = jnp.mean(xc * xc, axis=-1, keepdims=True)
    return xc * lax.rsqrt(var + LN_EPS) * g + b


def _causal_conv(x, carry, w, b):
    kw = w.shape[0]
    rows = x.shape[0]
    xp = jnp.concatenate([carry, x], axis=0)
    y = x * w[kw - 1:kw, :] + b
    for k in range(kw - 1):
        s = SUBLANES - (kw - 1) + k
        y = y + xp[s:s + rows, :] * w[k:k + 1, :]
    return y


def _pack_halves(y):
    half = y.shape[1] // 2
    lo = pltpu.bitcast(y[:, :half].astype(BF16).astype(F32), U32)
    hi = pltpu.bitcast(y[:, half:].astype(BF16).astype(F32), U32)
    return (hi & jnp.uint32(0xFFFF0000)) | (lo >> 16)


def _unpack_halves(u):
    lo = pltpu.bitcast(u << 16, F32).astype(BF16)
    hi = pltpu.bitcast(u & jnp.uint32(0xFFFF0000), F32).astype(BF16)
    return lo, hi


def _window_specs(off, width, rows, blk):
    first, shift = divmod(off, blk)
    nblk = pl.cdiv(shift + width, blk)
    specs = [pl.BlockSpec((rows, blk), lambda i, q=q: (i, first + q)) for q in range(nblk)]
    return specs, shift


def _window_load(refs, shift, width, rows=slice(None)):
    blk = refs[0].shape[1]
    first, lane_shift = divmod(shift, LANES)
    n_out = width // LANES

    def lane_block(k):
        q, o = divmod(k * LANES, blk)
        return refs[q][rows, o:o + LANES].astype(F32)

    if lane_shift == 0:
        return jnp.concatenate([lane_block(first + k) for k in range(n_out)], axis=1)
    rolled = [pltpu.roll(lane_block(first + k), LANES - lane_shift, 1) for k in range(n_out + 1)]
    lane = lax.broadcasted_iota(I32, rolled[0].shape, 1)
    outs = [jnp.where(lane < LANES - lane_shift, rolled[k], rolled[k + 1]) for k in range(n_out)]
    return jnp.concatenate(outs, axis=1)


def _ln0_kernel(x_ref, g_ref, b_ref, o_ref, ob_ref):
    y = _ln_rows(x_ref[...], g_ref[...], b_ref[...])
    o_ref[...] = y
    ob_ref[...] = y.astype(BF16)


def _ln0(x, g, b):
    s, d = x.shape
    tm = min(256, s)
    return pl.pallas_call(
        _ln0_kernel,
        out_shape=(jax.ShapeDtypeStruct((s, d), F32), jax.ShapeDtypeStruct((s, d), BF16)),
        grid=(s // tm,),
        in_specs=[pl.BlockSpec((tm, d), lambda i: (i, 0)),
                  pl.BlockSpec((1, d), lambda i: (0, 0)),
                  pl.BlockSpec((1, d), lambda i: (0, 0))],
        out_specs=(pl.BlockSpec((tm, d), lambda i: (i, 0)),
                   pl.BlockSpec((tm, d), lambda i: (i, 0))),
        compiler_params=_cparams("parallel"),
        name="ln0",
    )(x, g.reshape(1, d), b.reshape(1, d))


def _cast_weight_tile(w_ref, wb_ref, n_valid):
    tn = w_ref.shape[1]
    col = pl.program_id(0) * tn + lax.broadcasted_iota(I32, w_ref.shape, 1)
    wb_ref[...] = jnp.where(col < n_valid, w_ref[...], 0.0).astype(BF16)


def _mm_kernel(a_ref, w_ref, o_ref, wb_ref, *, n_valid):
    @pl.when(pl.program_id(1) == 0)
    def _():
        _cast_weight_tile(w_ref, wb_ref, n_valid)

    o_ref[...] = _dot(a_ref[...], wb_ref[...]).astype(o_ref.dtype)


def _mm_res_kernel(a_ref, w_ref, r_ref, o_ref, wb_ref, *, n_valid, alpha):
    @pl.when(pl.program_id(1) == 0)
    def _():
        _cast_weight_tile(w_ref, wb_ref, n_valid)

    o_ref[...] = (alpha * r_ref[...] + _dot(a_ref[...], wb_ref[...])).astype(o_ref.dtype)


def _matmul(a, w, layer, out_dtype, *, tm, tn, res=None, alpha=1.0, name="mm"):
    m, k = a.shape
    n = w.shape[2]
    tm = min(tm, m)
    tn = min(tn, n)
    n_tiles = pl.cdiv(n, tn)
    in_specs = [pl.BlockSpec((tm, k), lambda j, i: (i, 0)),
                pl.BlockSpec((None, k, tn), lambda j, i: (layer, 0, j))]
    args = [a, w]
    if res is None:
        kern = functools.partial(_mm_kernel, n_valid=n)
    else:
        kern = functools.partial(_mm_res_kernel, n_valid=n, alpha=alpha)
        in_specs.append(pl.BlockSpec((tm, tn), lambda j, i: (i, j)))
        args.append(res)
    return pl.pallas_call(
        kern,
        out_shape=jax.ShapeDtypeStruct((m, n_tiles * tn), out_dtype),
        grid=(n_tiles, m // tm),
        in_specs=in_specs,
        out_specs=pl.BlockSpec((tm, tn), lambda j, i: (i, j)),
        scratch_shapes=[pltpu.VMEM((k, tn), BF16)],
        compiler_params=_cparams("parallel", "arbitrary"),
        name=name,
    )(*args)


def _ssd_kernel(z_ref, xs_ref, bc_ref, dt_ref, cwx_ref, cbx_ref, cwb_ref, cbb_ref, dtb_ref, alog_ref,
                dskip_ref, ng_ref, e_ref, o_ref, cx_ref, cb_ref, st_ref, *, heads, head_dim):
    i = pl.program_id(0)

    @pl.when(i == 0)
    def _():
        cx_ref[...] = jnp.zeros_like(cx_ref)
        cb_ref[...] = jnp.zeros_like(cb_ref)
        st_ref[...] = jnp.zeros_like(st_ref)

    n = SSD_STATE
    gw = (heads // SSD_GROUPS) * head_dim
    hpg = heads // SSD_GROUPS
    pair = LANES // head_dim

    xs_raw = xs_ref[...].astype(F32)
    bc_raw = bc_ref[...].astype(F32)
    xs = _silu(_causal_conv(xs_raw, cx_ref[...], cwx_ref[...], cbx_ref[...]))
    bc = _silu(_causal_conv(bc_raw, cb_ref[...], cwb_ref[...], cbb_ref[...]))
    cx_ref[...] = xs_raw[CHUNK - SUBLANES:, :]
    cb_ref[...] = bc_raw[CHUNK - SUBLANES:, :]

    row = lax.broadcasted_iota(I32, (CHUNK, CHUNK), 0)
    col = lax.broadcasted_iota(I32, (CHUNK, CHUNK), 1)
    dt = _softplus(jnp.where(col < heads, dt_ref[...], 0.0) + dtb_ref[...])
    da = dt * (-jnp.exp(alog_ref[...]))
    causal = row >= col
    ltri = jnp.where(causal, 1.0, 0.0).astype(BF16)
    cum = _exact_left(ltri, da)
    cum_t = cum.T
    expand = e_ref[...]
    cum_e = _exact_right(cum, expand)
    dt_e = _exact_right(dt, expand)
    last_e = cum_e[CHUNK - 1:CHUNK, :]
    decay_out = jnp.exp(cum_e)
    decay_st = jnp.exp(last_e - cum_e)
    chunk_decay = jnp.exp(last_e)
    xdt = xs * dt_e
    xdt_b = xdt.astype(BF16)
    xst_b = (xdt * decay_st).astype(BF16)
    lane = lax.broadcasted_iota(I32, (CHUNK, LANES), 1)

    for g in range(SSD_GROUPS):
        bg = bc[:, g * n:(g + 1) * n].astype(BF16)
        cg = bc[:, (SSD_GROUPS + g) * n:(SSD_GROUPS + g + 1) * n].astype(BF16)
        cb = lax.dot_general(cg, bg, (((1,), (1,)), ((), ())), preferred_element_type=F32)
        slabs = []
        for j in range(hpg // pair):
            h0 = g * hpg + j * pair
            sc = []
            for h in range(h0, h0 + pair):
                seg = cum[:, h:h + 1] - cum_t[h:h + 1, :]
                dec = jnp.exp(jnp.where(causal, seg, -jnp.inf))
                sc.append((cb * dec).astype(BF16))
            xp = xdt_b[:, h0 * head_dim:h0 * head_dim + LANES]
            blocks = [jnp.where((lane >= q * head_dim) & (lane < (q + 1) * head_dim), xp, jnp.zeros_like(xp))
                      for q in range(pair)]
            slabs.append(_dot(jnp.concatenate(sc, axis=1), jnp.concatenate(blocks, axis=0)))
        y_diag = jnp.concatenate(slabs, axis=1)
        cs = slice(g * gw, (g + 1) * gw)
        st = st_ref[g]
        y_off = _dot(cg, st.astype(BF16)) * decay_out[:, cs]
        st_new = lax.dot_general(bg, xst_b[:, cs], (((0,), (0,)), ((), ())), preferred_element_type=F32)
        st_ref[g] = st * chunk_decay[:, cs] + st_new
        y = y_diag + y_off + xs[:, cs] * dskip_ref[:, cs]
        yg = y * _silu(z_ref[:, cs].astype(F32))
        ms = jnp.mean(yg * yg, axis=-1, keepdims=True)
        o_ref[:, cs] = (yg * lax.rsqrt(ms + RMS_EPS) * ng_ref[:, cs]).astype(o_ref.dtype)


def _ssd(proj, lay, conv_w, conv_b, dt_bias, a_log, d_skip, norm_g):
    s = proj.shape[0]
    heads = dt_bias.shape[0]
    inner = norm_g.shape[0]
    head_dim = inner // heads
    bcw = 2 * SSD_GROUPS * SSD_STATE
    gw = inner // SSD_GROUPS
    pad = LANES - heads
    assert lay["z"] % inner == 0 and lay["xs"] % inner == 0 and lay["bc"] % bcw == 0 and lay["dt"] % LANES == 0
    expand = (jnp.arange(LANES)[:, None] == (jnp.arange(inner) // head_dim)[None, :]).astype(BF16)
    kern = functools.partial(_ssd_kernel, heads=heads, head_dim=head_dim)
    const = lambda shape: pl.BlockSpec(shape, lambda i: (0,) * len(shape))
    return pl.pallas_call(
        kern,
        out_shape=jax.ShapeDtypeStruct((s, inner), BF16),
        grid=(s // CHUNK,),
        in_specs=[pl.BlockSpec((CHUNK, inner), lambda i: (i, lay["z"] // inner)),
                  pl.BlockSpec((CHUNK, inner), lambda i: (i, lay["xs"] // inner)),
                  pl.BlockSpec((CHUNK, bcw), lambda i: (i, lay["bc"] // bcw)),
                  pl.BlockSpec((CHUNK, LANES), lambda i: (i, lay["dt"] // LANES)),
                  const((conv_w.shape[0], inner)), const((1, inner)),
                  const((conv_w.shape[0], bcw)), const((1, bcw)),
                  const((1, LANES)), const((1, LANES)),
                  const((1, inner)), const((1, inner)), const((LANES, inner))],
        out_specs=pl.BlockSpec((CHUNK, inner), lambda i: (i, 0)),
        scratch_shapes=[pltpu.VMEM((SUBLANES, inner), F32), pltpu.VMEM((SUBLANES, bcw), F32),
                        pltpu.VMEM((SSD_GROUPS, SSD_STATE, gw), F32)],
        compiler_params=_cparams("arbitrary"),
        name="ssd",
    )(proj, proj, proj, proj,
      conv_w[:, :inner], conv_b[:inner].reshape(1, inner),
      conv_w[:, inner:], conv_b[inner:].reshape(1, bcw),
      jnp.pad(dt_bias, (0, pad)).reshape(1, LANES), jnp.pad(a_log, (0, pad)).reshape(1, LANES),
      jnp.repeat(d_skip, head_dim).reshape(1, inner), norm_g.reshape(1, inner), expand)


def _gmlp_kernel(*refs, chunks, nu, nv, su, sv):
    u_refs, v_refs = refs[:nu], refs[nu:nu + nv]
    lg_ref, lb_ref, ws_ref, be_ref, o_ref = refs[nu + nv:]
    width = o_ref.shape[1]
    gw = width // GMLP_GROUPS
    row = lax.broadcasted_iota(I32, (CHUNK, CHUNK), 0)
    col = lax.broadcasted_iota(I32, (CHUNK, CHUNK), 1)
    tril = row >= col
    ws = [jnp.where(tril, ws_ref[g], 0.0).astype(BF16) for g in range(GMLP_GROUPS)]
    for c in range(chunks):
        rs = slice(c * CHUNK, (c + 1) * CHUNK)
        v = _ln_rows(_gelu(_window_load(v_refs, sv, width, rs)), lg_ref[...], lb_ref[...]).astype(BF16)
        u = _gelu(_window_load(u_refs, su, width, rs))
        for g in range(GMLP_GROUPS):
            cs = slice(g * gw, (g + 1) * gw)
            mixed = _dot(ws[g], v[:, cs]) + be_ref[:, cs]
            o_ref[rs, cs] = (u[:, cs] * mixed).astype(o_ref.dtype)


def _gmlp(proj, lay, ln_g, ln_b, w_s, b_s, *, rows=256):
    s = proj.shape[0]
    width = ln_g.shape[0]
    gw = width // GMLP_GROUPS
    rows = min(rows, s)
    bias_e = jnp.repeat(b_s.T, gw, axis=1)
    u_specs, su = _window_specs(lay["u"], width, rows, WINDOW_BLOCK)
    v_specs, sv = _window_specs(lay["v"], width, rows, WINDOW_BLOCK)
    kern = functools.partial(_gmlp_kernel, chunks=rows // CHUNK, nu=len(u_specs), nv=len(v_specs), su=su, sv=sv)
    const = lambda shape: pl.BlockSpec(shape, lambda i: (0,) * len(shape))
    return pl.pallas_call(
        kern,
        out_shape=jax.ShapeDtypeStruct((s, width), BF16),
        grid=(s // rows,),
        in_specs=u_specs + v_specs + [const((1, width)), const((1, width)),
                                      const((GMLP_GROUPS, CHUNK, CHUNK)), const((CHUNK, width))],
        out_specs=pl.BlockSpec((rows, width), lambda i: (i, 0)),
        compiler_params=_cparams("parallel"),
        name="gmlp",
    )(*([proj] * (len(u_specs) + len(v_specs))), ln_g.reshape(1, width), ln_b.reshape(1, width), w_s, bias_e)


def _lru_kernel(*refs, ng, nx, sg, sx):
    gate_refs, x_refs = refs[:ng], refs[ng:ng + nx]
    cw_ref, cb_ref, wa_ref, ba_ref, wx_ref, bx_ref, lam_ref, o_ref, carry_ref, h_ref = refs[ng + nx:]
    i = pl.program_id(0)

    @pl.when(i == 0)
    def _():
        carry_ref[...] = jnp.zeros_like(carry_ref)
        h_ref[...] = jnp.zeros_like(h_ref)

    rows, width = o_ref.shape
    nblk, blk = wa_ref.shape[0], wa_ref.shape[1]
    x_raw = _window_load(x_refs, sx, width)
    xr = _causal_conv(x_raw, carry_ref[...], cw_ref[...], cb_ref[...])
    carry_ref[...] = x_raw[rows - SUBLANES:, :]
    xb = xr.astype(BF16)
    ra = jnp.concatenate([_dot(xb[:, q * blk:(q + 1) * blk], wa_ref[q]) for q in range(nblk)], axis=1)
    ia = jnp.concatenate([_dot(xb[:, q * blk:(q + 1) * blk], wx_ref[q]) for q in range(nblk)], axis=1)
    r = _sigmoid(ra + ba_ref[...])
    ig = _sigmoid(ia + bx_ref[...])
    log_a = (-LRU_C) * r * _softplus(-lam_ref[...])
    a = jnp.exp(log_a)
    b = jnp.sqrt((1.0 - a) * (1.0 + a)) * (ig * xr)
    ridx = lax.broadcasted_iota(I32, (rows, width), 0)
    d = 1
    while d < rows:
        keep = ridx >= d
        a_s = pltpu.roll(a, d, 0)
        b_s = pltpu.roll(b, d, 0)
        b = jnp.where(keep, a * b_s + b, b)
        a = jnp.where(keep, a * a_s, a)
        d *= 2
    h = a * h_ref[...] + b
    h_ref[...] = h[rows - 1:rows, :]
    o_ref[...] = (h * _gelu(_window_load(gate_refs, sg, width))).astype(o_ref.dtype)


def _lru(proj, lay, conv_w, conv_b, w_a, b_a, w_x, b_x, lam, *, rows=128):
    s = proj.shape[0]
    width = lam.shape[0]
    nblk, blk = w_a.shape[0], w_a.shape[1]
    rows = min(rows, s)
    g_specs, sg = _window_specs(lay["gate_c"], width, rows, WINDOW_BLOCK)
    x_specs, sx = _window_specs(lay["x_c"], width, rows, WINDOW_BLOCK)
    kern = functools.partial(_lru_kernel, ng=len(g_specs), nx=len(x_specs), sg=sg, sx=sx)
    const = lambda shape: pl.BlockSpec(shape, lambda i: (0,) * len(shape))
    return pl.pallas_call(
        kern,
        out_shape=jax.ShapeDtypeStruct((s, width), BF16),
        grid=(s // rows,),
        in_specs=g_specs + x_specs + [const((conv_w.shape[0], width)), const((1, width)),
                                      const((nblk, blk, blk)), const((1, width)),
                                      const((nblk, blk, blk)), const((1, width)), const((1, width))],
        out_specs=pl.BlockSpec((rows, width), lambda i: (i, 0)),
        scratch_shapes=[pltpu.VMEM((SUBLANES, width), F32), pltpu.VMEM((1, width), F32)],
        compiler_params=_cparams("arbitrary"),
        name="rglru",
    )(*([proj] * (len(g_specs) + len(x_specs))), conv_w, conv_b.reshape(1, width), w_a.astype(BF16),
      b_a.reshape(1, width), w_x.astype(BF16), b_x.reshape(1, width), lam.reshape(1, width))


def _merge_kernel(*refs, shift, ng):
    y_refs, p_refs = refs[:3], refs[3:6]
    gate_refs = [refs[6 + br * ng:6 + (br + 1) * ng] for br in range(3)]
    b_refs = refs[6 + 3 * ng:9 + 3 * ng]
    o_ref = refs[9 + 3 * ng]
    tn = o_ref.shape[1]
    acc = None
    for br in range(3):
        gate = _sigmoid(_window_load(gate_refs[br], shift, tn) + b_refs[br][...])
        term = gate * _dot(y_refs[br][...], p_refs[br][...])
        acc = term if acc is None else acc + term
    o_ref[...] = acc.astype(o_ref.dtype)


def _merge(y_a, y_b, y_c, p_a, p_b, p_c, proj, lay, b_merge, *, tm=256, tn=1024):
    s, w = y_a.shape
    d = p_a.shape[1]
    tm = min(tm, s)
    tn = min(tn, d)
    g0, shift = divmod(lay["gates"], tn)
    assert shift < LANES
    ng = 2 if shift else 1
    nd = d // tn
    bm = b_merge.reshape(1, 3 * d)
    y_spec = pl.BlockSpec((tm, w), lambda j, i: (i, 0))
    p_spec = pl.BlockSpec((w, tn), lambda j, i: (0, j))
    g_specs = []
    for br in range(3):
        g_specs.append(pl.BlockSpec((tm, tn), lambda j, i, br=br: (i, g0 + br * nd + j)))
        if shift:
            g_specs.append(pl.BlockSpec((tm, LANES), lambda j, i, br=br: (i, (g0 + br * nd + j + 1) * (tn // LANES))))
    b_specs = [pl.BlockSpec((1, tn), lambda j, i, br=br: (0, br * nd + j)) for br in range(3)]
    kern = functools.partial(_merge_kernel, shift=shift, ng=ng)
    return pl.pallas_call(
        kern,
        out_shape=jax.ShapeDtypeStruct((s, d), BF16),
        grid=(d // tn, s // tm),
        in_specs=[y_spec, y_spec, y_spec, p_spec, p_spec, p_spec] + g_specs + b_specs,
        out_specs=pl.BlockSpec((tm, tn), lambda j, i: (i, j)),
        compiler_params=_cparams("parallel", "parallel"),
        name="merge",
    )(y_a, y_b, y_c, p_a, p_b, p_c, *([proj] * (3 * ng)), bm, bm, bm)


def _ln_router_kernel(h_ref, g_ref, b_ref, wr_ref, br_ref, x_ref, xp_ref, te_ref, tw_ref, rk_ref, cnt_ref,
                      carry_ref):
    i = pl.program_id(0)

    @pl.when(i == 0)
    def _():
        carry_ref[...] = jnp.zeros_like(carry_ref)

    rows = h_ref.shape[0]
    y = _ln_rows(h_ref[...], g_ref[...], b_ref[...])
    x_ref[...] = y
    xp_ref[...] = _pack_halves(y)
    yh = y.astype(BF16)
    yl = (y - yh.astype(F32)).astype(BF16)
    wr = wr_ref[...]
    wh = wr.astype(BF16)
    wl = (wr - wh.astype(F32)).astype(BF16)
    logits = _dot(yh, wh) + (_dot(yh, wl) + _dot(yl, wh)) + br_ref[...]
    lane = lax.broadcasted_iota(I32, (rows, LANES), 1)
    lane_f = lane.astype(F32)
    cur = logits
    vals, sels, idxs = [], [], []
    for _ in range(TOP_K):
        m = jnp.max(cur, axis=-1, keepdims=True)
        idx = jnp.min(jnp.where(cur == m, lane_f, float(LANES)), axis=-1, keepdims=True)
        sel = lane_f == idx
        vals.append(m)
        sels.append(sel)
        idxs.append(idx)
        cur = jnp.where(sel, -jnp.inf, cur)
    ex = [jnp.exp(v - vals[0]) for v in vals]
    den = ex[0]
    for e in ex[1:]:
        den = den + e
    multi = jnp.zeros((rows, LANES), F32)
    for sel in sels:
        multi = multi + jnp.where(sel, 1.0, 0.0)
    r2 = lax.broadcasted_iota(I32, (rows, rows), 0)
    c2 = lax.broadcasted_iota(I32, (rows, rows), 1)
    lower = jnp.where(r2 > c2, 1.0, 0.0).astype(BF16)
    before = _dot(lower, multi.astype(BF16)) + carry_ref[...]
    te = jnp.zeros((rows, LANES), F32)
    tw = jnp.zeros((rows, LANES), F32)
    rk = jnp.zeros((rows, LANES), F32)
    for k in range(TOP_K):
        rank_k = jnp.sum(jnp.where(sels[k], before, 0.0), axis=-1, keepdims=True)
        te = jnp.where(lane == k, idxs[k], te)
        tw = jnp.where(lane == k, ex[k] / den, tw)
        rk = jnp.where(lane == k, rank_k, rk)
    te_ref[...] = te.astype(I32)
    tw_ref[...] = tw
    rk_ref[...] = rk.astype(I32)
    carry_ref[...] = carry_ref[...] + jnp.sum(multi, axis=0, keepdims=True)
    cnt_ref[...] = carry_ref[...].astype(I32)


def _ln_router(h, g, b, w_router, b_router, *, tm=256):
    s, d = h.shape
    ne = w_router.shape[1]
    tm = min(tm, s)
    wr = jnp.pad(w_router, ((0, 0), (0, LANES - ne)))
    br = jnp.pad(b_router, (0, LANES - ne), constant_values=NEG_BIG).reshape(1, LANES)
    row = lambda width: pl.BlockSpec((tm, width), lambda i: (i, 0))
    const = lambda shape: pl.BlockSpec(shape, lambda i: (0,) * len(shape))
    return pl.pallas_call(
        _ln_router_kernel,
        out_shape=(jax.ShapeDtypeStruct((s, d), F32), jax.ShapeDtypeStruct((s, d // 2), U32),
                   jax.ShapeDtypeStruct((s, LANES), I32), jax.ShapeDtypeStruct((s, LANES), F32),
                   jax.ShapeDtypeStruct((s, LANES), I32), jax.ShapeDtypeStruct((1, LANES), I32)),
        grid=(s // tm,),
        in_specs=[row(d), const((1, d)), const((1, d)), const((d, LANES)), const((1, LANES))],
        out_specs=(row(d), row(d // 2), row(LANES), row(LANES), row(LANES), const((1, LANES))),
        scratch_shapes=[pltpu.VMEM((1, LANES), F32)],
        compiler_params=_cparams("arbitrary"),
        name="ln_router",
    )(h, g.reshape(1, d), b.reshape(1, d), wr, br)


def _row_copy(src_hbm, dst_hbm, sem, src_row, dst_row):
    return pltpu.make_async_copy(src_hbm.at[pl.ds(src_row, 1)], dst_hbm.at[pl.ds(dst_row, 1)], sem)


def _dispatch_kernel(dest_ref, x_ref, init_hbm, o_hbm, sem, *, tokens):
    del init_hbm

    def issue(t, c):
        for k in range(TOP_K):
            _row_copy(x_ref, o_hbm, sem, t, dest_ref[0, 0, t * TOP_K + k]).start()
        return c

    lax.fori_loop(0, tokens, issue, 0)

    def drain(t, c):
        for k in range(TOP_K):
            _row_copy(x_ref, o_hbm, sem, 0, 0).wait()
        return c

    lax.fori_loop(0, tokens, drain, 0)


def _dispatch(xp, dest, n_rows, *, tokens=256):
    s, w = xp.shape
    tokens = min(tokens, s)
    steps = s // tokens
    kern = functools.partial(_dispatch_kernel, tokens=tokens)
    return pl.pallas_call(
        kern,
        out_shape=jax.ShapeDtypeStruct((n_rows, w), xp.dtype),
        grid=(steps,),
        in_specs=[pl.BlockSpec((1, 1, tokens * TOP_K), lambda i: (i, 0, 0), memory_space=pltpu.SMEM),
                  pl.BlockSpec((tokens, w), lambda i: (i, 0)),
                  pl.BlockSpec(memory_space=pl.ANY)],
        out_specs=pl.BlockSpec(memory_space=pl.ANY),
        scratch_shapes=[pltpu.SemaphoreType.DMA],
        input_output_aliases={2: 0},
        compiler_params=_cparams("arbitrary"),
        name="dispatch",
    )(dest.reshape(steps, 1, tokens * TOP_K), xp, jnp.zeros((n_rows, w), xp.dtype))


def _w1_prep_kernel(w_ref, o_ref):
    c = lax.broadcasted_iota(I32, (2 * LANES, 2 * LANES), 0)
    j = lax.broadcasted_iota(I32, (2 * LANES, 2 * LANES), 1)
    src = jnp.where(j < LANES, 2 * j, 2 * (j - LANES) + 1)
    perm = jnp.where(c == src, 1.0, 0.0).astype(BF16)
    o_ref[0] = _dot(w_ref[0].astype(BF16), perm).astype(BF16)


def _w1_prep(w1, layer):
    _, ne, d, f2 = w1.shape
    gw = 2 * LANES
    return pl.pallas_call(
        _w1_prep_kernel,
        out_shape=jax.ShapeDtypeStruct((ne, d, f2), BF16),
        grid=(ne, f2 // gw),
        in_specs=[pl.BlockSpec((None, 1, d, gw), lambda e, j: (layer, e, 0, j))],
        out_specs=pl.BlockSpec((1, d, gw), lambda e, j: (e, 0, j)),
        compiler_params=_cparams("parallel", "parallel"),
        name="w1_prep",
    )(w1)


def _regroup_bias(b1):
    ne, f2 = b1.shape
    return b1.reshape(ne, f2 // (2 * LANES), LANES, 2).transpose(0, 1, 3, 2).reshape(ne, 1, f2)


def _expert_kernel(be_ref, nu_ref, x_ref, w1_ref, b1_ref, w2_ref, b2_ref, o_ref, w2b_ref):
    b = pl.program_id(0)
    used = b < nu_ref[0]
    new_expert = jnp.logical_or(b == 0, be_ref[b] != be_ref[jnp.maximum(b - 1, 0)])

    @pl.when(jnp.logical_and(used, new_expert))
    def _():
        w2b_ref[...] = w2_ref[0].astype(BF16)

    @pl.when(used)
    def _():
        lo, hi = _unpack_halves(x_ref[...])
        half = lo.shape[1]
        gu = _dot(lo, w1_ref[0, :half, :]) + _dot(hi, w1_ref[0, half:, :]) + b1_ref[0]
        acts = []
        for j in range(gu.shape[1] // (2 * LANES)):
            glu = jnp.minimum(gu[:, 2 * j * LANES:(2 * j + 1) * LANES], SWIGLU_LIMIT)
            lin = jnp.clip(gu[:, (2 * j + 1) * LANES:(2 * j + 2) * LANES], -SWIGLU_LIMIT, SWIGLU_LIMIT)
            acts.append(glu * _sigmoid(SWIGLU_ALPHA * glu) * (lin + 1.0))
        act = jnp.concatenate(acts, axis=1)
        o_ref[...] = _pack_halves(_dot(act.astype(BF16), w2b_ref[...]) + b2_ref[0])

    @pl.when(jnp.logical_not(used))
    def _():
        o_ref[...] = jnp.zeros_like(o_ref)


def _experts(xs, block_e, n_used, w1p, b1p, w2, layer, b2):
    n_rows, half = xs.shape
    _, ne, f, d = w2.shape
    n_blocks = n_rows // EXPERT_ROWS

    def blk(b, be, nu):
        return jnp.minimum(b, nu[0] - 1)

    grid_spec = pltpu.PrefetchScalarGridSpec(
        num_scalar_prefetch=2,
        grid=(n_blocks,),
        in_specs=[pl.BlockSpec((EXPERT_ROWS, half), lambda b, be, nu: (blk(b, be, nu), 0)),
                  pl.BlockSpec((1, d, 2 * f), lambda b, be, nu: (be[blk(b, be, nu)], 0, 0)),
                  pl.BlockSpec((1, 1, 2 * f), lambda b, be, nu: (be[blk(b, be, nu)], 0, 0)),
                  pl.BlockSpec((None, 1, f, d), lambda b, be, nu: (layer, be[blk(b, be, nu)], 0, 0)),
                  pl.BlockSpec((1, 1, d), lambda b, be, nu: (be[blk(b, be, nu)], 0, 0))],
        out_specs=pl.BlockSpec((EXPERT_ROWS, d // 2), lambda b, be, nu: (b, 0)),
        scratch_shapes=[pltpu.VMEM((f, d), BF16)],
    )
    return pl.pallas_call(
        _expert_kernel,
        out_shape=jax.ShapeDtypeStruct((n_rows, d // 2), U32),
        grid_spec=grid_spec,
        compiler_params=_cparams("arbitrary"),
        name="experts",
    )(block_e, n_used, xs, w1p, b1p, w2, b2.reshape(ne, 1, d))


def _combine_kernel(dcur_ref, dnxt_ref, tw_ref, x_ref, g_ref, b_ref, ys_hbm, o_ref, ob_ref, buf_ref, sem, *, alpha):
    i = pl.program_id(0)
    tokens, d = x_ref.shape
    half = d // 2
    slot = i % 2

    def copy(s, t, k, src_row):
        return pltpu.make_async_copy(ys_hbm.at[pl.ds(src_row, 1)], buf_ref.at[s, k, pl.ds(t, 1)], sem.at[s])

    def request(d_ref, s):
        def body(t, c):
            for k in range(TOP_K):
                copy(s, t, k, d_ref[0, 0, t * TOP_K + k]).start()
            return c

        lax.fori_loop(0, tokens, body, 0)

    @pl.when(i == 0)
    def _():
        request(dcur_ref, 0)

    @pl.when(i + 1 < pl.num_programs(0))
    def _():
        request(dnxt_ref, 1 - slot)

    def drain(t, c):
        for k in range(TOP_K):
            copy(slot, 0, k, 0).wait()
        return c

    lax.fori_loop(0, tokens, drain, 0)
    tw = tw_ref[...]
    x = x_ref[...]
    y_lo = alpha * x[:, :half]
    y_hi = alpha * x[:, half:]
    for k in range(TOP_K):
        u = buf_ref[slot, k]
        wk = tw[:, k:k + 1]
        y_lo = y_lo + wk * pltpu.bitcast(u << 16, F32)
        y_hi = y_hi + wk * pltpu.bitcast(u & jnp.uint32(0xFFFF0000), F32)
    out = _ln_rows(jnp.concatenate([y_lo, y_hi], axis=1), g_ref[...], b_ref[...])
    o_ref[...] = out
    ob_ref[...] = out.astype(BF16)


def _combine(ys, dest, tw, x, g, b, alpha, *, tokens=128):
    s, d = x.shape
    tokens = min(tokens, s)
    steps = s // tokens
    kern = functools.partial(_combine_kernel, alpha=alpha)
    dest3 = dest.reshape(steps, 1, tokens * TOP_K)
    row = lambda width: pl.BlockSpec((tokens, width), lambda i: (i, 0))
    const = lambda shape: pl.BlockSpec(shape, lambda i: (0,) * len(shape))
    return pl.pallas_call(
        kern,
        out_shape=(jax.ShapeDtypeStruct((s, d), F32), jax.ShapeDtypeStruct((s, d), BF16)),
        grid=(steps,),
        in_specs=[pl.BlockSpec((1, 1, tokens * TOP_K), lambda i: (i, 0, 0), memory_space=pltpu.SMEM),
                  pl.BlockSpec((1, 1, tokens * TOP_K), lambda i: (jnp.minimum(i + 1, steps - 1), 0, 0),
                               memory_space=pltpu.SMEM),
                  row(LANES), row(d), const((1, d)), const((1, d)),
                  pl.BlockSpec(memory_space=pl.ANY)],
        out_specs=(row(d), row(d)),
        scratch_shapes=[pltpu.VMEM((2, TOP_K, tokens, d // 2), U32), pltpu.SemaphoreType.DMA((2,))],
        compiler_params=_cparams("arbitrary"),
        name="combine",
    )(dest3, dest3, tw, x, g.reshape(1, d), b.reshape(1, d), ys)


def _in_layout(sizes):
    z, xbc, dt, u, v, gate_c, x_c, gates = sizes
    lay = {"z": 0, "xs": z, "bc": 2 * z, "dt": z + xbc}
    lay["u"] = lay["dt"] + dt
    lay["v"] = lay["u"] + u
    lay["gate_c"] = lay["v"] + v
    lay["x_c"] = lay["gate_c"] + gate_c
    lay["gates"] = lay["x_c"] + x_c
    return lay


def _moe_plan(te, rk, cnt, ne, n_blocks):
    counts = cnt[0, :ne]
    padded = (counts + EXPERT_ROWS - 1) // EXPERT_ROWS * EXPERT_ROWS
    pad_end = jnp.cumsum(padded)
    pad_start = pad_end - padded
    picked = te[:, :TOP_K, None] == jnp.arange(ne, dtype=I32)[None, None, :]
    dest = (jnp.sum(jnp.where(picked, pad_start[None, None, :], 0), axis=-1) + rk[:, :TOP_K]).astype(I32)
    dest = dest.reshape(-1)
    block_start = jnp.arange(n_blocks, dtype=I32) * EXPERT_ROWS
    block_e = jnp.sum(pad_end[None, :] <= block_start[:, None], axis=1)
    block_e = jnp.minimum(block_e, ne - 1).astype(I32)
    n_used = (pad_end[-1] // EXPERT_ROWS).astype(I32).reshape(1)
    return dest, block_e, n_used


def kernel(x, ln_emb_g, ln_emb_b, w_in, b_merge, conv_a_w, conv_a_b, dt_bias, a_log, d_skip, norm_a_g,
           ln_v_g, ln_v_b, w_spatial, b_spatial, conv_c_w, conv_c_b, w_rg_a, b_rg_a, w_rg_x, b_rg_x, lam,
           p_a, p_b, p_c, w_o, ln_mix_g, ln_mix_b, w_router, b_router, w1, b1, w2, b2, ln_ffn_g, ln_ffn_b):
    bsz, seq, d = x.shape
    depth = w_in.shape[0]
    alpha = (2 * depth) ** 0.25
    heads = dt_bias.shape[1]
    inner = norm_a_g.shape[1]
    sizes = (inner, conv_a_w.shape[2], heads, ln_v_g.shape[1], ln_v_g.shape[1], lam.shape[1], lam.shape[1],
             3 * d)
    lay = _in_layout(sizes)
    ne = w_router.shape[2]
    s = bsz * seq
    n_rows = s * TOP_K + ne * EXPERT_ROWS
    n_blocks = n_rows // EXPERT_ROWS

    xf, xb = _ln0(x.reshape(s, d), ln_emb_g, ln_emb_b)
    for l in range(depth):
        proj = _matmul(xb, w_in, l, F32, tm=512, tn=1024, name="in_proj")
        y_a = _ssd(proj, lay, conv_a_w[l], conv_a_b[l], dt_bias[l], a_log[l], d_skip[l], norm_a_g[l])
        y_b = _gmlp(proj, lay, ln_v_g[l], ln_v_b[l], w_spatial[l], b_spatial[l])
        y_c = _lru(proj, lay, conv_c_w[l], conv_c_b[l], w_rg_a[l], b_rg_a[l], w_rg_x[l], b_rg_x[l], lam[l])
        merged = _merge(y_a, y_b, y_c, p_a[l].astype(BF16), p_b[l].astype(BF16), p_c[l].astype(BF16),
                        proj, lay, b_merge[l])
        h = _matmul(merged, w_o, l, F32, tm=1024, tn=512, res=xf, alpha=alpha, name="w_o")
        xm, xp, te, tw, rk, cnt = _ln_router(h, ln_mix_g[l], ln_mix_b[l], w_router[l], b_router[l])
        dest, block_e, n_used = _moe_plan(te, rk, cnt, ne, n_blocks)
        xs = _dispatch(xp, dest, n_rows)
        ys = _experts(xs, block_e, n_used, _w1_prep(w1, l), _regroup_bias(b1[l]), w2, l, b2[l])
        xf, xb = _combine(ys, dest, tw, xm, ln_ffn_g[l], ln_ffn_b[l], alpha)
    return xf.reshape(bsz, seq, d)
```

```python
import functools
import math

import jax
import jax.numpy as jnp
from jax import lax
from jax.experimental import pallas as pl
from jax.experimental.pallas import tpu as pltpu

F32 = jnp.float32
BF16 = jnp.bfloat16
U32 = jnp.uint32
I32 = jnp.int32

SSD_GROUPS = 4
SSD_STATE = 128
CHUNK = 128
GMLP_GROUPS = 8
LRU_C = 8.0
TOP_K = 4
SWIGLU_LIMIT = 7.0
SWIGLU_ALPHA = 1.702
LN_EPS = 1e-5
RMS_EPS = 1e-5

LANES = 128
SUBLANES = 8
VMEM_LIMIT = 56 * 1024 * 1024
EXPERT_ROWS = 256
WINDOW_BLOCK = 1024
NEG_BIG = -1e30


def _cparams(*sem):
    return pltpu.CompilerParams(dimension_semantics=sem, vmem_limit_bytes=VMEM_LIMIT)


def _dot(a, b):
    return jnp.dot(a, b, preferred_element_type=F32)


def _split3(x):
    hi = x.astype(BF16)
    r = x - hi.astype(F32)
    mid = r.astype(BF16)
    lo = (r - mid.astype(F32)).astype(BF16)
    return hi, mid, lo


def _exact_right(x, m):
    hi, mid, lo = _split3(x)
    return _dot(hi, m) + (_dot(mid, m) + _dot(lo, m))


def _exact_left(m, x):
    hi, mid, lo = _split3(x)
    return _dot(m, hi) + (_dot(m, mid) + _dot(m, lo))


def _sigmoid(x):
    return 1.0 / (1.0 + jnp.exp(-x))


def _silu(x):
    return x * _sigmoid(x)


def _softplus(x):
    return jnp.maximum(x, 0.0) + jnp.log1p(jnp.exp(-jnp.abs(x)))


def _gelu(x):
    c = math.sqrt(2.0 / math.pi)
    return x * (0.5 * (1.0 + jnp.tanh(c * (x + 0.044715 * (x * x * x)))))


def _ln_rows(x, g, b):
    mu = jnp.mean(x, axis=-1, keepdims=True)
    xc = x - mu
    var = jnp.mean(xc * xc, axis=-1, keepdims=True)
    return xc * lax.rsqrt(var + LN_EPS) * g + b


def _causal_conv(x, carry, w, b):
    kw = w.shape[0]
    rows = x.shape[0]
    xp = jnp.concatenate([carry, x], axis=0)
    y = x * w[kw - 1:kw, :] + b
    for k in range(kw - 1):
        s = SUBLANES - (kw - 1) + k
        y = y + xp[s:s + rows, :] * w[k:k + 1, :]
    return y


def _pack_halves(y):
    half = y.shape[1] // 2
    lo = pltpu.bitcast(y[:, :half].astype(BF16).astype(F32), U32)
    hi = pltpu.bitcast(y[:, half:].astype(BF16).astype(F32), U32)
    return (hi & jnp.uint32(0xFFFF0000)) | (lo >> 16)


def _unpack_halves(u):
    lo = pltpu.bitcast(u << 16, F32).astype(BF16)
    hi = pltpu.bitcast(u & jnp.uint32(0xFFFF0000), F32).astype(BF16)
    return lo, hi


def _window_specs(off, width, rows, blk):
    first, shift = divmod(off, blk)
    nblk = pl.cdiv(shift + width, blk)
    specs = [pl.BlockSpec((rows, blk), lambda i, q=q: (i, first + q)) for q in range(nblk)]
    return specs, shift


def _window_load(refs, shift, width, rows=slice(None)):
    blk = refs[0].shape[1]
    first, lane_shift = divmod(shift, LANES)
    n_out = width // LANES

    def lane_block(k):
        q, o = divmod(k * LANES, blk)
        return refs[q][rows, o:o + LANES].astype(F32)

    if lane_shift == 0:
        return jnp.concatenate([lane_block(first + k) for k in range(n_out)], axis=1)
    rolled = [pltpu.roll(lane_block(first + k), LANES - lane_shift, 1) for k in range(n_out + 1)]
    lane = lax.broadcasted_iota(I32, rolled[0].shape, 1)
    outs = [jnp.where(lane < LANES - lane_shift, rolled[k], rolled[k + 1]) for k in range(n_out)]
    return jnp.concatenate(outs, axis=1)


def _ln0_kernel(x_ref, g_ref, b_ref, o_ref, ob_ref):
    y = _ln_rows(x_ref[...], g_ref[...], b_ref[...])
    o_ref[...] = y
    ob_ref[...] = y.astype(BF16)


def _ln0(x, g, b):
    s, d = x.shape
    tm = min(256, s)
    return pl.pallas_call(
        _ln0_kernel,
        out_shape=(jax.ShapeDtypeStruct((s, d), F32), jax.ShapeDtypeStruct((s, d), BF16)),
        grid=(s // tm,),
        in_specs=[pl.BlockSpec((tm, d), lambda i: (i, 0)),
                  pl.BlockSpec((1, d), lambda i: (0, 0)),
                  pl.BlockSpec((1, d), lambda i: (0, 0))],
        out_specs=(pl.BlockSpec((tm, d), lambda i: (i, 0)),
                   pl.BlockSpec((tm, d), lambda i: (i, 0))),
        compiler_params=_cparams("parallel"),
        name="ln0",
    )(x, g.reshape(1, d), b.reshape(1, d))


def _cast_weight_tile(w_ref, wb_ref, n_valid):
    tn = w_ref.shape[1]
    col = pl.program_id(0) * tn + lax.broadcasted_iota(I32, w_ref.shape, 1)
    wb_ref[...] = jnp.where(col < n_valid, w_ref[...], 0.0).astype(BF16)


def _mm_kernel(a_ref, w_ref, o_ref, wb_ref, *, n_valid):
    @pl.when(pl.program_id(1) == 0)
    def _():
        _cast_weight_tile(w_ref, wb_ref, n_valid)

    o_ref[...] = _dot(a_ref[...], wb_ref[...]).astype(o_ref.dtype)


def _mm_res_kernel(a_ref, w_ref, r_ref, o_ref, wb_ref, *, n_valid, alpha):
    @pl.when(pl.program_id(1) == 0)
    def _():
        _cast_weight_tile(w_ref, wb_ref, n_valid)

    o_ref[...] = (alpha * r_ref[...] + _dot(a_ref[...], wb_ref[...])).astype(o_ref.dtype)


def _matmul(a, w, layer, out_dtype, *, tm, tn, res=None, alpha=1.0, name="mm"):
    m, k = a.shape
    n = w.shape[2]
    tm = min(tm, m)
    tn = min(tn, n)
    n_tiles = pl.cdiv(n, tn)
    in_specs = [pl.BlockSpec((tm, k), lambda j, i: (i, 0)),
                pl.BlockSpec((None, k, tn), lambda j, i: (layer, 0, j))]
    args = [a, w]
    if res is None:
        kern = functools.partial(_mm_kernel, n_valid=n)
    else:
        kern = functools.partial(_mm_res_kernel, n_valid=n, alpha=alpha)
        in_specs.append(pl.BlockSpec((tm, tn), lambda j, i: (i, j)))
        args.append(res)
    return pl.pallas_call(
        kern,
        out_shape=jax.ShapeDtypeStruct((m, n_tiles * tn), out_dtype),
        grid=(n_tiles, m // tm),
        in_specs=in_specs,
        out_specs=pl.BlockSpec((tm, tn), lambda j, i: (i, j)),
        scratch_shapes=[pltpu.VMEM((k, tn), BF16)],
        compiler_params=_cparams("parallel", "arbitrary"),
        name=name,
    )(*args)


def _ssd_kernel(z_ref, xs_ref, bc_ref, dt_ref, cwx_ref, cbx_ref, cwb_ref, cbb_ref, dtb_ref, alog_ref,
                dskip_ref, ng_ref, e_ref, o_ref, cx_ref, cb_ref, st_ref, *, heads, head_dim):
    i = pl.program_id(0)

    @pl.when(i == 0)
    def _():
        cx_ref[...] = jnp.zeros_like(cx_ref)
        cb_ref[...] = jnp.zeros_like(cb_ref)
        st_ref[...] = jnp.zeros_like(st_ref)

    n = SSD_STATE
    gw = (heads // SSD_GROUPS) * head_dim
    hpg = heads // SSD_GROUPS
    pair = LANES // head_dim

    xs_raw = xs_ref[...].astype(F32)
    bc_raw = bc_ref[...].astype(F32)
    xs = _silu(_causal_conv(xs_raw, cx_ref[...], cwx_ref[...], cbx_ref[...]))
    bc = _silu(_causal_conv(bc_raw, cb_ref[...], cwb_ref[...], cbb_ref[...]))
    cx_ref[...] = xs_raw[CHUNK - SUBLANES:, :]
    cb_ref[...] = bc_raw[CHUNK - SUBLANES:, :]

    row = lax.broadcasted_iota(I32, (CHUNK, CHUNK), 0)
    col = lax.broadcasted_iota(I32, (CHUNK, CHUNK), 1)
    dt = _softplus(jnp.where(col < heads, dt_ref[...], 0.0) + dtb_ref[...])
    da = dt * (-jnp.exp(alog_ref[...]))
    causal = row >= col
    ltri = jnp.where(causal, 1.0, 0.0).astype(BF16)
    cum = _exact_left(ltri, da)
    cum_t = cum.T
    expand = e_ref[...]
    cum_e = _exact_right(cum, expand)
    dt_e = _exact_right(dt, expand)
    last_e = cum_e[CHUNK - 1:CHUNK, :]
    decay_out = jnp.exp(cum_e)
    decay_st = jnp.exp(last_e - cum_e)
    chunk_decay = jnp.exp(last_e)
    xdt = xs * dt_e
    xdt_b = xdt.astype(BF16)
    xst_b = (xdt * decay_st).astype(BF16)
    lane = lax.broadcasted_iota(I32, (CHUNK, LANES), 1)

    for g in range(SSD_GROUPS):
        bg = bc[:, g * n:(g + 1) * n].astype(BF16)
        cg = bc[:, (SSD_GROUPS + g) * n:(SSD_GROUPS + g + 1) * n].astype(BF16)
        cb = lax.dot_general(cg, bg, (((1,), (1,)), ((), ())), preferred_element_type=F32)
        slabs = []
        for j in range(hpg // pair):
            h0 = g * hpg + j * pair
            sc = []
            for h in range(h0, h0 + pair):
                seg = cum[:, h:h + 1] - cum_t[h:h + 1, :]
                dec = jnp.exp(jnp.where(causal, seg, -jnp.inf))
                sc.append((cb * dec).astype(BF16))
            xp = xdt_b[:, h0 * head_dim:h0 * head_dim + LANES]
            blocks = [jnp.where((lane >= q * head_dim) & (lane < (q + 1) * head_dim), xp, jnp.zeros_like(xp))
                      for q in range(pair)]
            slabs.append(_dot(jnp.concatenate(sc, axis=1), jnp.concatenate(blocks, axis=0)))
        y_diag = jnp.concatenate(slabs, axis=1)
        cs = slice(g * gw, (g + 1) * gw)
        st = st_ref[g]
        y_off = _dot(cg, st.astype(BF16)) * decay_out[:, cs]
        st_new = lax.dot_general(bg, xst_b[:, cs], (((0,), (0,)), ((), ())), preferred_element_type=F32)
        st_ref[g] = st * chunk_decay[:, cs] + st_new
        y = y_diag + y_off + xs[:, cs] * dskip_ref[:, cs]
        yg = y * _silu(z_ref[:, cs].astype(F32))
        ms = jnp.mean(yg * yg, axis=-1, keepdims=True)
        o_ref[:, cs] = (yg * lax.rsqrt(ms + RMS_EPS) * ng_ref[:, cs]).astype(o_ref.dtype)


def _ssd(proj, dt_raw, lay, conv_w, conv_b, dt_bias, a_log, d_skip, norm_g):
    s = proj.shape[0]
    heads = dt_bias.shape[0]
    inner = norm_g.shape[0]
    head_dim = inner // heads
    bcw = 2 * SSD_GROUPS * SSD_STATE
    gw = inner // SSD_GROUPS
    pad = LANES - heads
    assert lay["z"] % inner == 0 and lay["xs"] % inner == 0 and lay["bc"] % bcw == 0
    expand = (jnp.arange(LANES)[:, None] == (jnp.arange(inner) // head_dim)[None, :]).astype(BF16)
    kern = functools.partial(_ssd_kernel, heads=heads, head_dim=head_dim)
    const = lambda shape: pl.BlockSpec(shape, lambda i: (0,) * len(shape))
    return pl.pallas_call(
        kern,
        out_shape=jax.ShapeDtypeStruct((s, inner), BF16),
        grid=(s // CHUNK,),
        in_specs=[pl.BlockSpec((CHUNK, inner), lambda i: (i, lay["z"] // inner)),
                  pl.BlockSpec((CHUNK, inner), lambda i: (i, lay["xs"] // inner)),
                  pl.BlockSpec((CHUNK, bcw), lambda i: (i, lay["bc"] // bcw)),
                  pl.BlockSpec((CHUNK, LANES), lambda i: (i, 0)),
                  const((conv_w.shape[0], inner)), const((1, inner)),
                  const((conv_w.shape[0], bcw)), const((1, bcw)),
                  const((1, LANES)), const((1, LANES)),
                  const((1, inner)), const((1, inner)), const((LANES, inner))],
        out_specs=pl.BlockSpec((CHUNK, inner), lambda i: (i, 0)),
        scratch_shapes=[pltpu.VMEM((SUBLANES, inner), F32), pltpu.VMEM((SUBLANES, bcw), F32),
                        pltpu.VMEM((SSD_GROUPS, SSD_STATE, gw), F32)],
        compiler_params=_cparams("arbitrary"),
        name="ssd",
    )(proj, proj, proj, dt_raw,
      conv_w[:, :inner], conv_b[:inner].reshape(1, inner),
      conv_w[:, inner:], conv_b[inner:].reshape(1, bcw),
      jnp.pad(dt_bias, (0, pad)).reshape(1, LANES), jnp.pad(a_log, (0, pad)).reshape(1, LANES),
      jnp.repeat(d_skip, head_dim).reshape(1, inner), norm_g.reshape(1, inner), expand)


def _gmlp_kernel(*refs, chunks, nu, nv, su, sv):
    u_refs, v_refs = refs[:nu], refs[nu:nu + nv]
    lg_ref, lb_ref, ws_ref, be_ref, o_ref = refs[nu + nv:]
    width = o_ref.shape[1]
    gw = width // GMLP_GROUPS
    row = lax.broadcasted_iota(I32, (CHUNK, CHUNK), 0)
    col = lax.broadcasted_iota(I32, (CHUNK, CHUNK), 1)
    tril = row >= col
    ws = [jnp.where(tril, ws_ref[g], 0.0).astype(BF16) for g in range(GMLP_GROUPS)]
    for c in range(chunks):
        rs = slice(c * CHUNK, (c + 1) * CHUNK)
        v = _ln_rows(_gelu(_window_load(v_refs, sv, width, rs)), lg_ref[...], lb_ref[...]).astype(BF16)
        u = _gelu(_window_load(u_refs, su, width, rs))
        for g in range(GMLP_GROUPS):
            cs = slice(g * gw, (g + 1) * gw)
            mixed = _dot(ws[g], v[:, cs]) + be_ref[:, cs]
            o_ref[rs, cs] = (u[:, cs] * mixed).astype(o_ref.dtype)


def _gmlp(proj, lay, ln_g, ln_b, w_s, b_s, *, rows=256):
    s = proj.shape[0]
    width = ln_g.shape[0]
    gw = width // GMLP_GROUPS
    rows = min(rows, s)
    bias_e = jnp.repeat(b_s.T, gw, axis=1)
    u_specs, su = _window_specs(lay["u"], width, rows, WINDOW_BLOCK)
    v_specs, sv = _window_specs(lay["v"], width, rows, WINDOW_BLOCK)
    kern = functools.partial(_gmlp_kernel, chunks=rows // CHUNK, nu=len(u_specs), nv=len(v_specs), su=su, sv=sv)
    const = lambda shape: pl.BlockSpec(shape, lambda i: (0,) * len(shape))
    return pl.pallas_call(
        kern,
        out_shape=jax.ShapeDtypeStruct((s, width), BF16),
        grid=(s // rows,),
        in_specs=u_specs + v_specs + [const((1, width)), const((1, width)),
                                      const((GMLP_GROUPS, CHUNK, CHUNK)), const((CHUNK, width))],
        out_specs=pl.BlockSpec((rows, width), lambda i: (i, 0)),
        compiler_params=_cparams("parallel"),
        name="gmlp",
    )(*([proj] * (len(u_specs) + len(v_specs))), ln_g.reshape(1, width), ln_b.reshape(1, width), w_s, bias_e)


def _lru_kernel(*refs, ng, nx, sg, sx):
    gate_refs, x_refs = refs[:ng], refs[ng:ng + nx]
    cw_ref, cb_ref, wa_ref, ba_ref, wx_ref, bx_ref, lam_ref, o_ref, carry_ref, h_ref = refs[ng + nx:]
    i = pl.program_id(0)

    @pl.when(i == 0)
    def _():
        carry_ref[...] = jnp.zeros_like(carry_ref)
        h_ref[...] = jnp.zeros_like(h_ref)

    rows, width = o_ref.shape
    nblk, blk = wa_ref.shape[0], wa_ref.shape[1]
    x_raw = _window_load(x_refs, sx, width)
    xr = _causal_conv(x_raw, carry_ref[...], cw_ref[...], cb_ref[...])
    carry_ref[...] = x_raw[rows - SUBLANES:, :]
    xb = xr.astype(BF16)
    ra = jnp.concatenate([_dot(xb[:, q * blk:(q + 1) * blk], wa_ref[q]) for q in range(nblk)], axis=1)
    ia = jnp.concatenate([_dot(xb[:, q * blk:(q + 1) * blk], wx_ref[q]) for q in range(nblk)], axis=1)
    r = _sigmoid(ra + ba_ref[...])
    ig = _sigmoid(ia + bx_ref[...])
    log_a = (-LRU_C) * r * _softplus(-lam_ref[...])
    a = jnp.exp(log_a)
    b = jnp.sqrt((1.0 - a) * (1.0 + a)) * (ig * xr)
    ridx = lax.broadcasted_iota(I32, (SUBLANES, width), 0)
    h_in = h_ref[...]
    groups = []
    for g in range(rows // SUBLANES):
        gs = slice(g * SUBLANES, (g + 1) * SUBLANES)
        ag, bg = a[gs, :], b[gs, :]
        d = 1
        while d < SUBLANES:
            keep = ridx >= d
            bg = jnp.where(keep, ag * pltpu.roll(bg, d, 0) + bg, bg)
            ag = jnp.where(keep, ag * pltpu.roll(ag, d, 0), ag)
            d *= 2
        hg = ag * h_in + bg
        h_in = hg[SUBLANES - 1:SUBLANES, :]
        groups.append(hg)
    h = jnp.concatenate(groups, axis=0)
    h_ref[...] = h_in
    o_ref[...] = (h * _gelu(_window_load(gate_refs, sg, width))).astype(o_ref.dtype)


def _lru(proj, lay, conv_w, conv_b, w_a, b_a, w_x, b_x, lam, *, rows=128):
    s = proj.shape[0]
    width = lam.shape[0]
    nblk, blk = w_a.shape[0], w_a.shape[1]
    rows = min(rows, s)
    g_specs, sg = _window_specs(lay["gate_c"], width, rows, WINDOW_BLOCK)
    x_specs, sx = _window_specs(lay["x_c"], width, rows, WINDOW_BLOCK)
    kern = functools.partial(_lru_kernel, ng=len(g_specs), nx=len(x_specs), sg=sg, sx=sx)
    const = lambda shape: pl.BlockSpec(shape, lambda i: (0,) * len(shape))
    return pl.pallas_call(
        kern,
        out_shape=jax.ShapeDtypeStruct((s, width), BF16),
        grid=(s // rows,),
        in_specs=g_specs + x_specs + [const((conv_w.shape[0], width)), const((1, width)),
                                      const((nblk, blk, blk)), const((1, width)),
                                      const((nblk, blk, blk)), const((1, width)), const((1, width))],
        out_specs=pl.BlockSpec((rows, width), lambda i: (i, 0)),
        scratch_shapes=[pltpu.VMEM((SUBLANES, width), F32), pltpu.VMEM((1, width), F32)],
        compiler_params=_cparams("arbitrary"),
        name="rglru",
    )(*([proj] * (len(g_specs) + len(x_specs))), conv_w, conv_b.reshape(1, width), w_a.astype(BF16),
      b_a.reshape(1, width), w_x.astype(BF16), b_x.reshape(1, width), lam.reshape(1, width))


def _merge_kernel(*refs, shift, ng):
    y_refs, p_refs = refs[:3], refs[3:6]
    gate_refs = [refs[6 + br * ng:6 + (br + 1) * ng] for br in range(3)]
    b_refs = refs[6 + 3 * ng:9 + 3 * ng]
    o_ref = refs[9 + 3 * ng]
    tn = o_ref.shape[1]
    acc = None
    for br in range(3):
        gate = _sigmoid(_window_load(gate_refs[br], shift, tn) + b_refs[br][...])
        term = gate * _dot(y_refs[br][...], p_refs[br][...])
        acc = term if acc is None else acc + term
    o_ref[...] = acc.astype(o_ref.dtype)


def _merge(y_a, y_b, y_c, p_a, p_b, p_c, proj, lay, b_merge, *, tm=256, tn=1024):
    s, w = y_a.shape
    d = p_a.shape[1]
    tm = min(tm, s)
    tn = min(tn, d)
    g0, shift = divmod(lay["gates"], tn)
    assert shift < LANES
    ng = 2 if shift else 1
    nd = d // tn
    bm = b_merge.reshape(1, 3 * d)
    y_spec = pl.BlockSpec((tm, w), lambda j, i: (i, 0))
    p_spec = pl.BlockSpec((w, tn), lambda j, i: (0, j))
    g_specs = []
    for br in range(3):
        g_specs.append(pl.BlockSpec((tm, tn), lambda j, i, br=br: (i, g0 + br * nd + j)))
        if shift:
            g_specs.append(pl.BlockSpec((tm, LANES), lambda j, i, br=br: (i, (g0 + br * nd + j + 1) * (tn // LANES))))
    b_specs = [pl.BlockSpec((1, tn), lambda j, i, br=br: (0, br * nd + j)) for br in range(3)]
    kern = functools.partial(_merge_kernel, shift=shift, ng=ng)
    return pl.pallas_call(
        kern,
        out_shape=jax.ShapeDtypeStruct((s, d), BF16),
        grid=(d // tn, s // tm),
        in_specs=[y_spec, y_spec, y_spec, p_spec, p_spec, p_spec] + g_specs + b_specs,
        out_specs=pl.BlockSpec((tm, tn), lambda j, i: (i, j)),
        compiler_params=_cparams("parallel", "parallel"),
        name="merge",
    )(y_a, y_b, y_c, p_a, p_b, p_c, *([proj] * (3 * ng)), bm, bm, bm)


def _ln_router_kernel(h_ref, g_ref, b_ref, wr_ref, br_ref, x_ref, xp_ref, te_ref, tw_ref, rk_ref, cnt_ref,
                      carry_ref):
    i = pl.program_id(0)

    @pl.when(i == 0)
    def _():
        carry_ref[...] = jnp.zeros_like(carry_ref)

    rows = h_ref.shape[0]
    y = _ln_rows(h_ref[...], g_ref[...], b_ref[...])
    x_ref[...] = y
    xp_ref[...] = _pack_halves(y)
    yh = y.astype(BF16)
    yl = (y - yh.astype(F32)).astype(BF16)
    wr = wr_ref[...]
    wh = wr.astype(BF16)
    wl = (wr - wh.astype(F32)).astype(BF16)
    logits = _dot(yh, wh) + (_dot(yh, wl) + _dot(yl, wh)) + br_ref[...]
    lane = lax.broadcasted_iota(I32, (rows, LANES), 1)
    lane_f = lane.astype(F32)
    cur = logits
    vals, sels, idxs = [], [], []
    for _ in range(TOP_K):
        m = jnp.max(cur, axis=-1, keepdims=True)
        idx = jnp.min(jnp.where(cur == m, lane_f, float(LANES)), axis=-1, keepdims=True)
        sel = lane_f == idx
        vals.append(m)
        sels.append(sel)
        idxs.append(idx)
        cur = jnp.where(sel, -jnp.inf, cur)
    ex = [jnp.exp(v - vals[0]) for v in vals]
    den = ex[0]
    for e in ex[1:]:
        den = den + e
    multi = jnp.zeros((rows, LANES), F32)
    for sel in sels:
        multi = multi + jnp.where(sel, 1.0, 0.0)
    r2 = lax.broadcasted_iota(I32, (rows, rows), 0)
    c2 = lax.broadcasted_iota(I32, (rows, rows), 1)
    lower = jnp.where(r2 > c2, 1.0, 0.0).astype(BF16)
    before = _dot(lower, multi.astype(BF16)) + carry_ref[...]
    te = jnp.zeros((rows, LANES), F32)
    tw = jnp.zeros((rows, LANES), F32)
    rk = jnp.zeros((rows, LANES), F32)
    for k in range(TOP_K):
        rank_k = jnp.sum(jnp.where(sels[k], before, 0.0), axis=-1, keepdims=True)
        te = jnp.where(lane == k, idxs[k], te)
        tw = jnp.where(lane == k, ex[k] / den, tw)
        rk = jnp.where(lane == k, rank_k, rk)
    te_ref[...] = te.astype(I32)
    tw_ref[...] = tw
    rk_ref[...] = rk.astype(I32)
    carry_ref[...] = carry_ref[...] + jnp.sum(multi, axis=0, keepdims=True)
    cnt_ref[...] = carry_ref[...].astype(I32)


def _ln_router(h, g, b, w_router, b_router, *, tm=256):
    s, d = h.shape
    ne = w_router.shape[1]
    tm = min(tm, s)
    wr = jnp.pad(w_router, ((0, 0), (0, LANES - ne)))
    br = jnp.pad(b_router, (0, LANES - ne), constant_values=NEG_BIG).reshape(1, LANES)
    row = lambda width: pl.BlockSpec((tm, width), lambda i: (i, 0))
    const = lambda shape: pl.BlockSpec(shape, lambda i: (0,) * len(shape))
    return pl.pallas_call(
        _ln_router_kernel,
        out_shape=(jax.ShapeDtypeStruct((s, d), F32), jax.ShapeDtypeStruct((s, d // 2), U32),
                   jax.ShapeDtypeStruct((s, LANES), I32), jax.ShapeDtypeStruct((s, LANES), F32),
                   jax.ShapeDtypeStruct((s, LANES), I32), jax.ShapeDtypeStruct((1, LANES), I32)),
        grid=(s // tm,),
        in_specs=[row(d), const((1, d)), const((1, d)), const((d, LANES)), const((1, LANES))],
        out_specs=(row(d), row(d // 2), row(LANES), row(LANES), row(LANES), const((1, LANES))),
        scratch_shapes=[pltpu.VMEM((1, LANES), F32)],
        compiler_params=_cparams("arbitrary"),
        name="ln_router",
    )(h, g.reshape(1, d), b.reshape(1, d), wr, br)


def _row_copy(src_hbm, dst_hbm, sem, src_row, dst_row):
    return pltpu.make_async_copy(src_hbm.at[pl.ds(src_row, 1)], dst_hbm.at[pl.ds(dst_row, 1)], sem)


def _dispatch_kernel(dest_ref, end_ref, cnt_ref, x_ref, o_hbm, zero_ref, sem, zsem, *, tokens):
    @pl.when(pl.program_id(0) == 0)
    def _():
        zero_ref[...] = jnp.zeros_like(zero_ref)

        def fill(e):
            start = pl.multiple_of(end_ref[e] - EXPERT_ROWS, EXPERT_ROWS)
            return pltpu.make_async_copy(zero_ref, o_hbm.at[pl.ds(start, EXPERT_ROWS)], zsem)

        for e in range(end_ref.shape[0]):
            @pl.when(cnt_ref[e] > 0)
            def _():
                fill(e).start()

        for e in range(end_ref.shape[0]):
            @pl.when(cnt_ref[e] > 0)
            def _():
                fill(e).wait()

        def tail(b):
            return pltpu.make_async_copy(
                zero_ref, o_hbm.at[pl.ds(pl.multiple_of(b * EXPERT_ROWS, EXPERT_ROWS), EXPERT_ROWS)], zsem)

        n_used = end_ref[end_ref.shape[0] - 1] // EXPERT_ROWS
        n_blocks = o_hbm.shape[0] // EXPERT_ROWS
        lax.fori_loop(n_used, n_blocks, lambda b, c: (tail(b).start(), c)[1], 0)
        lax.fori_loop(n_used, n_blocks, lambda b, c: (tail(b).wait(), c)[1], 0)

    def issue(t, c):
        for k in range(TOP_K):
            _row_copy(x_ref, o_hbm, sem, t, dest_ref[0, 0, t * TOP_K + k]).start()
        return c

    lax.fori_loop(0, tokens, issue, 0)

    def drain(t, c):
        for k in range(TOP_K):
            _row_copy(x_ref, o_hbm, sem, 0, 0).wait()
        return c

    lax.fori_loop(0, tokens, drain, 0)


def _dispatch(xp, dest, pad_end, counts, n_rows, *, tokens=256):
    s, w = xp.shape
    tokens = min(tokens, s)
    steps = s // tokens
    kern = functools.partial(_dispatch_kernel, tokens=tokens)
    return pl.pallas_call(
        kern,
        out_shape=jax.ShapeDtypeStruct((n_rows, w), xp.dtype),
        grid=(steps,),
        in_specs=[pl.BlockSpec((1, 1, tokens * TOP_K), lambda i: (i, 0, 0), memory_space=pltpu.SMEM),
                  pl.BlockSpec(memory_space=pltpu.SMEM),
                  pl.BlockSpec(memory_space=pltpu.SMEM),
                  pl.BlockSpec((tokens, w), lambda i: (i, 0))],
        out_specs=pl.BlockSpec(memory_space=pl.ANY),
        scratch_shapes=[pltpu.VMEM((EXPERT_ROWS, w), xp.dtype), pltpu.SemaphoreType.DMA,
                        pltpu.SemaphoreType.DMA],
        compiler_params=_cparams("arbitrary"),
        name="dispatch",
    )(dest.reshape(steps, 1, tokens * TOP_K), pad_end, counts, xp)


def _w1_prep_kernel(w_ref, o_ref):
    c = lax.broadcasted_iota(I32, (2 * LANES, 2 * LANES), 0)
    j = lax.broadcasted_iota(I32, (2 * LANES, 2 * LANES), 1)
    src = jnp.where(j < LANES, 2 * j, 2 * (j - LANES) + 1)
    perm = jnp.where(c == src, 1.0, 0.0).astype(BF16)
    o_ref[0] = _dot(w_ref[0].astype(BF16), perm).astype(BF16)


def _w1_prep(w1, layer):
    _, ne, d, f2 = w1.shape
    gw = 2 * LANES
    return pl.pallas_call(
        _w1_prep_kernel,
        out_shape=jax.ShapeDtypeStruct((ne, d, f2), BF16),
        grid=(ne, f2 // gw),
        in_specs=[pl.BlockSpec((None, 1, d, gw), lambda e, j: (layer, e, 0, j))],
        out_specs=pl.BlockSpec((1, d, gw), lambda e, j: (e, 0, j)),
        compiler_params=_cparams("parallel", "parallel"),
        name="w1_prep",
    )(w1)


def _regroup_bias(b1):
    ne, f2 = b1.shape
    return b1.reshape(ne, f2 // (2 * LANES), LANES, 2).transpose(0, 1, 3, 2).reshape(ne, 1, f2)


def _expert_kernel(be_ref, nu_ref, x_ref, w1_ref, b1_ref, w2_ref, b2_ref, o_ref, w2b_ref):
    b = pl.program_id(0)
    used = b < nu_ref[0]
    new_expert = jnp.logical_or(b == 0, be_ref[b] != be_ref[jnp.maximum(b - 1, 0)])

    @pl.when(jnp.logical_and(used, new_expert))
    def _():
        w2b_ref[...] = w2_ref[0].astype(BF16)

    @pl.when(used)
    def _():
        lo, hi = _unpack_halves(x_ref[...])
        half = lo.shape[1]
        gu = _dot(lo, w1_ref[0, :half, :]) + _dot(hi, w1_ref[0, half:, :]) + b1_ref[0]
        acts = []
        for j in range(gu.shape[1] // (2 * LANES)):
            glu = jnp.minimum(gu[:, 2 * j * LANES:(2 * j + 1) * LANES], SWIGLU_LIMIT)
            lin = jnp.clip(gu[:, (2 * j + 1) * LANES:(2 * j + 2) * LANES], -SWIGLU_LIMIT, SWIGLU_LIMIT)
            acts.append(glu * _sigmoid(SWIGLU_ALPHA * glu) * (lin + 1.0))
        act = jnp.concatenate(acts, axis=1)
        o_ref[...] = _pack_halves(_dot(act.astype(BF16), w2b_ref[...]) + b2_ref[0])

    @pl.when(jnp.logical_not(used))
    def _():
        o_ref[...] = jnp.zeros_like(o_ref)


def _experts(xs, block_e, n_used, w1p, b1p, w2, layer, b2):
    n_rows, half = xs.shape
    _, ne, f, d = w2.shape
    n_blocks = n_rows // EXPERT_ROWS

    def blk(b, be, nu):
        return jnp.minimum(b, nu[0] - 1)

    grid_spec = pltpu.PrefetchScalarGridSpec(
        num_scalar_prefetch=2,
        grid=(n_blocks,),
        in_specs=[pl.BlockSpec((EXPERT_ROWS, half), lambda b, be, nu: (blk(b, be, nu), 0)),
                  pl.BlockSpec((1, d, 2 * f), lambda b, be, nu: (be[blk(b, be, nu)], 0, 0)),
                  pl.BlockSpec((1, 1, 2 * f), lambda b, be, nu: (be[blk(b, be, nu)], 0, 0)),
                  pl.BlockSpec((None, 1, f, d), lambda b, be, nu: (layer, be[blk(b, be, nu)], 0, 0)),
                  pl.BlockSpec((1, 1, d), lambda b, be, nu: (be[blk(b, be, nu)], 0, 0))],
        out_specs=pl.BlockSpec((EXPERT_ROWS, d // 2), lambda b, be, nu: (b, 0)),
        scratch_shapes=[pltpu.VMEM((f, d), BF16)],
    )
    return pl.pallas_call(
        _expert_kernel,
        out_shape=jax.ShapeDtypeStruct((n_rows, d // 2), U32),
        grid_spec=grid_spec,
        compiler_params=_cparams("arbitrary"),
        name="experts",
    )(block_e, n_used, xs, w1p, b1p, w2, b2.reshape(ne, 1, d))


def _combine_kernel(dcur_ref, dnxt_ref, tw_ref, x_ref, g_ref, b_ref, ys_hbm, o_ref, ob_ref, buf_ref, sem, *, alpha):
    i = pl.program_id(0)
    tokens, d = x_ref.shape
    half = d // 2
    slot = i % 2

    def copy(s, t, k, src_row):
        return pltpu.make_async_copy(ys_hbm.at[pl.ds(src_row, 1)], buf_ref.at[s, k, pl.ds(t, 1)], sem.at[s])

    def request(d_ref, s):
        def body(t, c):
            for k in range(TOP_K):
                copy(s, t, k, d_ref[0, 0, t * TOP_K + k]).start()
            return c

        lax.fori_loop(0, tokens, body, 0)

    @pl.when(i == 0)
    def _():
        request(dcur_ref, 0)

    @pl.when(i + 1 < pl.num_programs(0))
    def _():
        request(dnxt_ref, 1 - slot)

    def drain(t, c):
        for k in range(TOP_K):
            copy(slot, 0, k, 0).wait()
        return c

    lax.fori_loop(0, tokens, drain, 0)
    tw = tw_ref[...]
    x = x_ref[...]
    y_lo = alpha * x[:, :half]
    y_hi = alpha * x[:, half:]
    for k in range(TOP_K):
        u = buf_ref[slot, k]
        wk = tw[:, k:k + 1]
        y_lo = y_lo + wk * pltpu.bitcast(u << 16, F32)
        y_hi = y_hi + wk * pltpu.bitcast(u & jnp.uint32(0xFFFF0000), F32)
    out = _ln_rows(jnp.concatenate([y_lo, y_hi], axis=1), g_ref[...], b_ref[...])
    o_ref[...] = out
    ob_ref[...] = out.astype(BF16)


def _combine(ys, dest, tw, x, g, b, alpha, *, tokens=128):
    s, d = x.shape
    tokens = min(tokens, s)
    steps = s // tokens
    kern = functools.partial(_combine_kernel, alpha=alpha)
    dest3 = dest.reshape(steps, 1, tokens * TOP_K)
    row = lambda width: pl.BlockSpec((tokens, width), lambda i: (i, 0))
    const = lambda shape: pl.BlockSpec(shape, lambda i: (0,) * len(shape))
    return pl.pallas_call(
        kern,
        out_shape=(jax.ShapeDtypeStruct((s, d), F32), jax.ShapeDtypeStruct((s, d), BF16)),
        grid=(steps,),
        in_specs=[pl.BlockSpec((1, 1, tokens * TOP_K), lambda i: (i, 0, 0), memory_space=pltpu.SMEM),
                  pl.BlockSpec((1, 1, tokens * TOP_K), lambda i: (jnp.minimum(i + 1, steps - 1), 0, 0),
                               memory_space=pltpu.SMEM),
                  row(LANES), row(d), const((1, d)), const((1, d)),
                  pl.BlockSpec(memory_space=pl.ANY)],
        out_specs=(row(d), row(d)),
        scratch_shapes=[pltpu.VMEM((2, TOP_K, tokens, d // 2), U32), pltpu.SemaphoreType.DMA((2,))],
        compiler_params=_cparams("arbitrary"),
        name="combine",
    )(dest3, dest3, tw, x, g.reshape(1, d), b.reshape(1, d), ys)


def _mm_rows_kernel(starts_ref, a_ref, wt_ref, o_ref, wb_ref):
    del starts_ref

    @pl.when(pl.program_id(1) == 0)
    def _():
        wb_ref[...] = wt_ref[0].astype(BF16)

    o_ref[...] = lax.dot_general(a_ref[...], wb_ref[...], (((1,), (1,)), ((), ())),
                                 preferred_element_type=F32).astype(o_ref.dtype)


def _matmul_rows(a, wt, layer, starts, tn, *, tm, name):
    m, k = a.shape
    tm = min(tm, m)
    grid_spec = pltpu.PrefetchScalarGridSpec(
        num_scalar_prefetch=1,
        grid=(len(starts), m // tm),
        in_specs=[pl.BlockSpec((tm, k), lambda j, i, st: (i, 0)),
                  pl.BlockSpec((pl.Element(1), pl.Element(tn), pl.Element(k)),
                               lambda j, i, st: (layer, pl.multiple_of(st[j], SUBLANES), 0))],
        out_specs=pl.BlockSpec((tm, tn), lambda j, i, st: (i, j)),
        scratch_shapes=[pltpu.VMEM((tn, k), BF16)],
    )
    return pl.pallas_call(
        _mm_rows_kernel,
        out_shape=jax.ShapeDtypeStruct((m, len(starts) * tn), F32),
        grid_spec=grid_spec,
        compiler_params=_cparams("parallel", "arbitrary"),
        name=name,
    )(jnp.asarray(starts, I32), a, wt)


def _in_proj(xb, w_in, layer, sizes, inner):
    z, xbc, dt, u, v, gate_c, x_c, gates = sizes
    src = {"z": 0, "xs": z, "bc": z + inner, "dt": z + xbc}
    src["u"] = src["dt"] + dt
    src["v"] = src["u"] + u
    src["gate_c"] = src["v"] + v
    src["x_c"] = src["gate_c"] + gate_c
    src["gates"] = src["x_c"] + x_c
    width = {"z": z, "xs": inner, "u": u, "v": v, "gate_c": gate_c, "x_c": x_c, "bc": xbc - inner, "gates": gates}
    tn = max(t for t in (1024, 512, 256, 128) if all(w % t == 0 for w in width.values()))
    assert dt <= LANES and all(s % SUBLANES == 0 for s in src.values())
    lay, starts, off = {}, [], 0
    for name in ("z", "xs", "u", "v", "gate_c", "x_c", "bc", "gates"):
        lay[name] = off
        starts += [src[name] + t for t in range(0, width[name], tn)]
        off += width[name]
    wt = jnp.swapaxes(w_in, 1, 2)
    proj = _matmul_rows(xb, wt, layer, starts, tn, tm=512, name="in_proj")
    dt_raw = _matmul_rows(xb, wt, layer, [src["dt"]], LANES, tm=1024, name="dt_proj")
    return proj, dt_raw, lay


def _moe_plan(te, rk, cnt, ne, n_blocks):
    counts = cnt[0, :ne]
    padded = (counts + EXPERT_ROWS - 1) // EXPERT_ROWS * EXPERT_ROWS
    pad_end = jnp.cumsum(padded)
    pad_start = pad_end - padded
    picked = te[:, :TOP_K, None] == jnp.arange(ne, dtype=I32)[None, None, :]
    dest = (jnp.sum(jnp.where(picked, pad_start[None, None, :], 0), axis=-1) + rk[:, :TOP_K]).astype(I32)
    dest = dest.reshape(-1)
    block_start = jnp.arange(n_blocks, dtype=I32) * EXPERT_ROWS
    block_e = jnp.sum(pad_end[None, :] <= block_start[:, None], axis=1)
    block_e = jnp.minimum(block_e, ne - 1).astype(I32)
    n_used = (pad_end[-1] // EXPERT_ROWS).astype(I32).reshape(1)
    return dest, block_e, n_used, pad_end.astype(I32), counts.astype(I32)


def kernel(x, ln_emb_g, ln_emb_b, w_in, b_merge, conv_a_w, conv_a_b, dt_bias, a_log, d_skip, norm_a_g,
           ln_v_g, ln_v_b, w_spatial, b_spatial, conv_c_w, conv_c_b, w_rg_a, b_rg_a, w_rg_x, b_rg_x, lam,
           p_a, p_b, p_c, w_o, ln_mix_g, ln_mix_b, w_router, b_router, w1, b1, w2, b2, ln_ffn_g, ln_ffn_b):
    bsz, seq, d = x.shape
    depth = w_in.shape[0]
    alpha = (2 * depth) ** 0.25
    heads = dt_bias.shape[1]
    inner = norm_a_g.shape[1]
    sizes = (inner, conv_a_w.shape[2], heads, ln_v_g.shape[1], ln_v_g.shape[1], lam.shape[1], lam.shape[1],
             3 * d)
    ne = w_router.shape[2]
    s = bsz * seq
    n_rows = s * TOP_K + ne * EXPERT_ROWS
    n_blocks = n_rows // EXPERT_ROWS

    xf, xb = _ln0(x.reshape(s, d), ln_emb_g, ln_emb_b)
    for l in range(depth):
        proj, dt_raw, lay = _in_proj(xb, w_in, l, sizes, inner)
        y_a = _ssd(proj, dt_raw, lay, conv_a_w[l], conv_a_b[l], dt_bias[l], a_log[l], d_skip[l], norm_a_g[l])
        y_b = _gmlp(proj, lay, ln_v_g[l], ln_v_b[l], w_spatial[l], b_spatial[l])
        y_c = _lru(proj, lay, conv_c_w[l], conv_c_b[l], w_rg_a[l], b_rg_a[l], w_rg_x[l], b_rg_x[l], lam[l])
        merged = _merge(y_a, y_b, y_c, p_a[l].astype(BF16), p_b[l].astype(BF16), p_c[l].astype(BF16),
                        proj, lay, b_merge[l])
        h = _matmul(merged, w_o, l, F32, tm=1024, tn=512, res=xf, alpha=alpha, name="w_o")
        xm, xp, te, tw, rk, cnt = _ln_router(h, ln_mix_g[l], ln_mix_b[l], w_router[l], b_router[l])
        dest, block_e, n_used, pad_end, counts = _moe_plan(te, rk, cnt, ne, n_blocks)
        xs = _dispatch(xp, dest, pad_end, counts, n_rows)
        ys = _experts(xs, block_e, n_used, _w1_prep(w1, l), _regroup_bias(b1[l]), w2, l, b2[l])
        xf, xb = _combine(ys, dest, tw, xm, ln_ffn_g[l], ln_ffn_b[l], alpha)
    return xf.reshape(bsz, seq, d)
```

```python
import functools
import math

import jax
import jax.numpy as jnp
from jax import lax
from jax.experimental import pallas as pl
from jax.experimental.pallas import tpu as pltpu

F32 = jnp.float32
BF16 = jnp.bfloat16
U32 = jnp.uint32
I32 = jnp.int32

SSD_GROUPS = 4
SSD_STATE = 128
CHUNK = 128
GMLP_GROUPS = 8
LRU_C = 8.0
TOP_K = 4
SWIGLU_LIMIT = 7.0
SWIGLU_ALPHA = 1.702
LN_EPS = 1e-5
RMS_EPS = 1e-5

LANES = 128
SUBLANES = 8
VMEM_LIMIT = 56 * 1024 * 1024
EXPERT_ROWS = 256
WINDOW_BLOCK = 1024
NEG_BIG = -1e30


def _cparams(*sem):
    return pltpu.CompilerParams(dimension_semantics=sem, vmem_limit_bytes=VMEM_LIMIT)


def _dot(a, b):
    return jnp.dot(a, b, preferred_element_type=F32)


def _split3(x):
    hi = x.astype(BF16)
    r = x - hi.astype(F32)
    mid = r.astype(BF16)
    lo = (r - mid.astype(F32)).astype(BF16)
    return hi, mid, lo


def _exact_right(x, m):
    hi, mid, lo = _split3(x)
    return _dot(hi, m) + (_dot(mid, m) + _dot(lo, m))


def _exact_left(m, x):
    hi, mid, lo = _split3(x)
    return _dot(m, hi) + (_dot(m, mid) + _dot(m, lo))


def _sigmoid(x):
    return 1.0 / (1.0 + jnp.exp(-x))


def _silu(x):
    return x * _sigmoid(x)


def _softplus(x):
    return jnp.maximum(x, 0.0) + jnp.log1p(jnp.exp(-jnp.abs(x)))


def _gelu(x):
    c = math.sqrt(2.0 / math.pi)
    return x * (0.5 * (1.0 + jnp.tanh(c * (x + 0.044715 * (x * x * x)))))


def _ln_rows(x, g, b):
    mu = jnp.mean(x, axis=-1, keepdims=True)
    xc = x - mu
    var = jnp.mean(xc * xc, axis=-1, keepdims=True)
    return xc * lax.rsqrt(var + LN_EPS) * g + b


def _causal_conv(x, carry, w, b):
    kw = w.shape[0]
    rows = x.shape[0]
    xp = jnp.concatenate([carry, x], axis=0)
    y = x * w[kw - 1:kw, :] + b
    for k in range(kw - 1):
        s = SUBLANES - (kw - 1) + k
        y = y + xp[s:s + rows, :] * w[k:k + 1, :]
    return y


def _pack_halves(y):
    half = y.shape[1] // 2
    lo = pltpu.bitcast(y[:, :half].astype(BF16).astype(F32), U32)
    hi = pltpu.bitcast(y[:, half:].astype(BF16).astype(F32), U32)
    return (hi & jnp.uint32(0xFFFF0000)) | (lo >> 16)


def _unpack_halves(u):
    lo = pltpu.bitcast(u << 16, F32).astype(BF16)
    hi = pltpu.bitcast(u & jnp.uint32(0xFFFF0000), F32).astype(BF16)
    return lo, hi


def _window_specs(off, width, rows, blk):
    first, shift = divmod(off, blk)
    nblk = pl.cdiv(shift + width, blk)
    specs = [pl.BlockSpec((rows, blk), lambda i, q=q: (i, first + q)) for q in range(nblk)]
    return specs, shift


def _window_load(refs, shift, width, rows=slice(None)):
    blk = refs[0].shape[1]
    first, lane_shift = divmod(shift, LANES)
    n_out = width // LANES

    def lane_block(k):
        q, o = divmod(k * LANES, blk)
        return refs[q][rows, o:o + LANES].astype(F32)

    if lane_shift == 0:
        return jnp.concatenate([lane_block(first + k) for k in range(n_out)], axis=1)
    rolled = [pltpu.roll(lane_block(first + k), LANES - lane_shift, 1) for k in range(n_out + 1)]
    lane = lax.broadcasted_iota(I32, rolled[0].shape, 1)
    outs = [jnp.where(lane < LANES - lane_shift, rolled[k], rolled[k + 1]) for k in range(n_out)]
    return jnp.concatenate(outs, axis=1)


def _ln0_kernel(x_ref, g_ref, b_ref, o_ref, ob_ref):
    y = _ln_rows(x_ref[...], g_ref[...], b_ref[...])
    o_ref[...] = y
    ob_ref[...] = y.astype(BF16)


def _ln0(x, g, b):
    s, d = x.shape
    tm = min(256, s)
    return pl.pallas_call(
        _ln0_kernel,
        out_shape=(jax.ShapeDtypeStruct((s, d), F32), jax.ShapeDtypeStruct((s, d), BF16)),
        grid=(s // tm,),
        in_specs=[pl.BlockSpec((tm, d), lambda i: (i, 0)),
                  pl.BlockSpec((1, d), lambda i: (0, 0)),
                  pl.BlockSpec((1, d), lambda i: (0, 0))],
        out_specs=(pl.BlockSpec((tm, d), lambda i: (i, 0)),
                   pl.BlockSpec((tm, d), lambda i: (i, 0))),
        compiler_params=_cparams("parallel"),
        name="ln0",
    )(x, g.reshape(1, d), b.reshape(1, d))


def _cast_weight_tile(w_ref, wb_ref, n_valid):
    tn = w_ref.shape[1]
    col = pl.program_id(0) * tn + lax.broadcasted_iota(I32, w_ref.shape, 1)
    wb_ref[...] = jnp.where(col < n_valid, w_ref[...], 0.0).astype(BF16)


def _mm_kernel(a_ref, w_ref, o_ref, wb_ref, *, n_valid):
    @pl.when(pl.program_id(1) == 0)
    def _():
        _cast_weight_tile(w_ref, wb_ref, n_valid)

    o_ref[...] = _dot(a_ref[...], wb_ref[...]).astype(o_ref.dtype)


def _mm_res_kernel(a_ref, w_ref, r_ref, o_ref, wb_ref, *, n_valid, alpha):
    @pl.when(pl.program_id(1) == 0)
    def _():
        _cast_weight_tile(w_ref, wb_ref, n_valid)

    o_ref[...] = (alpha * r_ref[...] + _dot(a_ref[...], wb_ref[...])).astype(o_ref.dtype)


def _matmul(a, w, layer, out_dtype, *, tm, tn, res=None, alpha=1.0, name="mm"):
    m, k = a.shape
    n = w.shape[2]
    tm = min(tm, m)
    tn = min(tn, n)
    n_tiles = pl.cdiv(n, tn)
    in_specs = [pl.BlockSpec((tm, k), lambda j, i: (i, 0)),
                pl.BlockSpec((None, k, tn), lambda j, i: (layer, 0, j))]
    args = [a, w]
    if res is None:
        kern = functools.partial(_mm_kernel, n_valid=n)
    else:
        kern = functools.partial(_mm_res_kernel, n_valid=n, alpha=alpha)
        in_specs.append(pl.BlockSpec((tm, tn), lambda j, i: (i, j)))
        args.append(res)
    return pl.pallas_call(
        kern,
        out_shape=jax.ShapeDtypeStruct((m, n_tiles * tn), out_dtype),
        grid=(n_tiles, m // tm),
        in_specs=in_specs,
        out_specs=pl.BlockSpec((tm, tn), lambda j, i: (i, j)),
        scratch_shapes=[pltpu.VMEM((k, tn), BF16)],
        compiler_params=_cparams("parallel", "arbitrary"),
        name=name,
    )(*args)


def _ssd_kernel(z_ref, xs_ref, bc_ref, dt_ref, cwx_ref, cbx_ref, cwb_ref, cbb_ref, dtb_ref, alog_ref,
                dskip_ref, ng_ref, e_ref, o_ref, cx_ref, cb_ref, st_ref, *, heads, head_dim):
    i = pl.program_id(0)

    @pl.when(i == 0)
    def _():
        cx_ref[...] = jnp.zeros_like(cx_ref)
        cb_ref[...] = jnp.zeros_like(cb_ref)
        st_ref[...] = jnp.zeros_like(st_ref)

    n = SSD_STATE
    gw = (heads // SSD_GROUPS) * head_dim
    hpg = heads // SSD_GROUPS
    pair = LANES // head_dim

    xs_raw = xs_ref[...].astype(F32)
    bc_raw = bc_ref[...].astype(F32)
    xs = _silu(_causal_conv(xs_raw, cx_ref[...], cwx_ref[...], cbx_ref[...]))
    bc = _silu(_causal_conv(bc_raw, cb_ref[...], cwb_ref[...], cbb_ref[...]))
    cx_ref[...] = xs_raw[CHUNK - SUBLANES:, :]
    cb_ref[...] = bc_raw[CHUNK - SUBLANES:, :]

    row = lax.broadcasted_iota(I32, (CHUNK, CHUNK), 0)
    col = lax.broadcasted_iota(I32, (CHUNK, CHUNK), 1)
    dt = _softplus(jnp.where(col < heads, dt_ref[...], 0.0) + dtb_ref[...])
    da = dt * (-jnp.exp(alog_ref[...]))
    causal = row >= col
    ltri = jnp.where(causal, 1.0, 0.0).astype(BF16)
    cum = _exact_left(ltri, da)
    cum_t = cum.T
    expand = e_ref[...]
    cum_e = _exact_right(cum, expand)
    dt_e = _exact_right(dt, expand)
    last_e = cum_e[CHUNK - 1:CHUNK, :]
    decay_out = jnp.exp(cum_e)
    decay_st = jnp.exp(last_e - cum_e)
    chunk_decay = jnp.exp(last_e)
    xdt = xs * dt_e
    xdt_b = xdt.astype(BF16)
    xst_b = (xdt * decay_st).astype(BF16)
    lane = lax.broadcasted_iota(I32, (CHUNK, LANES), 1)

    for g in range(SSD_GROUPS):
        bg = bc[:, g * n:(g + 1) * n].astype(BF16)
        cg = bc[:, (SSD_GROUPS + g) * n:(SSD_GROUPS + g + 1) * n].astype(BF16)
        cb = lax.dot_general(cg, bg, (((1,), (1,)), ((), ())), preferred_element_type=F32)
        slabs = []
        for j in range(hpg // pair):
            h0 = g * hpg + j * pair
            sc = []
            for h in range(h0, h0 + pair):
                seg = cum[:, h:h + 1] - cum_t[h:h + 1, :]
                dec = jnp.exp(jnp.where(causal, seg, -jnp.inf))
                sc.append((cb * dec).astype(BF16))
            xp = xdt_b[:, h0 * head_dim:h0 * head_dim + LANES]
            blocks = [jnp.where((lane >= q * head_dim) & (lane < (q + 1) * head_dim), xp, jnp.zeros_like(xp))
                      for q in range(pair)]
            slabs.append(_dot(jnp.concatenate(sc, axis=1), jnp.concatenate(blocks, axis=0)))
        y_diag = jnp.concatenate(slabs, axis=1)
        cs = slice(g * gw, (g + 1) * gw)
        st = st_ref[g]
        y_off = _dot(cg, st.astype(BF16)) * decay_out[:, cs]
        st_new = lax.dot_general(bg, xst_b[:, cs], (((0,), (0,)), ((), ())), preferred_element_type=F32)
        st_ref[g] = st * chunk_decay[:, cs] + st_new
        y = y_diag + y_off + xs[:, cs] * dskip_ref[:, cs]
        yg = y * _silu(z_ref[:, cs].astype(F32))
        ms = jnp.mean(yg * yg, axis=-1, keepdims=True)
        o_ref[:, cs] = (yg * lax.rsqrt(ms + RMS_EPS) * ng_ref[:, cs]).astype(o_ref.dtype)


def _ssd(proj, dt_raw, lay, conv_w, conv_b, dt_bias, a_log, d_skip, norm_g):
    s = proj.shape[0]
    heads = dt_bias.shape[0]
    inner = norm_g.shape[0]
    head_dim = inner // heads
    bcw = 2 * SSD_GROUPS * SSD_STATE
    gw = inner // SSD_GROUPS
    pad = LANES - heads
    assert lay["z"] % inner == 0 and lay["xs"] % inner == 0 and lay["bc"] % bcw == 0
    expand = (jnp.arange(LANES)[:, None] == (jnp.arange(inner) // head_dim)[None, :]).astype(BF16)
    kern = functools.partial(_ssd_kernel, heads=heads, head_dim=head_dim)
    const = lambda shape: pl.BlockSpec(shape, lambda i: (0,) * len(shape))
    return pl.pallas_call(
        kern,
        out_shape=jax.ShapeDtypeStruct((s, inner), BF16),
        grid=(s // CHUNK,),
        in_specs=[pl.BlockSpec((CHUNK, inner), lambda i: (i, lay["z"] // inner)),
                  pl.BlockSpec((CHUNK, inner), lambda i: (i, lay["xs"] // inner)),
                  pl.BlockSpec((CHUNK, bcw), lambda i: (i, lay["bc"] // bcw)),
                  pl.BlockSpec((CHUNK, LANES), lambda i: (i, 0)),
                  const((conv_w.shape[0], inner)), const((1, inner)),
                  const((conv_w.shape[0], bcw)), const((1, bcw)),
                  const((1, LANES)), const((1, LANES)),
                  const((1, inner)), const((1, inner)), const((LANES, inner))],
        out_specs=pl.BlockSpec((CHUNK, inner), lambda i: (i, 0)),
        scratch_shapes=[pltpu.VMEM((SUBLANES, inner), F32), pltpu.VMEM((SUBLANES, bcw), F32),
                        pltpu.VMEM((SSD_GROUPS, SSD_STATE, gw), F32)],
        compiler_params=_cparams("arbitrary"),
        name="ssd",
    )(proj, proj, proj, dt_raw,
      conv_w[:, :inner], conv_b[:inner].reshape(1, inner),
      conv_w[:, inner:], conv_b[inner:].reshape(1, bcw),
      jnp.pad(dt_bias, (0, pad)).reshape(1, LANES), jnp.pad(a_log, (0, pad)).reshape(1, LANES),
      jnp.repeat(d_skip, head_dim).reshape(1, inner), norm_g.reshape(1, inner), expand)


def _gmlp_kernel(*refs, chunks, nu, nv, su, sv):
    u_refs, v_refs = refs[:nu], refs[nu:nu + nv]
    lg_ref, lb_ref, ws_ref, be_ref, o_ref = refs[nu + nv:]
    width = o_ref.shape[1]
    gw = width // GMLP_GROUPS
    row = lax.broadcasted_iota(I32, (CHUNK, CHUNK), 0)
    col = lax.broadcasted_iota(I32, (CHUNK, CHUNK), 1)
    tril = row >= col
    ws = [jnp.where(tril, ws_ref[g], 0.0).astype(BF16) for g in range(GMLP_GROUPS)]
    for c in range(chunks):
        rs = slice(c * CHUNK, (c + 1) * CHUNK)
        v = _ln_rows(_gelu(_window_load(v_refs, sv, width, rs)), lg_ref[...], lb_ref[...]).astype(BF16)
        u = _gelu(_window_load(u_refs, su, width, rs))
        for g in range(GMLP_GROUPS):
            cs = slice(g * gw, (g + 1) * gw)
            mixed = _dot(ws[g], v[:, cs]) + be_ref[:, cs]
            o_ref[rs, cs] = (u[:, cs] * mixed).astype(o_ref.dtype)


def _gmlp(proj, lay, ln_g, ln_b, w_s, b_s, *, rows=256):
    s = proj.shape[0]
    width = ln_g.shape[0]
    gw = width // GMLP_GROUPS
    rows = min(rows, s)
    bias_e = jnp.repeat(b_s.T, gw, axis=1)
    u_specs, su = _window_specs(lay["u"], width, rows, WINDOW_BLOCK)
    v_specs, sv = _window_specs(lay["v"], width, rows, WINDOW_BLOCK)
    kern = functools.partial(_gmlp_kernel, chunks=rows // CHUNK, nu=len(u_specs), nv=len(v_specs), su=su, sv=sv)
    const = lambda shape: pl.BlockSpec(shape, lambda i: (0,) * len(shape))
    return pl.pallas_call(
        kern,
        out_shape=jax.ShapeDtypeStruct((s, width), BF16),
        grid=(s // rows,),
        in_specs=u_specs + v_specs + [const((1, width)), const((1, width)),
                                      const((GMLP_GROUPS, CHUNK, CHUNK)), const((CHUNK, width))],
        out_specs=pl.BlockSpec((rows, width), lambda i: (i, 0)),
        compiler_params=_cparams("parallel"),
        name="gmlp",
    )(*([proj] * (len(u_specs) + len(v_specs))), ln_g.reshape(1, width), ln_b.reshape(1, width), w_s, bias_e)


def _lru_kernel(*refs, ng, nx, sg, sx):
    gate_refs, x_refs = refs[:ng], refs[ng:ng + nx]
    cw_ref, cb_ref, wa_ref, ba_ref, wx_ref, bx_ref, lam_ref, o_ref, carry_ref, h_ref = refs[ng + nx:]
    i = pl.program_id(0)

    @pl.when(i == 0)
    def _():
        carry_ref[...] = jnp.zeros_like(carry_ref)
        h_ref[...] = jnp.zeros_like(h_ref)

    rows, width = o_ref.shape
    nblk, blk = wa_ref.shape[0], wa_ref.shape[1]
    x_raw = _window_load(x_refs, sx, width)
    xr = _causal_conv(x_raw, carry_ref[...], cw_ref[...], cb_ref[...])
    carry_ref[...] = x_raw[rows - SUBLANES:, :]
    xb = xr.astype(BF16)
    ra = jnp.concatenate([_dot(xb[:, q * blk:(q + 1) * blk], wa_ref[q]) for q in range(nblk)], axis=1)
    ia = jnp.concatenate([_dot(xb[:, q * blk:(q + 1) * blk], wx_ref[q]) for q in range(nblk)], axis=1)
    r = _sigmoid(ra + ba_ref[...])
    ig = _sigmoid(ia + bx_ref[...])
    log_a = (-LRU_C) * r * _softplus(-lam_ref[...])
    a = jnp.exp(log_a)
    b = jnp.sqrt((1.0 - a) * (1.0 + a)) * (ig * xr)
    ridx = lax.broadcasted_iota(I32, (SUBLANES, width), 0)
    h_in = h_ref[...]
    groups = []
    for g in range(rows // SUBLANES):
        gs = slice(g * SUBLANES, (g + 1) * SUBLANES)
        ag, bg = a[gs, :], b[gs, :]
        d = 1
        while d < SUBLANES:
            keep = ridx >= d
            bg = jnp.where(keep, ag * pltpu.roll(bg, d, 0) + bg, bg)
            ag = jnp.where(keep, ag * pltpu.roll(ag, d, 0), ag)
            d *= 2
        hg = ag * h_in + bg
        h_in = hg[SUBLANES - 1:SUBLANES, :]
        groups.append(hg)
    h = jnp.concatenate(groups, axis=0)
    h_ref[...] = h_in
    o_ref[...] = (h * _gelu(_window_load(gate_refs, sg, width))).astype(o_ref.dtype)


def _lru(proj, lay, conv_w, conv_b, w_a, b_a, w_x, b_x, lam, *, rows=128):
    s = proj.shape[0]
    width = lam.shape[0]
    nblk, blk = w_a.shape[0], w_a.shape[1]
    rows = min(rows, s)
    g_specs, sg = _window_specs(lay["gate_c"], width, rows, WINDOW_BLOCK)
    x_specs, sx = _window_specs(lay["x_c"], width, rows, WINDOW_BLOCK)
    kern = functools.partial(_lru_kernel, ng=len(g_specs), nx=len(x_specs), sg=sg, sx=sx)
    const = lambda shape: pl.BlockSpec(shape, lambda i: (0,) * len(shape))
    return pl.pallas_call(
        kern,
        out_shape=jax.ShapeDtypeStruct((s, width), BF16),
        grid=(s // rows,),
        in_specs=g_specs + x_specs + [const((conv_w.shape[0], width)), const((1, width)),
                                      const((nblk, blk, blk)), const((1, width)),
                                      const((nblk, blk, blk)), const((1, width)), const((1, width))],
        out_specs=pl.BlockSpec((rows, width), lambda i: (i, 0)),
        scratch_shapes=[pltpu.VMEM((SUBLANES, width), F32), pltpu.VMEM((1, width), F32)],
        compiler_params=_cparams("arbitrary"),
        name="rglru",
    )(*([proj] * (len(g_specs) + len(x_specs))), conv_w, conv_b.reshape(1, width), w_a.astype(BF16),
      b_a.reshape(1, width), w_x.astype(BF16), b_x.reshape(1, width), lam.reshape(1, width))


def _merge_kernel(*refs, shift, ng):
    y_refs, p_refs = refs[:3], refs[3:6]
    gate_refs = [refs[6 + br * ng:6 + (br + 1) * ng] for br in range(3)]
    b_refs = refs[6 + 3 * ng:9 + 3 * ng]
    o_ref = refs[9 + 3 * ng]
    tn = o_ref.shape[1]
    acc = None
    for br in range(3):
        gate = _sigmoid(_window_load(gate_refs[br], shift, tn) + b_refs[br][...])
        term = gate * _dot(y_refs[br][...], p_refs[br][...])
        acc = term if acc is None else acc + term
    o_ref[...] = acc.astype(o_ref.dtype)


def _merge(y_a, y_b, y_c, p_a, p_b, p_c, proj, lay, b_merge, *, tm=256, tn=1024):
    s, w = y_a.shape
    d = p_a.shape[1]
    tm = min(tm, s)
    tn = min(tn, d)
    g0, shift = divmod(lay["gates"], tn)
    assert shift < LANES
    ng = 2 if shift else 1
    nd = d // tn
    bm = b_merge.reshape(1, 3 * d)
    y_spec = pl.BlockSpec((tm, w), lambda j, i: (i, 0))
    p_spec = pl.BlockSpec((w, tn), lambda j, i: (0, j))
    g_specs = []
    for br in range(3):
        g_specs.append(pl.BlockSpec((tm, tn), lambda j, i, br=br: (i, g0 + br * nd + j)))
        if shift:
            g_specs.append(pl.BlockSpec((tm, LANES), lambda j, i, br=br: (i, (g0 + br * nd + j + 1) * (tn // LANES))))
    b_specs = [pl.BlockSpec((1, tn), lambda j, i, br=br: (0, br * nd + j)) for br in range(3)]
    kern = functools.partial(_merge_kernel, shift=shift, ng=ng)
    return pl.pallas_call(
        kern,
        out_shape=jax.ShapeDtypeStruct((s, d), BF16),
        grid=(d // tn, s // tm),
        in_specs=[y_spec, y_spec, y_spec, p_spec, p_spec, p_spec] + g_specs + b_specs,
        out_specs=pl.BlockSpec((tm, tn), lambda j, i: (i, j)),
        compiler_params=_cparams("parallel", "parallel"),
        name="merge",
    )(y_a, y_b, y_c, p_a, p_b, p_c, *([proj] * (3 * ng)), bm, bm, bm)


def _ln_router_kernel(h_ref, g_ref, b_ref, wr_ref, br_ref, x_ref, xp_ref, te_ref, tw_ref, rk_ref, cnt_ref,
                      carry_ref):
    i = pl.program_id(0)

    @pl.when(i == 0)
    def _():
        carry_ref[...] = jnp.zeros_like(carry_ref)

    rows = h_ref.shape[0]
    y = _ln_rows(h_ref[...], g_ref[...], b_ref[...])
    x_ref[...] = y
    xp_ref[...] = _pack_halves(y)
    yh = y.astype(BF16)
    yl = (y - yh.astype(F32)).astype(BF16)
    wr = wr_ref[...]
    wh = wr.astype(BF16)
    wl = (wr - wh.astype(F32)).astype(BF16)
    logits = _dot(yh, wh) + (_dot(yh, wl) + _dot(yl, wh)) + br_ref[...]
    lane = lax.broadcasted_iota(I32, (rows, LANES), 1)
    lane_f = lane.astype(F32)
    cur = logits
    vals, sels, idxs = [], [], []
    for _ in range(TOP_K):
        m = jnp.max(cur, axis=-1, keepdims=True)
        idx = jnp.min(jnp.where(cur == m, lane_f, float(LANES)), axis=-1, keepdims=True)
        sel = lane_f == idx
        vals.append(m)
        sels.append(sel)
        idxs.append(idx)
        cur = jnp.where(sel, -jnp.inf, cur)
    ex = [jnp.exp(v - vals[0]) for v in vals]
    den = ex[0]
    for e in ex[1:]:
        den = den + e
    multi = jnp.zeros((rows, LANES), F32)
    for sel in sels:
        multi = multi + jnp.where(sel, 1.0, 0.0)
    r2 = lax.broadcasted_iota(I32, (rows, rows), 0)
    c2 = lax.broadcasted_iota(I32, (rows, rows), 1)
    lower = jnp.where(r2 > c2, 1.0, 0.0).astype(BF16)
    before = _dot(lower, multi.astype(BF16)) + carry_ref[...]
    te = jnp.zeros((rows, LANES), F32)
    tw = jnp.zeros((rows, LANES), F32)
    rk = jnp.zeros((rows, LANES), F32)
    for k in range(TOP_K):
        rank_k = jnp.sum(jnp.where(sels[k], before, 0.0), axis=-1, keepdims=True)
        te = jnp.where(lane == k, idxs[k], te)
        tw = jnp.where(lane == k, ex[k] / den, tw)
        rk = jnp.where(lane == k, rank_k, rk)
    te_ref[...] = te.astype(I32)
    tw_ref[...] = tw
    rk_ref[...] = rk.astype(I32)
    carry_ref[...] = carry_ref[...] + jnp.sum(multi, axis=0, keepdims=True)
    cnt_ref[...] = carry_ref[...].astype(I32)


def _ln_router(h, g, b, w_router, b_router, *, tm=256):
    s, d = h.shape
    ne = w_router.shape[1]
    tm = min(tm, s)
    wr = jnp.pad(w_router, ((0, 0), (0, LANES - ne)))
    br = jnp.pad(b_router, (0, LANES - ne), constant_values=NEG_BIG).reshape(1, LANES)
    row = lambda width: pl.BlockSpec((tm, width), lambda i: (i, 0))
    const = lambda shape: pl.BlockSpec(shape, lambda i: (0,) * len(shape))
    return pl.pallas_call(
        _ln_router_kernel,
        out_shape=(jax.ShapeDtypeStruct((s, d), F32), jax.ShapeDtypeStruct((s, d // 2), U32),
                   jax.ShapeDtypeStruct((s, LANES), I32), jax.ShapeDtypeStruct((s, LANES), F32),
                   jax.ShapeDtypeStruct((s, LANES), I32), jax.ShapeDtypeStruct((1, LANES), I32)),
        grid=(s // tm,),
        in_specs=[row(d), const((1, d)), const((1, d)), const((d, LANES)), const((1, LANES))],
        out_specs=(row(d), row(d // 2), row(LANES), row(LANES), row(LANES), const((1, LANES))),
        scratch_shapes=[pltpu.VMEM((1, LANES), F32)],
        compiler_params=_cparams("arbitrary"),
        name="ln_router",
    )(h, g.reshape(1, d), b.reshape(1, d), wr, br)


def _row_copy(src_hbm, dst_hbm, sem, src_row, dst_row):
    return pltpu.make_async_copy(src_hbm.at[pl.ds(src_row, 1)], dst_hbm.at[pl.ds(dst_row, 1)], sem)


def _dispatch_kernel(dest_ref, end_ref, cnt_ref, x_ref, o_hbm, zero_ref, sem, zsem, *, tokens):
    @pl.when(pl.program_id(0) == 0)
    def _():
        zero_ref[...] = jnp.zeros_like(zero_ref)

        def fill(e):
            start = pl.multiple_of(end_ref[e] - EXPERT_ROWS, EXPERT_ROWS)
            return pltpu.make_async_copy(zero_ref, o_hbm.at[pl.ds(start, EXPERT_ROWS)], zsem)

        for e in range(end_ref.shape[0]):
            @pl.when(cnt_ref[e] > 0)
            def _():
                fill(e).start()

        for e in range(end_ref.shape[0]):
            @pl.when(cnt_ref[e] > 0)
            def _():
                fill(e).wait()

        def tail(b):
            return pltpu.make_async_copy(
                zero_ref, o_hbm.at[pl.ds(pl.multiple_of(b * EXPERT_ROWS, EXPERT_ROWS), EXPERT_ROWS)], zsem)

        n_used = end_ref[end_ref.shape[0] - 1] // EXPERT_ROWS
        n_blocks = o_hbm.shape[0] // EXPERT_ROWS
        lax.fori_loop(n_used, n_blocks, lambda b, c: (tail(b).start(), c)[1], 0)
        lax.fori_loop(n_used, n_blocks, lambda b, c: (tail(b).wait(), c)[1], 0)

    def issue(t, c):
        for k in range(TOP_K):
            _row_copy(x_ref, o_hbm, sem, t, dest_ref[0, 0, t * TOP_K + k]).start()
        return c

    lax.fori_loop(0, tokens, issue, 0)

    for _ in range(TOP_K):
        pltpu.make_async_copy(x_ref, o_hbm.at[pl.ds(0, tokens)], sem).wait()


def _dispatch(xp, dest, pad_end, counts, n_rows, *, tokens=256):
    s, w = xp.shape
    tokens = min(tokens, s)
    steps = s // tokens
    kern = functools.partial(_dispatch_kernel, tokens=tokens)
    return pl.pallas_call(
        kern,
        out_shape=jax.ShapeDtypeStruct((n_rows, w), xp.dtype),
        grid=(steps,),
        in_specs=[pl.BlockSpec((1, 1, tokens * TOP_K), lambda i: (i, 0, 0), memory_space=pltpu.SMEM),
                  pl.BlockSpec(memory_space=pltpu.SMEM),
                  pl.BlockSpec(memory_space=pltpu.SMEM),
                  pl.BlockSpec((tokens, w), lambda i: (i, 0))],
        out_specs=pl.BlockSpec(memory_space=pl.ANY),
        scratch_shapes=[pltpu.VMEM((EXPERT_ROWS, w), xp.dtype), pltpu.SemaphoreType.DMA,
                        pltpu.SemaphoreType.DMA],
        compiler_params=_cparams("arbitrary"),
        name="dispatch",
    )(dest.reshape(steps, 1, tokens * TOP_K), pad_end, counts, xp)


def _w1_prep_kernel(w_ref, o_ref):
    c = lax.broadcasted_iota(I32, (2 * LANES, 2 * LANES), 0)
    j = lax.broadcasted_iota(I32, (2 * LANES, 2 * LANES), 1)
    src = jnp.where(j < LANES, 2 * j, 2 * (j - LANES) + 1)
    perm = jnp.where(c == src, 1.0, 0.0).astype(BF16)
    o_ref[0] = _dot(w_ref[0].astype(BF16), perm).astype(BF16)


def _regroup_bias(b1):
    ne, f2 = b1.shape
    return b1.reshape(ne, f2 // (2 * LANES), LANES, 2).transpose(0, 1, 3, 2).reshape(ne, 1, f2)


def _expert_kernel(be_ref, nu_ref, x_ref, w1_ref, b1_ref, w2_ref, b2_ref, o_ref, w2b_ref):
    b = pl.program_id(0)
    used = b < nu_ref[0]
    new_expert = jnp.logical_or(b == 0, be_ref[b] != be_ref[jnp.maximum(b - 1, 0)])

    @pl.when(jnp.logical_and(used, new_expert))
    def _():
        w2b_ref[...] = w2_ref[0].astype(BF16)

    @pl.when(used)
    def _():
        lo, hi = _unpack_halves(x_ref[...])
        half = lo.shape[1]
        gu = _dot(lo, w1_ref[0, :half, :]) + _dot(hi, w1_ref[0, half:, :]) + b1_ref[0]
        acts = []
        for j in range(gu.shape[1] // (2 * LANES)):
            glu = jnp.minimum(gu[:, 2 * j * LANES:(2 * j + 1) * LANES], SWIGLU_LIMIT)
            lin = jnp.clip(gu[:, (2 * j + 1) * LANES:(2 * j + 2) * LANES], -SWIGLU_LIMIT, SWIGLU_LIMIT)
            acts.append(glu * _sigmoid(SWIGLU_ALPHA * glu) * (lin + 1.0))
        act = jnp.concatenate(acts, axis=1)
        o_ref[...] = _pack_halves(_dot(act.astype(BF16), w2b_ref[...]) + b2_ref[0])

    @pl.when(jnp.logical_not(used))
    def _():
        o_ref[...] = jnp.zeros_like(o_ref)


def _experts(xs, block_e, n_used, w1p, b1p, w2, layer, b2):
    n_rows, half = xs.shape
    _, ne, f, d = w2.shape
    n_blocks = n_rows // EXPERT_ROWS

    def blk(b, be, nu):
        return jnp.minimum(b, nu[0] - 1)

    grid_spec = pltpu.PrefetchScalarGridSpec(
        num_scalar_prefetch=2,
        grid=(n_blocks,),
        in_specs=[pl.BlockSpec((EXPERT_ROWS, half), lambda b, be, nu: (blk(b, be, nu), 0)),
                  pl.BlockSpec((1, d, 2 * f), lambda b, be, nu: (be[blk(b, be, nu)], 0, 0)),
                  pl.BlockSpec((1, 1, 2 * f), lambda b, be, nu: (be[blk(b, be, nu)], 0, 0)),
                  pl.BlockSpec((None, 1, f, d), lambda b, be, nu: (layer, be[blk(b, be, nu)], 0, 0)),
                  pl.BlockSpec((1, 1, d), lambda b, be, nu: (be[blk(b, be, nu)], 0, 0))],
        out_specs=pl.BlockSpec((EXPERT_ROWS, d // 2), lambda b, be, nu: (b, 0)),
        scratch_shapes=[pltpu.VMEM((f, d), BF16)],
    )
    return pl.pallas_call(
        _expert_kernel,
        out_shape=jax.ShapeDtypeStruct((n_rows, d // 2), U32),
        grid_spec=grid_spec,
        compiler_params=_cparams("arbitrary"),
        name="experts",
    )(block_e, n_used, xs, w1p, b1p, w2, b2.reshape(ne, 1, d))


def _combine_kernel(dcur_ref, dnxt_ref, tw_ref, x_ref, g_ref, b_ref, ys_hbm, o_ref, ob_ref, buf_ref, sem, *, alpha):
    i = pl.program_id(0)
    tokens, d = x_ref.shape
    half = d // 2
    slot = i % 2

    def copy(s, t, k, src_row):
        return pltpu.make_async_copy(ys_hbm.at[pl.ds(src_row, 1)], buf_ref.at[s, k, pl.ds(t, 1)], sem.at[s])

    def request(d_ref, s):
        def body(t, c):
            for k in range(TOP_K):
                copy(s, t, k, d_ref[0, 0, t * TOP_K + k]).start()
            return c

        lax.fori_loop(0, tokens, body, 0)

    @pl.when(i == 0)
    def _():
        request(dcur_ref, 0)

    @pl.when(i + 1 < pl.num_programs(0))
    def _():
        request(dnxt_ref, 1 - slot)

    for k in range(TOP_K):
        pltpu.make_async_copy(ys_hbm.at[pl.ds(0, tokens)], buf_ref.at[slot, k], sem.at[slot]).wait()
    tw = tw_ref[...]
    x = x_ref[...]
    y_lo = alpha * x[:, :half]
    y_hi = alpha * x[:, half:]
    for k in range(TOP_K):
        u = buf_ref[slot, k]
        wk = tw[:, k:k + 1]
        y_lo = y_lo + wk * pltpu.bitcast(u << 16, F32)
        y_hi = y_hi + wk * pltpu.bitcast(u & jnp.uint32(0xFFFF0000), F32)
    out = _ln_rows(jnp.concatenate([y_lo, y_hi], axis=1), g_ref[...], b_ref[...])
    o_ref[...] = out
    ob_ref[...] = out.astype(BF16)


def _combine(ys, dest, tw, x, g, b, alpha, *, tokens=128):
    s, d = x.shape
    tokens = min(tokens, s)
    steps = s // tokens
    kern = functools.partial(_combine_kernel, alpha=alpha)
    dest3 = dest.reshape(steps, 1, tokens * TOP_K)
    row = lambda width: pl.BlockSpec((tokens, width), lambda i: (i, 0))
    const = lambda shape: pl.BlockSpec(shape, lambda i: (0,) * len(shape))
    return pl.pallas_call(
        kern,
        out_shape=(jax.ShapeDtypeStruct((s, d), F32), jax.ShapeDtypeStruct((s, d), BF16)),
        grid=(steps,),
        in_specs=[pl.BlockSpec((1, 1, tokens * TOP_K), lambda i: (i, 0, 0), memory_space=pltpu.SMEM),
                  pl.BlockSpec((1, 1, tokens * TOP_K), lambda i: (jnp.minimum(i + 1, steps - 1), 0, 0),
                               memory_space=pltpu.SMEM),
                  row(LANES), row(d), const((1, d)), const((1, d)),
                  pl.BlockSpec(memory_space=pl.ANY)],
        out_specs=(row(d), row(d)),
        scratch_shapes=[pltpu.VMEM((2, TOP_K, tokens, d // 2), U32), pltpu.SemaphoreType.DMA((2,))],
        compiler_params=_cparams("arbitrary"),
        name="combine",
    )(dest3, dest3, tw, x, g.reshape(1, d), b.reshape(1, d), ys)


def _mm_rows_kernel(starts_ref, a_ref, wt_ref, *refs, n_side):
    del starts_ref
    if n_side:
        w1_ref, o_ref, w1p_ref, wb_ref = refs
    else:
        o_ref, wb_ref = refs

    @pl.when(pl.program_id(1) == 0)
    def _():
        wb_ref[...] = wt_ref[0].astype(BF16)

    o_ref[...] = lax.dot_general(a_ref[...], wb_ref[...], (((1,), (1,)), ((), ())),
                                 preferred_element_type=F32).astype(o_ref.dtype)

    if n_side:
        @pl.when(pl.program_id(0) * pl.num_programs(1) + pl.program_id(1) < n_side)
        def _():
            _w1_prep_kernel(w1_ref, w1p_ref)


def _matmul_rows(a, wt, layer, starts, tn, *, tm, name, w1=None):
    m, k = a.shape
    tm = min(tm, m)
    n_tiles, m_tiles = len(starts), m // tm
    in_specs = [pl.BlockSpec((tm, k), lambda j, i, st: (i, 0)),
                pl.BlockSpec((pl.Element(1), pl.Element(tn), pl.Element(k)),
                             lambda j, i, st: (layer, pl.multiple_of(st[j], SUBLANES), 0))]
    out_specs = pl.BlockSpec((tm, tn), lambda j, i, st: (i, j))
    out_shape = jax.ShapeDtypeStruct((m, n_tiles * tn), F32)
    args = [jnp.asarray(starts, I32), a, wt]
    n_side = 0
    if w1 is not None:
        _, ne, d, f2 = w1.shape
        gw = 2 * LANES
        groups = f2 // gw
        r = 1
        while ne * groups * r * 2 <= n_tiles * m_tiles and (d // (r * 2)) % (2 * SUBLANES) == 0:
            r *= 2
        n_side = ne * groups * r
        assert n_side <= n_tiles * m_tiles

        def side(j, i, st):
            h = jnp.minimum(j * m_tiles + i, n_side - 1)
            return h // (groups * r), h % r, (h // r) % groups

        in_specs.append(pl.BlockSpec((None, 1, d // r, gw), lambda j, i, st: (layer, *side(j, i, st))))
        out_specs = (out_specs, pl.BlockSpec((1, d // r, gw), side))
        out_shape = (out_shape, jax.ShapeDtypeStruct((ne, d, f2), BF16))
        args.append(w1)
    grid_spec = pltpu.PrefetchScalarGridSpec(
        num_scalar_prefetch=1,
        grid=(n_tiles, m_tiles),
        in_specs=in_specs,
        out_specs=out_specs,
        scratch_shapes=[pltpu.VMEM((tn, k), BF16)],
    )
    return pl.pallas_call(
        functools.partial(_mm_rows_kernel, n_side=n_side),
        out_shape=out_shape,
        grid_spec=grid_spec,
        compiler_params=_cparams("arbitrary", "arbitrary"),
        name=name,
    )(*args)


def _in_proj(xb, w_in, layer, sizes, inner, w1):
    z, xbc, dt, u, v, gate_c, x_c, gates = sizes
    src = {"z": 0, "xs": z, "bc": z + inner, "dt": z + xbc}
    src["u"] = src["dt"] + dt
    src["v"] = src["u"] + u
    src["gate_c"] = src["v"] + v
    src["x_c"] = src["gate_c"] + gate_c
    src["gates"] = src["x_c"] + x_c
    width = {"z": z, "xs": inner, "u": u, "v": v, "gate_c": gate_c, "x_c": x_c, "bc": xbc - inner, "gates": gates}
    tn = max(t for t in (512, 256, 128) if all(w % t == 0 for w in width.values()))
    assert dt <= LANES and all(s % SUBLANES == 0 for s in src.values())
    lay, starts, off = {}, [], 0
    for name in ("z", "xs", "u", "v", "gate_c", "x_c", "bc", "gates"):
        lay[name] = off
        starts += [src[name] + t for t in range(0, width[name], tn)]
        off += width[name]
    wt = jnp.swapaxes(w_in, 1, 2)
    proj, w1p = _matmul_rows(xb, wt, layer, starts, tn, tm=1024, name="in_proj", w1=w1)
    dt_raw = _matmul_rows(xb, wt, layer, [src["dt"]], LANES, tm=1024, name="dt_proj")
    return proj, dt_raw, lay, w1p


def _moe_plan(te, rk, cnt, ne, n_blocks):
    counts = cnt[0, :ne]
    padded = (counts + EXPERT_ROWS - 1) // EXPERT_ROWS * EXPERT_ROWS
    pad_end = jnp.cumsum(padded)
    pad_start = pad_end - padded
    picked = te[:, :TOP_K, None] == jnp.arange(ne, dtype=I32)[None, None, :]
    dest = (jnp.sum(jnp.where(picked, pad_start[None, None, :], 0), axis=-1) + rk[:, :TOP_K]).astype(I32)
    dest = dest.reshape(-1)
    block_start = jnp.arange(n_blocks, dtype=I32) * EXPERT_ROWS
    block_e = jnp.sum(pad_end[None, :] <= block_start[:, None], axis=1)
    block_e = jnp.minimum(block_e, ne - 1).astype(I32)
    n_used = (pad_end[-1] // EXPERT_ROWS).astype(I32).reshape(1)
    return dest, block_e, n_used, pad_end.astype(I32), counts.astype(I32)


def kernel(x, ln_emb_g, ln_emb_b, w_in, b_merge, conv_a_w, conv_a_b, dt_bias, a_log, d_skip, norm_a_g,
           ln_v_g, ln_v_b, w_spatial, b_spatial, conv_c_w, conv_c_b, w_rg_a, b_rg_a, w_rg_x, b_rg_x, lam,
           p_a, p_b, p_c, w_o, ln_mix_g, ln_mix_b, w_router, b_router, w1, b1, w2, b2, ln_ffn_g, ln_ffn_b):
    bsz, seq, d = x.shape
    depth = w_in.shape[0]
    alpha = (2 * depth) ** 0.25
    heads = dt_bias.shape[1]
    inner = norm_a_g.shape[1]
    sizes = (inner, conv_a_w.shape[2], heads, ln_v_g.shape[1], ln_v_g.shape[1], lam.shape[1], lam.shape[1],
             3 * d)
    ne = w_router.shape[2]
    s = bsz * seq
    n_rows = s * TOP_K + ne * EXPERT_ROWS
    n_blocks = n_rows // EXPERT_ROWS

    xf, xb = _ln0(x.reshape(s, d), ln_emb_g, ln_emb_b)
    for l in range(depth):
        proj, dt_raw, lay, w1p = _in_proj(xb, w_in, l, sizes, inner, w1)
        y_a = _ssd(proj, dt_raw, lay, conv_a_w[l], conv_a_b[l], dt_bias[l], a_log[l], d_skip[l], norm_a_g[l])
        y_b = _gmlp(proj, lay, ln_v_g[l], ln_v_b[l], w_spatial[l], b_spatial[l])
        y_c = _lru(proj, lay, conv_c_w[l], conv_c_b[l], w_rg_a[l], b_rg_a[l], w_rg_x[l], b_rg_x[l], lam[l])
        merged = _merge(y_a, y_b, y_c, p_a[l].astype(BF16), p_b[l].astype(BF16), p_c[l].astype(BF16),
                        proj, lay, b_merge[l])
        h = _matmul(merged, w_o, l, F32, tm=1024, tn=512, res=xf, alpha=alpha, name="w_o")
        xm, xp, te, tw, rk, cnt = _ln_router(h, ln_mix_g[l], ln_mix_b[l], w_router[l], b_router[l])
        dest, block_e, n_used, pad_end, counts = _moe_plan(te, rk, cnt, ne, n_blocks)
        xs = _dispatch(xp, dest, pad_end, counts, n_rows)
        ys = _experts(xs, block_e, n_used, w1p, _regroup_bias(b1[l]), w2, l, b2[l])
        xf, xb = _combine(ys, dest, tw, xm, ln_ffn_g[l], ln_ffn_b[l], alpha)
    return xf.reshape(bsz, seq, d)
```

```python
import functools
import math

import jax
import jax.numpy as jnp
from jax import lax
from jax.experimental import pallas as pl
from jax.experimental.pallas import tpu as pltpu

F32 = jnp.float32
BF16 = jnp.bfloat16
U32 = jnp.uint32
I32 = jnp.int32

SSD_GROUPS = 4
SSD_STATE = 128
CHUNK = 128
GMLP_GROUPS = 8
LRU_C = 8.0
TOP_K = 4
SWIGLU_LIMIT = 7.0
SWIGLU_ALPHA = 1.702
LN_EPS = 1e-5
RMS_EPS = 1e-5

LANES = 128
SUBLANES = 8
VMEM_LIMIT = 56 * 1024 * 1024
EXPERT_ROWS = 256
WINDOW_BLOCK = 1024
NEG_BIG = -1e30


def _cparams(*sem):
    return pltpu.CompilerParams(dimension_semantics=sem, vmem_limit_bytes=VMEM_LIMIT)


def _dot(a, b):
    return jnp.dot(a, b, preferred_element_type=F32)


def _split3(x):
    hi = x.astype(BF16)
    r = x - hi.astype(F32)
    mid = r.astype(BF16)
    lo = (r - mid.astype(F32)).astype(BF16)
    return hi, mid, lo


def _exact_right(x, m):
    hi, mid, lo = _split3(x)
    return _dot(hi, m) + (_dot(mid, m) + _dot(lo, m))


def _exact_left(m, x):
    hi, mid, lo = _split3(x)
    return _dot(m, hi) + (_dot(m, mid) + _dot(m, lo))


def _sigmoid(x):
    return 1.0 / (1.0 + jnp.exp(-x))


def _silu(x):
    return x * _sigmoid(x)


def _softplus(x):
    return jnp.maximum(x, 0.0) + jnp.log1p(jnp.exp(-jnp.abs(x)))


def _gelu(x):
    c = math.sqrt(2.0 / math.pi)
    return x * (0.5 * (1.0 + jnp.tanh(c * (x + 0.044715 * (x * x * x)))))


def _ln_rows(x, g, b):
    mu = jnp.mean(x, axis=-1, keepdims=True)
    xc = x - mu
    var = jnp.mean(xc * xc, axis=-1, keepdims=True)
    return xc * lax.rsqrt(var + LN_EPS) * g + b


def _causal_conv(x, carry, w, b):
    kw = w.shape[0]
    rows = x.shape[0]
    xp = jnp.concatenate([carry, x], axis=0)
    y = x * w[kw - 1:kw, :] + b
    for k in range(kw - 1):
        s = SUBLANES - (kw - 1) + k
        y = y + xp[s:s + rows, :] * w[k:k + 1, :]
    return y


def _pack_halves(y):
    half = y.shape[1] // 2
    lo = pltpu.bitcast(y[:, :half].astype(BF16).astype(F32), U32)
    hi = pltpu.bitcast(y[:, half:].astype(BF16).astype(F32), U32)
    return (hi & jnp.uint32(0xFFFF0000)) | (lo >> 16)


def _unpack_halves(u):
    lo = pltpu.bitcast(u << 16, F32).astype(BF16)
    hi = pltpu.bitcast(u & jnp.uint32(0xFFFF0000), F32).astype(BF16)
    return lo, hi


def _window_specs(off, width, rows, blk):
    first, shift = divmod(off, blk)
    nblk = pl.cdiv(shift + width, blk)
    specs = [pl.BlockSpec((rows, blk), lambda i, q=q: (i, first + q)) for q in range(nblk)]
    return specs, shift


def _window_load(refs, shift, width, rows=slice(None)):
    blk = refs[0].shape[1]
    first, lane_shift = divmod(shift, LANES)
    n_out = width // LANES

    def lane_block(k):
        q, o = divmod(k * LANES, blk)
        return refs[q][rows, o:o + LANES].astype(F32)

    if lane_shift == 0:
        return jnp.concatenate([lane_block(first + k) for k in range(n_out)], axis=1)
    rolled = [pltpu.roll(lane_block(first + k), LANES - lane_shift, 1) for k in range(n_out + 1)]
    lane = lax.broadcasted_iota(I32, rolled[0].shape, 1)
    outs = [jnp.where(lane < LANES - lane_shift, rolled[k], rolled[k + 1]) for k in range(n_out)]
    return jnp.concatenate(outs, axis=1)


def _ln0_kernel(x_ref, g_ref, b_ref, o_ref, ob_ref):
    y = _ln_rows(x_ref[...], g_ref[...], b_ref[...])
    o_ref[...] = y
    ob_ref[...] = y.astype(BF16)


def _ln0(x, g, b):
    s, d = x.shape
    tm = min(256, s)
    return pl.pallas_call(
        _ln0_kernel,
        out_shape=(jax.ShapeDtypeStruct((s, d), F32), jax.ShapeDtypeStruct((s, d), BF16)),
        grid=(s // tm,),
        in_specs=[pl.BlockSpec((tm, d), lambda i: (i, 0)),
                  pl.BlockSpec((1, d), lambda i: (0, 0)),
                  pl.BlockSpec((1, d), lambda i: (0, 0))],
        out_specs=(pl.BlockSpec((tm, d), lambda i: (i, 0)),
                   pl.BlockSpec((tm, d), lambda i: (i, 0))),
        compiler_params=_cparams("parallel"),
        name="ln0",
    )(x, g.reshape(1, d), b.reshape(1, d))


def _cast_weight_tile(w_ref, wb_ref, n_valid):
    tn = w_ref.shape[1]
    col = pl.program_id(0) * tn + lax.broadcasted_iota(I32, w_ref.shape, 1)
    wb_ref[...] = jnp.where(col < n_valid, w_ref[...], 0.0).astype(BF16)


def _mm_kernel(a_ref, w_ref, o_ref, wb_ref, *, n_valid):
    @pl.when(pl.program_id(1) == 0)
    def _():
        _cast_weight_tile(w_ref, wb_ref, n_valid)

    o_ref[...] = _dot(a_ref[...], wb_ref[...]).astype(o_ref.dtype)


def _mm_res_kernel(a_ref, w_ref, r_ref, o_ref, wb_ref, *, n_valid, alpha):
    @pl.when(pl.program_id(1) == 0)
    def _():
        _cast_weight_tile(w_ref, wb_ref, n_valid)

    o_ref[...] = (alpha * r_ref[...] + _dot(a_ref[...], wb_ref[...])).astype(o_ref.dtype)


def _matmul(a, w, layer, out_dtype, *, tm, tn, res=None, alpha=1.0, name="mm"):
    m, k = a.shape
    n = w.shape[2]
    tm = min(tm, m)
    tn = min(tn, n)
    n_tiles = pl.cdiv(n, tn)
    in_specs = [pl.BlockSpec((tm, k), lambda j, i: (i, 0)),
                pl.BlockSpec((None, k, tn), lambda j, i: (layer, 0, j))]
    args = [a, w]
    if res is None:
        kern = functools.partial(_mm_kernel, n_valid=n)
    else:
        kern = functools.partial(_mm_res_kernel, n_valid=n, alpha=alpha)
        in_specs.append(pl.BlockSpec((tm, tn), lambda j, i: (i, j)))
        args.append(res)
    return pl.pallas_call(
        kern,
        out_shape=jax.ShapeDtypeStruct((m, n_tiles * tn), out_dtype),
        grid=(n_tiles, m // tm),
        in_specs=in_specs,
        out_specs=pl.BlockSpec((tm, tn), lambda j, i: (i, j)),
        scratch_shapes=[pltpu.VMEM((k, tn), BF16)],
        compiler_params=_cparams("parallel", "arbitrary"),
        name=name,
    )(*args)


def _ssd_kernel(z_ref, xs_ref, bc_ref, dt_ref, cwx_ref, cbx_ref, cwb_ref, cbb_ref, dtb_ref, alog_ref,
                dskip_ref, ng_ref, e_ref, o_ref, cx_ref, cb_ref, st_ref, *, heads, head_dim):
    i = pl.program_id(0)

    @pl.when(i == 0)
    def _():
        cx_ref[...] = jnp.zeros_like(cx_ref)
        cb_ref[...] = jnp.zeros_like(cb_ref)
        st_ref[...] = jnp.zeros_like(st_ref)

    n = SSD_STATE
    gw = (heads // SSD_GROUPS) * head_dim
    hpg = heads // SSD_GROUPS
    pair = LANES // head_dim

    xs_raw = xs_ref[...].astype(F32)
    bc_raw = bc_ref[...].astype(F32)
    xs = _silu(_causal_conv(xs_raw, cx_ref[...], cwx_ref[...], cbx_ref[...]))
    bc = _silu(_causal_conv(bc_raw, cb_ref[...], cwb_ref[...], cbb_ref[...]))
    cx_ref[...] = xs_raw[CHUNK - SUBLANES:, :]
    cb_ref[...] = bc_raw[CHUNK - SUBLANES:, :]

    row = lax.broadcasted_iota(I32, (CHUNK, CHUNK), 0)
    col = lax.broadcasted_iota(I32, (CHUNK, CHUNK), 1)
    dt = _softplus(jnp.where(col < heads, dt_ref[...], 0.0) + dtb_ref[...])
    da = dt * (-jnp.exp(alog_ref[...]))
    causal = row >= col
    ltri = jnp.where(causal, 1.0, 0.0).astype(BF16)
    cum = _exact_left(ltri, da)
    cum_t = cum.T
    expand = e_ref[...]
    cum_e = _exact_right(cum, expand)
    dt_e = _exact_right(dt, expand)
    last_e = cum_e[CHUNK - 1:CHUNK, :]
    decay_out = jnp.exp(cum_e)
    decay_st = jnp.exp(last_e - cum_e)
    chunk_decay = jnp.exp(last_e)
    xdt = xs * dt_e
    xdt_b = xdt.astype(BF16)
    xst_b = (xdt * decay_st).astype(BF16)
    lane = lax.broadcasted_iota(I32, (CHUNK, LANES), 1)

    for g in range(SSD_GROUPS):
        bg = bc[:, g * n:(g + 1) * n].astype(BF16)
        cg = bc[:, (SSD_GROUPS + g) * n:(SSD_GROUPS + g + 1) * n].astype(BF16)
        cb = lax.dot_general(cg, bg, (((1,), (1,)), ((), ())), preferred_element_type=F32)
        slabs = []
        for j in range(hpg // pair):
            h0 = g * hpg + j * pair
            sc = []
            for h in range(h0, h0 + pair):
                seg = cum[:, h:h + 1] - cum_t[h:h + 1, :]
                dec = jnp.exp(jnp.where(causal, seg, -jnp.inf))
                sc.append((cb * dec).astype(BF16))
            xp = xdt_b[:, h0 * head_dim:h0 * head_dim + LANES]
            blocks = [jnp.where((lane >= q * head_dim) & (lane < (q + 1) * head_dim), xp, jnp.zeros_like(xp))
                      for q in range(pair)]
            slabs.append(_dot(jnp.concatenate(sc, axis=1), jnp.concatenate(blocks, axis=0)))
        y_diag = jnp.concatenate(slabs, axis=1)
        cs = slice(g * gw, (g + 1) * gw)
        st = st_ref[g]
        y_off = _dot(cg, st.astype(BF16)) * decay_out[:, cs]
        st_new = lax.dot_general(bg, xst_b[:, cs], (((0,), (0,)), ((), ())), preferred_element_type=F32)
        st_ref[g] = st * chunk_decay[:, cs] + st_new
        y = y_diag + y_off + xs[:, cs] * dskip_ref[:, cs]
        yg = y * _silu(z_ref[:, cs].astype(F32))
        ms = jnp.mean(yg * yg, axis=-1, keepdims=True)
        o_ref[:, cs] = (yg * lax.rsqrt(ms + RMS_EPS) * ng_ref[:, cs]).astype(o_ref.dtype)


def _ssd(proj, dt_raw, lay, conv_w, conv_b, dt_bias, a_log, d_skip, norm_g):
    s = proj.shape[0]
    heads = dt_bias.shape[0]
    inner = norm_g.shape[0]
    head_dim = inner // heads
    bcw = 2 * SSD_GROUPS * SSD_STATE
    gw = inner // SSD_GROUPS
    pad = LANES - heads
    assert lay["z"] % inner == 0 and lay["xs"] % inner == 0 and lay["bc"] % bcw == 0
    expand = (jnp.arange(LANES)[:, None] == (jnp.arange(inner) // head_dim)[None, :]).astype(BF16)
    kern = functools.partial(_ssd_kernel, heads=heads, head_dim=head_dim)
    const = lambda shape: pl.BlockSpec(shape, lambda i: (0,) * len(shape))
    return pl.pallas_call(
        kern,
        out_shape=jax.ShapeDtypeStruct((s, inner), BF16),
        grid=(s // CHUNK,),
        in_specs=[pl.BlockSpec((CHUNK, inner), lambda i: (i, lay["z"] // inner)),
                  pl.BlockSpec((CHUNK, inner), lambda i: (i, lay["xs"] // inner)),
                  pl.BlockSpec((CHUNK, bcw), lambda i: (i, lay["bc"] // bcw)),
                  pl.BlockSpec((CHUNK, LANES), lambda i: (i, 0)),
                  const((conv_w.shape[0], inner)), const((1, inner)),
                  const((conv_w.shape[0], bcw)), const((1, bcw)),
                  const((1, LANES)), const((1, LANES)),
                  const((1, inner)), const((1, inner)), const((LANES, inner))],
        out_specs=pl.BlockSpec((CHUNK, inner), lambda i: (i, 0)),
        scratch_shapes=[pltpu.VMEM((SUBLANES, inner), F32), pltpu.VMEM((SUBLANES, bcw), F32),
                        pltpu.VMEM((SSD_GROUPS, SSD_STATE, gw), F32)],
        compiler_params=_cparams("arbitrary"),
        name="ssd",
    )(proj, proj, proj, dt_raw,
      conv_w[:, :inner], conv_b[:inner].reshape(1, inner),
      conv_w[:, inner:], conv_b[inner:].reshape(1, bcw),
      jnp.pad(dt_bias, (0, pad)).reshape(1, LANES), jnp.pad(a_log, (0, pad)).reshape(1, LANES),
      jnp.repeat(d_skip, head_dim).reshape(1, inner), norm_g.reshape(1, inner), expand)


def _gmlp_kernel(*refs, chunks, nu, nv, su, sv):
    u_refs, v_refs = refs[:nu], refs[nu:nu + nv]
    lg_ref, lb_ref, ws_ref, be_ref, o_ref = refs[nu + nv:]
    width = o_ref.shape[1]
    gw = width // GMLP_GROUPS
    row = lax.broadcasted_iota(I32, (CHUNK, CHUNK), 0)
    col = lax.broadcasted_iota(I32, (CHUNK, CHUNK), 1)
    tril = row >= col
    ws = [jnp.where(tril, ws_ref[g], 0.0).astype(BF16) for g in range(GMLP_GROUPS)]
    for c in range(chunks):
        rs = slice(c * CHUNK, (c + 1) * CHUNK)
        v = _ln_rows(_gelu(_window_load(v_refs, sv, width, rs)), lg_ref[...], lb_ref[...]).astype(BF16)
        u = _gelu(_window_load(u_refs, su, width, rs))
        for g in range(GMLP_GROUPS):
            cs = slice(g * gw, (g + 1) * gw)
            mixed = _dot(ws[g], v[:, cs]) + be_ref[:, cs]
            o_ref[rs, cs] = (u[:, cs] * mixed).astype(o_ref.dtype)


def _gmlp(proj, lay, ln_g, ln_b, w_s, b_s, *, rows=256):
    s = proj.shape[0]
    width = ln_g.shape[0]
    gw = width // GMLP_GROUPS
    rows = min(rows, s)
    bias_e = jnp.repeat(b_s.T, gw, axis=1)
    u_specs, su = _window_specs(lay["u"], width, rows, WINDOW_BLOCK)
    v_specs, sv = _window_specs(lay["v"], width, rows, WINDOW_BLOCK)
    kern = functools.partial(_gmlp_kernel, chunks=rows // CHUNK, nu=len(u_specs), nv=len(v_specs), su=su, sv=sv)
    const = lambda shape: pl.BlockSpec(shape, lambda i: (0,) * len(shape))
    return pl.pallas_call(
        kern,
        out_shape=jax.ShapeDtypeStruct((s, width), BF16),
        grid=(s // rows,),
        in_specs=u_specs + v_specs + [const((1, width)), const((1, width)),
                                      const((GMLP_GROUPS, CHUNK, CHUNK)), const((CHUNK, width))],
        out_specs=pl.BlockSpec((rows, width), lambda i: (i, 0)),
        compiler_params=_cparams("parallel"),
        name="gmlp",
    )(*([proj] * (len(u_specs) + len(v_specs))), ln_g.reshape(1, width), ln_b.reshape(1, width), w_s, bias_e)


def _lru_kernel(*refs, ng, nx, sg, sx):
    gate_refs, x_refs = refs[:ng], refs[ng:ng + nx]
    (cw_ref, cb_ref, wa_ref, ba_ref, wx_ref, bx_ref, lam_ref, w1_ref, o_ref, w1p_ref,
     carry_ref, h_ref) = refs[ng + nx:]
    i = pl.program_id(0)

    _w1_prep_kernel(w1_ref, w1p_ref)

    @pl.when(i == 0)
    def _():
        carry_ref[...] = jnp.zeros_like(carry_ref)
        h_ref[...] = jnp.zeros_like(h_ref)

    rows, width = o_ref.shape
    nblk, blk = wa_ref.shape[0], wa_ref.shape[1]
    x_raw = _window_load(x_refs, sx, width)
    xr = _causal_conv(x_raw, carry_ref[...], cw_ref[...], cb_ref[...])
    carry_ref[...] = x_raw[rows - SUBLANES:, :]
    xb = xr.astype(BF16)
    ra = jnp.concatenate([_dot(xb[:, q * blk:(q + 1) * blk], wa_ref[q]) for q in range(nblk)], axis=1)
    ia = jnp.concatenate([_dot(xb[:, q * blk:(q + 1) * blk], wx_ref[q]) for q in range(nblk)], axis=1)
    r = _sigmoid(ra + ba_ref[...])
    ig = _sigmoid(ia + bx_ref[...])
    log_a = (-LRU_C) * r * _softplus(-lam_ref[...])
    a = jnp.exp(log_a)
    b = jnp.sqrt((1.0 - a) * (1.0 + a)) * (ig * xr)
    ridx = lax.broadcasted_iota(I32, (SUBLANES, width), 0)
    h_in = h_ref[...]
    groups = []
    for g in range(rows // SUBLANES):
        gs = slice(g * SUBLANES, (g + 1) * SUBLANES)
        ag, bg = a[gs, :], b[gs, :]
        d = 1
        while d < SUBLANES:
            keep = ridx >= d
            bg = jnp.where(keep, ag * pltpu.roll(bg, d, 0) + bg, bg)
            ag = jnp.where(keep, ag * pltpu.roll(ag, d, 0), ag)
            d *= 2
        hg = ag * h_in + bg
        h_in = hg[SUBLANES - 1:SUBLANES, :]
        groups.append(hg)
    h = jnp.concatenate(groups, axis=0)
    h_ref[...] = h_in
    o_ref[...] = (h * _gelu(_window_load(gate_refs, sg, width))).astype(o_ref.dtype)


def _lru(proj, lay, conv_w, conv_b, w_a, b_a, w_x, b_x, lam, w1, layer, *, rows=128):
    s = proj.shape[0]
    width = lam.shape[0]
    nblk, blk = w_a.shape[0], w_a.shape[1]
    rows = min(rows, s)
    steps = s // rows
    g_specs, sg = _window_specs(lay["gate_c"], width, rows, WINDOW_BLOCK)
    x_specs, sx = _window_specs(lay["x_c"], width, rows, WINDOW_BLOCK)
    _, ne, d, f2 = w1.shape
    gw = 2 * LANES
    groups = f2 // gw
    nw = pl.cdiv(ne * groups, steps)
    assert groups % nw == 0
    per_e = groups // nw
    n_side = ne * per_e

    def side(i):
        h = jnp.minimum(i, n_side - 1)
        return h // per_e, 0, h % per_e

    kern = functools.partial(_lru_kernel, ng=len(g_specs), nx=len(x_specs), sg=sg, sx=sx)
    const = lambda shape: pl.BlockSpec(shape, lambda i: (0,) * len(shape))
    return pl.pallas_call(
        kern,
        out_shape=(jax.ShapeDtypeStruct((s, width), BF16), jax.ShapeDtypeStruct((ne, d, f2), BF16)),
        grid=(steps,),
        in_specs=g_specs + x_specs + [const((conv_w.shape[0], width)), const((1, width)),
                                      const((nblk, blk, blk)), const((1, width)),
                                      const((nblk, blk, blk)), const((1, width)), const((1, width)),
                                      pl.BlockSpec((None, 1, d, nw * gw), lambda i: (layer, *side(i)))],
        out_specs=(pl.BlockSpec((rows, width), lambda i: (i, 0)),
                   pl.BlockSpec((1, d, nw * gw), side)),
        scratch_shapes=[pltpu.VMEM((SUBLANES, width), F32), pltpu.VMEM((1, width), F32)],
        compiler_params=_cparams("arbitrary"),
        name="rglru",
    )(*([proj] * (len(g_specs) + len(x_specs))), conv_w, conv_b.reshape(1, width), w_a.astype(BF16),
      b_a.reshape(1, width), w_x.astype(BF16), b_x.reshape(1, width), lam.reshape(1, width), w1)


def _merge_kernel(*refs, shift, ng):
    y_refs, p_refs = refs[:3], refs[3:6]
    gate_refs = [refs[6 + br * ng:6 + (br + 1) * ng] for br in range(3)]
    b_refs = refs[6 + 3 * ng:9 + 3 * ng]
    o_ref = refs[9 + 3 * ng]
    tn = o_ref.shape[1]
    acc = None
    for br in range(3):
        gate = _sigmoid(_window_load(gate_refs[br], shift, tn) + b_refs[br][...])
        term = gate * _dot(y_refs[br][...], p_refs[br][...])
        acc = term if acc is None else acc + term
    o_ref[...] = acc.astype(o_ref.dtype)


def _merge(y_a, y_b, y_c, p_a, p_b, p_c, proj, lay, b_merge, *, tm=256, tn=1024):
    s, w = y_a.shape
    d = p_a.shape[1]
    tm = min(tm, s)
    tn = min(tn, d)
    g0, shift = divmod(lay["gates"], tn)
    assert shift < LANES
    ng = 2 if shift else 1
    nd = d // tn
    bm = b_merge.reshape(1, 3 * d)
    y_spec = pl.BlockSpec((tm, w), lambda j, i: (i, 0))
    p_spec = pl.BlockSpec((w, tn), lambda j, i: (0, j))
    g_specs = []
    for br in range(3):
        g_specs.append(pl.BlockSpec((tm, tn), lambda j, i, br=br: (i, g0 + br * nd + j)))
        if shift:
            g_specs.append(pl.BlockSpec((tm, LANES), lambda j, i, br=br: (i, (g0 + br * nd + j + 1) * (tn // LANES))))
    b_specs = [pl.BlockSpec((1, tn), lambda j, i, br=br: (0, br * nd + j)) for br in range(3)]
    kern = functools.partial(_merge_kernel, shift=shift, ng=ng)
    return pl.pallas_call(
        kern,
        out_shape=jax.ShapeDtypeStruct((s, d), BF16),
        grid=(d // tn, s // tm),
        in_specs=[y_spec, y_spec, y_spec, p_spec, p_spec, p_spec] + g_specs + b_specs,
        out_specs=pl.BlockSpec((tm, tn), lambda j, i: (i, j)),
        compiler_params=_cparams("parallel", "parallel"),
        name="merge",
    )(y_a, y_b, y_c, p_a, p_b, p_c, *([proj] * (3 * ng)), bm, bm, bm)


def _ln_router_kernel(h_ref, g_ref, b_ref, wr_ref, br_ref, x_ref, xp_ref, te_ref, tw_ref, rk_ref, cnt_ref,
                      carry_ref):
    i = pl.program_id(0)

    @pl.when(i == 0)
    def _():
        carry_ref[...] = jnp.zeros_like(carry_ref)

    rows = h_ref.shape[0]
    y = _ln_rows(h_ref[...], g_ref[...], b_ref[...])
    x_ref[...] = y
    xp_ref[...] = _pack_halves(y)
    yh = y.astype(BF16)
    yl = (y - yh.astype(F32)).astype(BF16)
    wr = wr_ref[...]
    wh = wr.astype(BF16)
    wl = (wr - wh.astype(F32)).astype(BF16)
    logits = _dot(yh, wh) + (_dot(yh, wl) + _dot(yl, wh)) + br_ref[...]
    lane = lax.broadcasted_iota(I32, (rows, LANES), 1)
    lane_f = lane.astype(F32)
    cur = logits
    vals, sels, idxs = [], [], []
    for _ in range(TOP_K):
        m = jnp.max(cur, axis=-1, keepdims=True)
        idx = jnp.min(jnp.where(cur == m, lane_f, float(LANES)), axis=-1, keepdims=True)
        sel = lane_f == idx
        vals.append(m)
        sels.append(sel)
        idxs.append(idx)
        cur = jnp.where(sel, -jnp.inf, cur)
    ex = [jnp.exp(v - vals[0]) for v in vals]
    den = ex[0]
    for e in ex[1:]:
        den = den + e
    multi = jnp.zeros((rows, LANES), F32)
    for sel in sels:
        multi = multi + jnp.where(sel, 1.0, 0.0)
    r2 = lax.broadcasted_iota(I32, (rows, rows), 0)
    c2 = lax.broadcasted_iota(I32, (rows, rows), 1)
    lower = jnp.where(r2 > c2, 1.0, 0.0).astype(BF16)
    before = _dot(lower, multi.astype(BF16)) + carry_ref[...]
    te = jnp.zeros((rows, LANES), F32)
    tw = jnp.zeros((rows, LANES), F32)
    rk = jnp.zeros((rows, LANES), F32)
    for k in range(TOP_K):
        rank_k = jnp.sum(jnp.where(sels[k], before, 0.0), axis=-1, keepdims=True)
        te = jnp.where(lane == k, idxs[k], te)
        tw = jnp.where(lane == k, ex[k] / den, tw)
        rk = jnp.where(lane == k, rank_k, rk)
    te_ref[...] = te.astype(I32)
    tw_ref[...] = tw
    rk_ref[...] = rk.astype(I32)
    carry_ref[...] = carry_ref[...] + jnp.sum(multi, axis=0, keepdims=True)
    cnt_ref[...] = carry_ref[...].astype(I32)


def _ln_router(h, g, b, w_router, b_router, *, tm=256):
    s, d = h.shape
    ne = w_router.shape[1]
    tm = min(tm, s)
    wr = jnp.pad(w_router, ((0, 0), (0, LANES - ne)))
    br = jnp.pad(b_router, (0, LANES - ne), constant_values=NEG_BIG).reshape(1, LANES)
    row = lambda width: pl.BlockSpec((tm, width), lambda i: (i, 0))
    const = lambda shape: pl.BlockSpec(shape, lambda i: (0,) * len(shape))
    return pl.pallas_call(
        _ln_router_kernel,
        out_shape=(jax.ShapeDtypeStruct((s, d), F32), jax.ShapeDtypeStruct((s, d // 2), U32),
                   jax.ShapeDtypeStruct((s, LANES), I32), jax.ShapeDtypeStruct((s, LANES), F32),
                   jax.ShapeDtypeStruct((s, LANES), I32), jax.ShapeDtypeStruct((1, LANES), I32)),
        grid=(s // tm,),
        in_specs=[row(d), const((1, d)), const((1, d)), const((d, LANES)), const((1, LANES))],
        out_specs=(row(d), row(d // 2), row(LANES), row(LANES), row(LANES), const((1, LANES))),
        scratch_shapes=[pltpu.VMEM((1, LANES), F32)],
        compiler_params=_cparams("arbitrary"),
        name="ln_router",
    )(h, g.reshape(1, d), b.reshape(1, d), wr, br)


def _row_copy(src_hbm, dst_hbm, sem, src_row, dst_row):
    return pltpu.make_async_copy(src_hbm.at[pl.ds(src_row, 1)], dst_hbm.at[pl.ds(dst_row, 1)], sem)


def _dispatch_kernel(dest_ref, end_ref, cnt_ref, x_ref, o_hbm, zero_ref, sem, zsem, *, tokens):
    @pl.when(pl.program_id(0) == 0)
    def _():
        zero_ref[...] = jnp.zeros_like(zero_ref)

        def fill(e):
            start = pl.multiple_of(end_ref[e] - EXPERT_ROWS, EXPERT_ROWS)
            return pltpu.make_async_copy(zero_ref, o_hbm.at[pl.ds(start, EXPERT_ROWS)], zsem)

        for e in range(end_ref.shape[0]):
            @pl.when(cnt_ref[e] > 0)
            def _():
                fill(e).start()

        for e in range(end_ref.shape[0]):
            @pl.when(cnt_ref[e] > 0)
            def _():
                fill(e).wait()

        def tail(b):
            return pltpu.make_async_copy(
                zero_ref, o_hbm.at[pl.ds(pl.multiple_of(b * EXPERT_ROWS, EXPERT_ROWS), EXPERT_ROWS)], zsem)

        n_used = end_ref[end_ref.shape[0] - 1] // EXPERT_ROWS
        n_blocks = o_hbm.shape[0] // EXPERT_ROWS
        lax.fori_loop(n_used, n_blocks, lambda b, c: (tail(b).start(), c)[1], 0)
        lax.fori_loop(n_used, n_blocks, lambda b, c: (tail(b).wait(), c)[1], 0)

    def issue(t, c):
        for k in range(TOP_K):
            _row_copy(x_ref, o_hbm, sem, t, dest_ref[0, 0, t * TOP_K + k]).start()
        return c

    lax.fori_loop(0, tokens, issue, 0)

    for _ in range(TOP_K):
        pltpu.make_async_copy(x_ref, o_hbm.at[pl.ds(0, tokens)], sem).wait()


def _dispatch(xp, dest, pad_end, counts, n_rows, *, tokens=256):
    s, w = xp.shape
    tokens = min(tokens, s)
    steps = s // tokens
    kern = functools.partial(_dispatch_kernel, tokens=tokens)
    return pl.pallas_call(
        kern,
        out_shape=jax.ShapeDtypeStruct((n_rows, w), xp.dtype),
        grid=(steps,),
        in_specs=[pl.BlockSpec((1, 1, tokens * TOP_K), lambda i: (i, 0, 0), memory_space=pltpu.SMEM),
                  pl.BlockSpec(memory_space=pltpu.SMEM),
                  pl.BlockSpec(memory_space=pltpu.SMEM),
                  pl.BlockSpec((tokens, w), lambda i: (i, 0))],
        out_specs=pl.BlockSpec(memory_space=pl.ANY),
        scratch_shapes=[pltpu.VMEM((EXPERT_ROWS, w), xp.dtype), pltpu.SemaphoreType.DMA,
                        pltpu.SemaphoreType.DMA],
        compiler_params=_cparams("arbitrary"),
        name="dispatch",
    )(dest.reshape(steps, 1, tokens * TOP_K), pad_end, counts, xp)


def _w1_prep_kernel(w_ref, o_ref):
    gw = 2 * LANES
    c = lax.broadcasted_iota(I32, (gw, gw), 0)
    j = lax.broadcasted_iota(I32, (gw, gw), 1)
    src = jnp.where(j < LANES, 2 * j, 2 * (j - LANES) + 1)
    perm = jnp.where(c == src, 1.0, 0.0).astype(BF16)
    for q in range(w_ref.shape[-1] // gw):
        cs = slice(q * gw, (q + 1) * gw)
        o_ref[0, :, cs] = _dot(w_ref[0, :, cs].astype(BF16), perm).astype(BF16)


def _regroup_bias(b1):
    ne, f2 = b1.shape
    return b1.reshape(ne, f2 // (2 * LANES), LANES, 2).transpose(0, 1, 3, 2).reshape(ne, 1, f2)


def _expert_kernel(be_ref, nu_ref, x_ref, w1_ref, b1_ref, w2_ref, b2_ref, o_ref):
    del be_ref
    used = pl.program_id(0) < nu_ref[0]

    @pl.when(used)
    def _():
        lo, hi = _unpack_halves(x_ref[...])
        half = lo.shape[1]
        gu = _dot(lo, w1_ref[0, :half, :]) + _dot(hi, w1_ref[0, half:, :]) + b1_ref[0]
        acts = []
        for j in range(gu.shape[1] // (2 * LANES)):
            glu = jnp.minimum(gu[:, 2 * j * LANES:(2 * j + 1) * LANES], SWIGLU_LIMIT)
            lin = jnp.clip(gu[:, (2 * j + 1) * LANES:(2 * j + 2) * LANES], -SWIGLU_LIMIT, SWIGLU_LIMIT)
            acts.append(glu * _sigmoid(SWIGLU_ALPHA * glu) * (lin + 1.0))
        act = jnp.concatenate(acts, axis=1)
        o_ref[...] = _pack_halves(_dot(act.astype(BF16), w2_ref[0]) + b2_ref[0])

    @pl.when(jnp.logical_not(used))
    def _():
        o_ref[...] = jnp.zeros_like(o_ref)


def _experts(xs, block_e, n_used, w1p, b1p, w2b, b2):
    n_rows, half = xs.shape
    ne, f, d = w2b.shape
    n_blocks = n_rows // EXPERT_ROWS

    def blk(b, be, nu):
        return jnp.minimum(b, nu[0] - 1)

    grid_spec = pltpu.PrefetchScalarGridSpec(
        num_scalar_prefetch=2,
        grid=(n_blocks,),
        in_specs=[pl.BlockSpec((EXPERT_ROWS, half), lambda b, be, nu: (blk(b, be, nu), 0)),
                  pl.BlockSpec((1, d, 2 * f), lambda b, be, nu: (be[blk(b, be, nu)], 0, 0)),
                  pl.BlockSpec((1, 1, 2 * f), lambda b, be, nu: (be[blk(b, be, nu)], 0, 0)),
                  pl.BlockSpec((1, f, d), lambda b, be, nu: (be[blk(b, be, nu)], 0, 0)),
                  pl.BlockSpec((1, 1, d), lambda b, be, nu: (be[blk(b, be, nu)], 0, 0))],
        out_specs=pl.BlockSpec((EXPERT_ROWS, d // 2), lambda b, be, nu: (b, 0)),
    )
    return pl.pallas_call(
        _expert_kernel,
        out_shape=jax.ShapeDtypeStruct((n_rows, d // 2), U32),
        grid_spec=grid_spec,
        compiler_params=_cparams("arbitrary"),
        name="experts",
    )(block_e, n_used, xs, w1p, b1p, w2b, b2.reshape(ne, 1, d))


def _combine_kernel(dcur_ref, dnxt_ref, tw_ref, x_ref, g_ref, b_ref, ys_hbm, o_ref, ob_ref, buf_ref, sem, *, alpha):
    i = pl.program_id(0)
    tokens, d = x_ref.shape
    half = d // 2
    slot = i % 2

    def copy(s, t, k, src_row):
        return pltpu.make_async_copy(ys_hbm.at[pl.ds(src_row, 1)], buf_ref.at[s, k, pl.ds(t, 1)], sem.at[s])

    def request(d_ref, s):
        def body(t, c):
            for k in range(TOP_K):
                copy(s, t, k, d_ref[0, 0, t * TOP_K + k]).start()
            return c

        lax.fori_loop(0, tokens, body, 0)

    @pl.when(i == 0)
    def _():
        request(dcur_ref, 0)

    @pl.when(i + 1 < pl.num_programs(0))
    def _():
        request(dnxt_ref, 1 - slot)

    for k in range(TOP_K):
        pltpu.make_async_copy(ys_hbm.at[pl.ds(0, tokens)], buf_ref.at[slot, k], sem.at[slot]).wait()
    tw = tw_ref[...]
    x = x_ref[...]
    y_lo = alpha * x[:, :half]
    y_hi = alpha * x[:, half:]
    for k in range(TOP_K):
        u = buf_ref[slot, k]
        wk = tw[:, k:k + 1]
        y_lo = y_lo + wk * pltpu.bitcast(u << 16, F32)
        y_hi = y_hi + wk * pltpu.bitcast(u & jnp.uint32(0xFFFF0000), F32)
    out = _ln_rows(jnp.concatenate([y_lo, y_hi], axis=1), g_ref[...], b_ref[...])
    o_ref[...] = out
    ob_ref[...] = out.astype(BF16)


def _combine(ys, dest, tw, x, g, b, alpha, *, tokens=128):
    s, d = x.shape
    tokens = min(tokens, s)
    steps = s // tokens
    kern = functools.partial(_combine_kernel, alpha=alpha)
    dest3 = dest.reshape(steps, 1, tokens * TOP_K)
    row = lambda width: pl.BlockSpec((tokens, width), lambda i: (i, 0))
    const = lambda shape: pl.BlockSpec(shape, lambda i: (0,) * len(shape))
    return pl.pallas_call(
        kern,
        out_shape=(jax.ShapeDtypeStruct((s, d), F32), jax.ShapeDtypeStruct((s, d), BF16)),
        grid=(steps,),
        in_specs=[pl.BlockSpec((1, 1, tokens * TOP_K), lambda i: (i, 0, 0), memory_space=pltpu.SMEM),
                  pl.BlockSpec((1, 1, tokens * TOP_K), lambda i: (jnp.minimum(i + 1, steps - 1), 0, 0),
                               memory_space=pltpu.SMEM),
                  row(LANES), row(d), const((1, d)), const((1, d)),
                  pl.BlockSpec(memory_space=pl.ANY)],
        out_specs=(row(d), row(d)),
        scratch_shapes=[pltpu.VMEM((2, TOP_K, tokens, d // 2), U32), pltpu.SemaphoreType.DMA((2,))],
        compiler_params=_cparams("arbitrary"),
        name="combine",
    )(dest3, dest3, tw, x, g.reshape(1, d), b.reshape(1, d), ys)


def _mm_rows_kernel(starts_ref, a_ref, wt_ref, *refs, n_casts):
    del starts_ref
    cast_in, o_ref, cast_out, wb_ref = refs[:n_casts], refs[n_casts], refs[n_casts + 1:-1], refs[-1]

    @pl.when(pl.program_id(1) == 0)
    def _():
        wb_ref[...] = wt_ref[0].astype(BF16)

    o_ref[...] = lax.dot_general(a_ref[...], wb_ref[...], (((1,), (1,)), ((), ())),
                                 preferred_element_type=F32).astype(o_ref.dtype)

    for src, dst in zip(cast_in, cast_out):
        dst[...] = src[...].astype(BF16)


def _matmul_rows(a, wt, layer, starts, tn, *, tm, name, casts=()):
    m, k = a.shape
    tm = min(tm, m)
    n_tiles, m_tiles = len(starts), m // tm
    in_specs = [pl.BlockSpec((tm, k), lambda j, i, st: (i, 0)),
                pl.BlockSpec((pl.Element(1), pl.Element(tn), pl.Element(k)),
                             lambda j, i, st: (layer, pl.multiple_of(st[j], SUBLANES), 0))]
    out_specs = [pl.BlockSpec((tm, tn), lambda j, i, st: (i, j))]
    out_shape = [jax.ShapeDtypeStruct((m, n_tiles * tn), F32)]
    args = [jnp.asarray(starts, I32), a, wt]
    for arr, rows, rb in casts:
        nb = rows // rb
        assert rows % rb == 0 and nb <= n_tiles * m_tiles
        cols = arr.shape[1]
        in_specs.append(pl.BlockSpec(
            (rb, cols), lambda j, i, st, nb=nb: (layer * nb + jnp.minimum(j * m_tiles + i, nb - 1), 0)))
        out_specs.append(pl.BlockSpec(
            (rb, cols), lambda j, i, st, nb=nb: (jnp.minimum(j * m_tiles + i, nb - 1), 0)))
        out_shape.append(jax.ShapeDtypeStruct((rows, cols), BF16))
        args.append(arr)
    grid_spec = pltpu.PrefetchScalarGridSpec(
        num_scalar_prefetch=1,
        grid=(n_tiles, m_tiles),
        in_specs=in_specs,
        out_specs=tuple(out_specs),
        scratch_shapes=[pltpu.VMEM((tn, k), BF16)],
    )
    return pl.pallas_call(
        functools.partial(_mm_rows_kernel, n_casts=len(casts)),
        out_shape=tuple(out_shape),
        grid_spec=grid_spec,
        compiler_params=_cparams("arbitrary", "arbitrary"),
        name=name,
    )(*args)


def _in_proj(xb, w_in, layer, sizes, inner, side_weights):
    z, xbc, dt, u, v, gate_c, x_c, gates = sizes
    src = {"z": 0, "xs": z, "bc": z + inner, "dt": z + xbc}
    src["u"] = src["dt"] + dt
    src["v"] = src["u"] + u
    src["gate_c"] = src["v"] + v
    src["x_c"] = src["gate_c"] + gate_c
    src["gates"] = src["x_c"] + x_c
    width = {"z": z, "xs": inner, "u": u, "v": v, "gate_c": gate_c, "x_c": x_c, "bc": xbc - inner, "gates": gates}
    tn = max(t for t in (512, 256, 128) if all(w % t == 0 for w in width.values()))
    assert dt <= LANES and all(s % SUBLANES == 0 for s in src.values())
    lay, starts, off = {}, [], 0
    for name in ("z", "xs", "u", "v", "gate_c", "x_c", "bc", "gates"):
        lay[name] = off
        starts += [src[name] + t for t in range(0, width[name], tn)]
        off += width[name]
    wt = jnp.swapaxes(w_in, 1, 2)
    tm = min(1024, xb.shape[0])
    steps = len(starts) * (xb.shape[0] // tm)

    def cast_job(arr):
        rows = math.prod(arr.shape[1:-1])
        rb = 2 * SUBLANES
        while rows // rb > steps or rows % rb:
            rb *= 2
        return arr.reshape(-1, arr.shape[-1]), rows, rb

    proj, *copies = _matmul_rows(xb, wt, layer, starts, tn, tm=tm, name="in_proj",
                                 casts=[cast_job(w) for w in side_weights])
    (dt_raw,) = _matmul_rows(xb, wt, layer, [src["dt"]], LANES, tm=tm, name="dt_proj")
    return proj, dt_raw, lay, copies


def _moe_plan(te, rk, cnt, ne, n_blocks):
    counts = cnt[0, :ne]
    padded = (counts + EXPERT_ROWS - 1) // EXPERT_ROWS * EXPERT_ROWS
    pad_end = jnp.cumsum(padded)
    pad_start = pad_end - padded
    picked = te[:, :TOP_K, None] == jnp.arange(ne, dtype=I32)[None, None, :]
    dest = (jnp.sum(jnp.where(picked, pad_start[None, None, :], 0), axis=-1) + rk[:, :TOP_K]).astype(I32)
    dest = dest.reshape(-1)
    block_start = jnp.arange(n_blocks, dtype=I32) * EXPERT_ROWS
    block_e = jnp.sum(pad_end[None, :] <= block_start[:, None], axis=1)
    block_e = jnp.minimum(block_e, ne - 1).astype(I32)
    n_used = (pad_end[-1] // EXPERT_ROWS).astype(I32).reshape(1)
    return dest, block_e, n_used, pad_end.astype(I32), counts.astype(I32)


def kernel(x, ln_emb_g, ln_emb_b, w_in, b_merge, conv_a_w, conv_a_b, dt_bias, a_log, d_skip, norm_a_g,
           ln_v_g, ln_v_b, w_spatial, b_spatial, conv_c_w, conv_c_b, w_rg_a, b_rg_a, w_rg_x, b_rg_x, lam,
           p_a, p_b, p_c, w_o, ln_mix_g, ln_mix_b, w_router, b_router, w1, b1, w2, b2, ln_ffn_g, ln_ffn_b):
    bsz, seq, d = x.shape
    depth = w_in.shape[0]
    alpha = (2 * depth) ** 0.25
    heads = dt_bias.shape[1]
    inner = norm_a_g.shape[1]
    sizes = (inner, conv_a_w.shape[2], heads, ln_v_g.shape[1], ln_v_g.shape[1], lam.shape[1], lam.shape[1],
             3 * d)
    ne = w_router.shape[2]
    s = bsz * seq
    n_rows = s * TOP_K + ne * EXPERT_ROWS
    n_blocks = n_rows // EXPERT_ROWS

    xf, xb = _ln0(x.reshape(s, d), ln_emb_g, ln_emb_b)
    for l in range(depth):
        proj, dt_raw, lay, (w2b, pab, pbb, pcb) = _in_proj(xb, w_in, l, sizes, inner, (w2, p_a, p_b, p_c))
        y_a = _ssd(proj, dt_raw, lay, conv_a_w[l], conv_a_b[l], dt_bias[l], a_log[l], d_skip[l], norm_a_g[l])
        y_b = _gmlp(proj, lay, ln_v_g[l], ln_v_b[l], w_spatial[l], b_spatial[l])
        y_c, w1p = _lru(proj, lay, conv_c_w[l], conv_c_b[l], w_rg_a[l], b_rg_a[l], w_rg_x[l], b_rg_x[l], lam[l],
                        w1, l)
        merged = _merge(y_a, y_b, y_c, pab, pbb, pcb, proj, lay, b_merge[l])
        h = _matmul(merged, w_o, l, F32, tm=1024, tn=512, res=xf, alpha=alpha, name="w_o")
        xm, xp, te, tw, rk, cnt = _ln_router(h, ln_mix_g[l], ln_mix_b[l], w_router[l], b_router[l])
        dest, block_e, n_used, pad_end, counts = _moe_plan(te, rk, cnt, ne, n_blocks)
        xs = _dispatch(xp, dest, pad_end, counts, n_rows)
        ys = _experts(xs, block_e, n_used, w1p, _regroup_bias(b1[l]), w2b.reshape(w2.shape[1:]), b2[l])
        xf, xb = _combine(ys, dest, tw, xm, ln_ffn_g[l], ln_ffn_b[l], alpha)
    return xf.reshape(bsz, seq, d)
```

```python
import functools
import math

import jax
import jax.numpy as jnp
from jax import lax
from jax.experimental import pallas as pl
from jax.experimental.pallas import tpu as pltpu

F32 = jnp.float32
BF16 = jnp.bfloat16
U32 = jnp.uint32
I32 = jnp.int32

SSD_GROUPS = 4
SSD_STATE = 128
CHUNK = 128
GMLP_GROUPS = 8
LRU_C = 8.0
TOP_K = 4
SWIGLU_LIMIT = 7.0
SWIGLU_ALPHA = 1.702
LN_EPS = 1e-5
RMS_EPS = 1e-5

LANES = 128
SUBLANES = 8
VMEM_LIMIT = 56 * 1024 * 1024
IN_PROJ_VMEM_LIMIT = 60 * 1024 * 1024
EXPERT_ROWS = 256
WINDOW_BLOCK = 1024
NEG_BIG = -1e30


def _cparams(*sem, vmem=VMEM_LIMIT):
    return pltpu.CompilerParams(dimension_semantics=sem, vmem_limit_bytes=vmem)


def _dot(a, b):
    return jnp.dot(a, b, preferred_element_type=F32)


def _split3(x):
    hi = x.astype(BF16)
    r = x - hi.astype(F32)
    mid = r.astype(BF16)
    lo = (r - mid.astype(F32)).astype(BF16)
    return hi, mid, lo


def _exact_right(x, m):
    hi, mid, lo = _split3(x)
    return _dot(hi, m) + (_dot(mid, m) + _dot(lo, m))


def _exact_left(m, x):
    hi, mid, lo = _split3(x)
    return _dot(m, hi) + (_dot(m, mid) + _dot(m, lo))


def _sigmoid(x):
    return 1.0 / (1.0 + jnp.exp(-x))


def _silu(x):
    return x * _sigmoid(x)


def _softplus(x):
    return jnp.maximum(x, 0.0) + jnp.log1p(jnp.exp(-jnp.abs(x)))


def _gelu(x):
    c = math.sqrt(2.0 / math.pi)
    return x * (0.5 * (1.0 + jnp.tanh(c * (x + 0.044715 * (x * x * x)))))


def _ln_rows(x, g, b):
    mu = jnp.mean(x, axis=-1, keepdims=True)
    xc = x - mu
    var = jnp.mean(xc * xc, axis=-1, keepdims=True)
    return xc * lax.rsqrt(var + LN_EPS) * g + b


def _causal_conv(x, carry, w, b):
    kw = w.shape[0]
    rows = x.shape[0]
    xp = jnp.concatenate([carry, x], axis=0)
    y = x * w[kw - 1:kw, :] + b
    for k in range(kw - 1):
        s = SUBLANES - (kw - 1) + k
        y = y + xp[s:s + rows, :] * w[k:k + 1, :]
    return y


def _pack_halves(y):
    half = y.shape[1] // 2
    lo = pltpu.bitcast(y[:, :half].astype(BF16).astype(F32), U32)
    hi = pltpu.bitcast(y[:, half:].astype(BF16).astype(F32), U32)
    return (hi & jnp.uint32(0xFFFF0000)) | (lo >> 16)


def _unpack_halves(u):
    lo = pltpu.bitcast(u << 16, F32).astype(BF16)
    hi = pltpu.bitcast(u & jnp.uint32(0xFFFF0000), F32).astype(BF16)
    return lo, hi


def _window_specs(off, width, rows, blk):
    first, shift = divmod(off, blk)
    nblk = pl.cdiv(shift + width, blk)
    specs = [pl.BlockSpec((rows, blk), lambda i, q=q: (i, first + q)) for q in range(nblk)]
    return specs, shift


def _window_load(refs, shift, width, rows=slice(None)):
    blk = refs[0].shape[1]
    first, lane_shift = divmod(shift, LANES)
    n_out = width // LANES

    def lane_block(k):
        q, o = divmod(k * LANES, blk)
        return refs[q][rows, o:o + LANES].astype(F32)

    if lane_shift == 0:
        return jnp.concatenate([lane_block(first + k) for k in range(n_out)], axis=1)
    rolled = [pltpu.roll(lane_block(first + k), LANES - lane_shift, 1) for k in range(n_out + 1)]
    lane = lax.broadcasted_iota(I32, rolled[0].shape, 1)
    outs = [jnp.where(lane < LANES - lane_shift, rolled[k], rolled[k + 1]) for k in range(n_out)]
    return jnp.concatenate(outs, axis=1)


def _ln0_kernel(x_ref, g_ref, b_ref, o_ref, ob_ref):
    y = _ln_rows(x_ref[...], g_ref[...], b_ref[...])
    o_ref[...] = y
    ob_ref[...] = y.astype(BF16)


def _ln0(x, g, b):
    s, d = x.shape
    tm = min(256, s)
    return pl.pallas_call(
        _ln0_kernel,
        out_shape=(jax.ShapeDtypeStruct((s, d), F32), jax.ShapeDtypeStruct((s, d), BF16)),
        grid=(s // tm,),
        in_specs=[pl.BlockSpec((tm, d), lambda i: (i, 0)),
                  pl.BlockSpec((1, d), lambda i: (0, 0)),
                  pl.BlockSpec((1, d), lambda i: (0, 0))],
        out_specs=(pl.BlockSpec((tm, d), lambda i: (i, 0)),
                   pl.BlockSpec((tm, d), lambda i: (i, 0))),
        compiler_params=_cparams("parallel"),
        name="ln0",
    )(x, g.reshape(1, d), b.reshape(1, d))


def _cast_weight_tile(w_ref, wb_ref, n_valid):
    tn = w_ref.shape[1]
    col = pl.program_id(0) * tn + lax.broadcasted_iota(I32, w_ref.shape, 1)
    wb_ref[...] = jnp.where(col < n_valid, w_ref[...], 0.0).astype(BF16)


def _mm_kernel(a_ref, w_ref, o_ref, wb_ref, *, n_valid):
    @pl.when(pl.program_id(1) == 0)
    def _():
        _cast_weight_tile(w_ref, wb_ref, n_valid)

    o_ref[...] = _dot(a_ref[...], wb_ref[...]).astype(o_ref.dtype)


def _mm_res_kernel(a_ref, w_ref, r_ref, o_ref, wb_ref, *, n_valid, alpha):
    @pl.when(pl.program_id(1) == 0)
    def _():
        _cast_weight_tile(w_ref, wb_ref, n_valid)

    o_ref[...] = (alpha * r_ref[...] + _dot(a_ref[...], wb_ref[...])).astype(o_ref.dtype)


def _matmul(a, w, layer, out_dtype, *, tm, tn, res=None, alpha=1.0, name="mm"):
    m, k = a.shape
    n = w.shape[2]
    tm = min(tm, m)
    tn = min(tn, n)
    n_tiles = pl.cdiv(n, tn)
    in_specs = [pl.BlockSpec((tm, k), lambda j, i: (i, 0)),
                pl.BlockSpec((None, k, tn), lambda j, i: (layer, 0, j))]
    args = [a, w]
    if res is None:
        kern = functools.partial(_mm_kernel, n_valid=n)
    else:
        kern = functools.partial(_mm_res_kernel, n_valid=n, alpha=alpha)
        in_specs.append(pl.BlockSpec((tm, tn), lambda j, i: (i, j)))
        args.append(res)
    return pl.pallas_call(
        kern,
        out_shape=jax.ShapeDtypeStruct((m, n_tiles * tn), out_dtype),
        grid=(n_tiles, m // tm),
        in_specs=in_specs,
        out_specs=pl.BlockSpec((tm, tn), lambda j, i: (i, j)),
        scratch_shapes=[pltpu.VMEM((k, tn), BF16)],
        compiler_params=_cparams("parallel", "arbitrary"),
        name=name,
    )(*args)


def _ssd_kernel(z_ref, xs_ref, bc_ref, dt_ref, cwx_ref, cbx_ref, cwb_ref, cbb_ref, dtb_ref, alog_ref,
                dskip_ref, ng_ref, e_ref, o_ref, cx_ref, cb_ref, st_ref, *, heads, head_dim):
    i = pl.program_id(0)

    @pl.when(i == 0)
    def _():
        cx_ref[...] = jnp.zeros_like(cx_ref)
        cb_ref[...] = jnp.zeros_like(cb_ref)
        st_ref[...] = jnp.zeros_like(st_ref)

    n = SSD_STATE
    gw = (heads // SSD_GROUPS) * head_dim
    hpg = heads // SSD_GROUPS
    pair = LANES // head_dim

    xs_raw = xs_ref[...].astype(F32)
    bc_raw = bc_ref[...].astype(F32)
    xs = _silu(_causal_conv(xs_raw, cx_ref[...], cwx_ref[...], cbx_ref[...]))
    bc = _silu(_causal_conv(bc_raw, cb_ref[...], cwb_ref[...], cbb_ref[...]))
    cx_ref[...] = xs_raw[CHUNK - SUBLANES:, :]
    cb_ref[...] = bc_raw[CHUNK - SUBLANES:, :]

    row = lax.broadcasted_iota(I32, (CHUNK, CHUNK), 0)
    col = lax.broadcasted_iota(I32, (CHUNK, CHUNK), 1)
    dt = _softplus(jnp.where(col < heads, dt_ref[...], 0.0) + dtb_ref[...])
    da = dt * (-jnp.exp(alog_ref[...]))
    causal = row >= col
    ltri = jnp.where(causal, 1.0, 0.0).astype(BF16)
    cum = _exact_left(ltri, da)
    cum_t = cum.T
    expand = e_ref[...]
    cum_e = _exact_right(cum, expand)
    dt_e = _exact_right(dt, expand)
    last_e = cum_e[CHUNK - 1:CHUNK, :]
    decay_out = jnp.exp(cum_e)
    decay_st = jnp.exp(last_e - cum_e)
    chunk_decay = jnp.exp(last_e)
    xdt = xs * dt_e
    xdt_b = xdt.astype(BF16)
    xst_b = (xdt * decay_st).astype(BF16)
    lane = lax.broadcasted_iota(I32, (CHUNK, LANES), 1)

    for g in range(SSD_GROUPS):
        bg = bc[:, g * n:(g + 1) * n].astype(BF16)
        cg = bc[:, (SSD_GROUPS + g) * n:(SSD_GROUPS + g + 1) * n].astype(BF16)
        cb = lax.dot_general(cg, bg, (((1,), (1,)), ((), ())), preferred_element_type=F32)
        slabs = []
        for j in range(hpg // pair):
            h0 = g * hpg + j * pair
            sc = []
            for h in range(h0, h0 + pair):
                seg = cum[:, h:h + 1] - cum_t[h:h + 1, :]
                dec = jnp.exp(jnp.where(causal, seg, -jnp.inf))
                sc.append((cb * dec).astype(BF16))
            xp = xdt_b[:, h0 * head_dim:h0 * head_dim + LANES]
            blocks = [jnp.where((lane >= q * head_dim) & (lane < (q + 1) * head_dim), xp, jnp.zeros_like(xp))
                      for q in range(pair)]
            slabs.append(_dot(jnp.concatenate(sc, axis=1), jnp.concatenate(blocks, axis=0)))
        y_diag = jnp.concatenate(slabs, axis=1)
        cs = slice(g * gw, (g + 1) * gw)
        st = st_ref[g]
        y_off = _dot(cg, st.astype(BF16)) * decay_out[:, cs]
        st_new = lax.dot_general(bg, xst_b[:, cs], (((0,), (0,)), ((), ())), preferred_element_type=F32)
        st_ref[g] = st * chunk_decay[:, cs] + st_new
        y = y_diag + y_off + xs[:, cs] * dskip_ref[:, cs]
        yg = y * _silu(z_ref[:, cs].astype(F32))
        ms = jnp.mean(yg * yg, axis=-1, keepdims=True)
        o_ref[:, cs] = (yg * lax.rsqrt(ms + RMS_EPS) * ng_ref[:, cs]).astype(o_ref.dtype)


def _ssd(proj, dt_raw, lay, conv_w, conv_b, dt_bias, a_log, d_skip, norm_g):
    s = proj.shape[0]
    heads = dt_bias.shape[0]
    inner = norm_g.shape[0]
    head_dim = inner // heads
    bcw = 2 * SSD_GROUPS * SSD_STATE
    gw = inner // SSD_GROUPS
    pad = LANES - heads
    assert lay["z"] % inner == 0 and lay["xs"] % inner == 0 and lay["bc"] % bcw == 0
    expand = (jnp.arange(LANES)[:, None] == (jnp.arange(inner) // head_dim)[None, :]).astype(BF16)
    kern = functools.partial(_ssd_kernel, heads=heads, head_dim=head_dim)
    const = lambda shape: pl.BlockSpec(shape, lambda i: (0,) * len(shape))
    return pl.pallas_call(
        kern,
        out_shape=jax.ShapeDtypeStruct((s, inner), BF16),
        grid=(s // CHUNK,),
        in_specs=[pl.BlockSpec((CHUNK, inner), lambda i: (i, lay["z"] // inner)),
                  pl.BlockSpec((CHUNK, inner), lambda i: (i, lay["xs"] // inner)),
                  pl.BlockSpec((CHUNK, bcw), lambda i: (i, lay["bc"] // bcw)),
                  pl.BlockSpec((CHUNK, LANES), lambda i: (i, 0)),
                  const((conv_w.shape[0], inner)), const((1, inner)),
                  const((conv_w.shape[0], bcw)), const((1, bcw)),
                  const((1, LANES)), const((1, LANES)),
                  const((1, inner)), const((1, inner)), const((LANES, inner))],
        out_specs=pl.BlockSpec((CHUNK, inner), lambda i: (i, 0)),
        scratch_shapes=[pltpu.VMEM((SUBLANES, inner), F32), pltpu.VMEM((SUBLANES, bcw), F32),
                        pltpu.VMEM((SSD_GROUPS, SSD_STATE, gw), F32)],
        compiler_params=_cparams("arbitrary"),
        name="ssd",
    )(proj, proj, proj, dt_raw,
      conv_w[:, :inner], conv_b[:inner].reshape(1, inner),
      conv_w[:, inner:], conv_b[inner:].reshape(1, bcw),
      jnp.pad(dt_bias, (0, pad)).reshape(1, LANES), jnp.pad(a_log, (0, pad)).reshape(1, LANES),
      jnp.repeat(d_skip, head_dim).reshape(1, inner), norm_g.reshape(1, inner), expand)


def _gmlp_kernel(*refs, chunks, nu, nv, su, sv):
    u_refs, v_refs = refs[:nu], refs[nu:nu + nv]
    lg_ref, lb_ref, ws_ref, be_ref, o_ref = refs[nu + nv:]
    width = o_ref.shape[1]
    gw = width // GMLP_GROUPS
    row = lax.broadcasted_iota(I32, (CHUNK, CHUNK), 0)
    col = lax.broadcasted_iota(I32, (CHUNK, CHUNK), 1)
    tril = row >= col
    ws = [jnp.where(tril, ws_ref[g], 0.0).astype(BF16) for g in range(GMLP_GROUPS)]
    for c in range(chunks):
        rs = slice(c * CHUNK, (c + 1) * CHUNK)
        v = _ln_rows(_gelu(_window_load(v_refs, sv, width, rs)), lg_ref[...], lb_ref[...]).astype(BF16)
        u = _gelu(_window_load(u_refs, su, width, rs))
        for g in range(GMLP_GROUPS):
            cs = slice(g * gw, (g + 1) * gw)
            mixed = _dot(ws[g], v[:, cs]) + be_ref[:, cs]
            o_ref[rs, cs] = (u[:, cs] * mixed).astype(o_ref.dtype)


def _gmlp(proj, lay, ln_g, ln_b, w_s, b_s, *, rows=256):
    s = proj.shape[0]
    width = ln_g.shape[0]
    gw = width // GMLP_GROUPS
    rows = min(rows, s)
    bias_e = jnp.repeat(b_s.T, gw, axis=1)
    u_specs, su = _window_specs(lay["u"], width, rows, WINDOW_BLOCK)
    v_specs, sv = _window_specs(lay["v"], width, rows, WINDOW_BLOCK)
    kern = functools.partial(_gmlp_kernel, chunks=rows // CHUNK, nu=len(u_specs), nv=len(v_specs), su=su, sv=sv)
    const = lambda shape: pl.BlockSpec(shape, lambda i: (0,) * len(shape))
    return pl.pallas_call(
        kern,
        out_shape=jax.ShapeDtypeStruct((s, width), BF16),
        grid=(s // rows,),
        in_specs=u_specs + v_specs + [const((1, width)), const((1, width)),
                                      const((GMLP_GROUPS, CHUNK, CHUNK)), const((CHUNK, width))],
        out_specs=pl.BlockSpec((rows, width), lambda i: (i, 0)),
        compiler_params=_cparams("parallel"),
        name="gmlp",
    )(*([proj] * (len(u_specs) + len(v_specs))), ln_g.reshape(1, width), ln_b.reshape(1, width), w_s, bias_e)


def _lru_kernel(*refs, ng, nx, sg, sx):
    gate_refs, x_refs = refs[:ng], refs[ng:ng + nx]
    (cw_ref, cb_ref, wa_ref, ba_ref, wx_ref, bx_ref, lam_ref, w1_ref, o_ref, w1p_ref,
     carry_ref, h_ref) = refs[ng + nx:]
    i = pl.program_id(0)

    _w1_prep_kernel(w1_ref, w1p_ref)

    @pl.when(i == 0)
    def _():
        carry_ref[...] = jnp.zeros_like(carry_ref)
        h_ref[...] = jnp.zeros_like(h_ref)

    rows, width = o_ref.shape
    nblk, blk = wa_ref.shape[0], wa_ref.shape[1]
    x_raw = _window_load(x_refs, sx, width)
    xr = _causal_conv(x_raw, carry_ref[...], cw_ref[...], cb_ref[...])
    carry_ref[...] = x_raw[rows - SUBLANES:, :]
    xb = xr.astype(BF16)
    ra = jnp.concatenate([_dot(xb[:, q * blk:(q + 1) * blk], wa_ref[q]) for q in range(nblk)], axis=1)
    ia = jnp.concatenate([_dot(xb[:, q * blk:(q + 1) * blk], wx_ref[q]) for q in range(nblk)], axis=1)
    r = _sigmoid(ra + ba_ref[...])
    ig = _sigmoid(ia + bx_ref[...])
    log_a = (-LRU_C) * r * _softplus(-lam_ref[...])
    a = jnp.exp(log_a)
    b = jnp.sqrt((1.0 - a) * (1.0 + a)) * (ig * xr)
    ridx = lax.broadcasted_iota(I32, (SUBLANES, width), 0)
    h_in = h_ref[...]
    groups = []
    for g in range(rows // SUBLANES):
        gs = slice(g * SUBLANES, (g + 1) * SUBLANES)
        ag, bg = a[gs, :], b[gs, :]
        d = 1
        while d < SUBLANES:
            keep = ridx >= d
            bg = jnp.where(keep, ag * pltpu.roll(bg, d, 0) + bg, bg)
            ag = jnp.where(keep, ag * pltpu.roll(ag, d, 0), ag)
            d *= 2
        hg = ag * h_in + bg
        h_in = hg[SUBLANES - 1:SUBLANES, :]
        groups.append(hg)
    h = jnp.concatenate(groups, axis=0)
    h_ref[...] = h_in
    o_ref[...] = (h * _gelu(_window_load(gate_refs, sg, width))).astype(o_ref.dtype)


def _lru(proj, lay, conv_w, conv_b, w_a, b_a, w_x, b_x, lam, w1, layer, *, rows=128):
    s = proj.shape[0]
    width = lam.shape[0]
    nblk, blk = w_a.shape[0], w_a.shape[1]
    rows = min(rows, s)
    steps = s // rows
    g_specs, sg = _window_specs(lay["gate_c"], width, rows, WINDOW_BLOCK)
    x_specs, sx = _window_specs(lay["x_c"], width, rows, WINDOW_BLOCK)
    _, ne, d, f2 = w1.shape
    gw = 2 * LANES
    groups = f2 // gw
    nw = pl.cdiv(ne * groups, steps)
    assert groups % nw == 0
    per_e = groups // nw
    n_side = ne * per_e

    def side(i):
        h = jnp.minimum(i, n_side - 1)
        return h // per_e, 0, h % per_e

    kern = functools.partial(_lru_kernel, ng=len(g_specs), nx=len(x_specs), sg=sg, sx=sx)
    const = lambda shape: pl.BlockSpec(shape, lambda i: (0,) * len(shape))
    return pl.pallas_call(
        kern,
        out_shape=(jax.ShapeDtypeStruct((s, width), BF16), jax.ShapeDtypeStruct((ne, d, f2), BF16)),
        grid=(steps,),
        in_specs=g_specs + x_specs + [const((conv_w.shape[0], width)), const((1, width)),
                                      const((nblk, blk, blk)), const((1, width)),
                                      const((nblk, blk, blk)), const((1, width)), const((1, width)),
                                      pl.BlockSpec((None, 1, d, nw * gw), lambda i: (layer, *side(i)))],
        out_specs=(pl.BlockSpec((rows, width), lambda i: (i, 0)),
                   pl.BlockSpec((1, d, nw * gw), side)),
        scratch_shapes=[pltpu.VMEM((SUBLANES, width), F32), pltpu.VMEM((1, width), F32)],
        compiler_params=_cparams("arbitrary"),
        name="rglru",
    )(*([proj] * (len(g_specs) + len(x_specs))), conv_w, conv_b.reshape(1, width), w_a.astype(BF16),
      b_a.reshape(1, width), w_x.astype(BF16), b_x.reshape(1, width), lam.reshape(1, width), w1)


def _merge_kernel(*refs, shift, ng):
    y_refs, p_refs = refs[:3], refs[3:6]
    gate_refs = [refs[6 + br * ng:6 + (br + 1) * ng] for br in range(3)]
    b_refs = refs[6 + 3 * ng:9 + 3 * ng]
    o_ref = refs[9 + 3 * ng]
    tn = o_ref.shape[1]
    acc = None
    for br in range(3):
        gate = _sigmoid(_window_load(gate_refs[br], shift, tn) + b_refs[br][...])
        term = gate * _dot(y_refs[br][...], p_refs[br][...])
        acc = term if acc is None else acc + term
    o_ref[...] = acc.astype(o_ref.dtype)


def _merge(y_a, y_b, y_c, p_a, p_b, p_c, proj, lay, b_merge, *, tm=256, tn=1024):
    s, w = y_a.shape
    d = p_a.shape[1]
    tm = min(tm, s)
    tn = min(tn, d)
    g0, shift = divmod(lay["gates"], tn)
    assert shift < LANES
    ng = 2 if shift else 1
    nd = d // tn
    bm = b_merge.reshape(1, 3 * d)
    y_spec = pl.BlockSpec((tm, w), lambda j, i: (i, 0))
    p_spec = pl.BlockSpec((w, tn), lambda j, i: (0, j))
    g_specs = []
    for br in range(3):
        g_specs.append(pl.BlockSpec((tm, tn), lambda j, i, br=br: (i, g0 + br * nd + j)))
        if shift:
            g_specs.append(pl.BlockSpec((tm, LANES), lambda j, i, br=br: (i, (g0 + br * nd + j + 1) * (tn // LANES))))
    b_specs = [pl.BlockSpec((1, tn), lambda j, i, br=br: (0, br * nd + j)) for br in range(3)]
    kern = functools.partial(_merge_kernel, shift=shift, ng=ng)
    return pl.pallas_call(
        kern,
        out_shape=jax.ShapeDtypeStruct((s, d), BF16),
        grid=(d // tn, s // tm),
        in_specs=[y_spec, y_spec, y_spec, p_spec, p_spec, p_spec] + g_specs + b_specs,
        out_specs=pl.BlockSpec((tm, tn), lambda j, i: (i, j)),
        compiler_params=_cparams("parallel", "parallel"),
        name="merge",
    )(y_a, y_b, y_c, p_a, p_b, p_c, *([proj] * (3 * ng)), bm, bm, bm)


def _ln_router_kernel(h_ref, g_ref, b_ref, wr_ref, br_ref, x_ref, xp_ref, te_ref, tw_ref, rk_ref, cnt_ref,
                      carry_ref):
    i = pl.program_id(0)

    @pl.when(i == 0)
    def _():
        carry_ref[...] = jnp.zeros_like(carry_ref)

    rows = h_ref.shape[0]
    y = _ln_rows(h_ref[...], g_ref[...], b_ref[...])
    x_ref[...] = y
    xp_ref[...] = _pack_halves(y)
    yh = y.astype(BF16)
    yl = (y - yh.astype(F32)).astype(BF16)
    wr = wr_ref[...]
    wh = wr.astype(BF16)
    wl = (wr - wh.astype(F32)).astype(BF16)
    logits = _dot(yh, wh) + (_dot(yh, wl) + _dot(yl, wh)) + br_ref[...]
    lane = lax.broadcasted_iota(I32, (rows, LANES), 1)
    lane_f = lane.astype(F32)
    cur = logits
    vals, sels, idxs = [], [], []
    for _ in range(TOP_K):
        m = jnp.max(cur, axis=-1, keepdims=True)
        idx = jnp.min(jnp.where(cur == m, lane_f, float(LANES)), axis=-1, keepdims=True)
        sel = lane_f == idx
        vals.append(m)
        sels.append(sel)
        idxs.append(idx)
        cur = jnp.where(sel, -jnp.inf, cur)
    ex = [jnp.exp(v - vals[0]) for v in vals]
    den = ex[0]
    for e in ex[1:]:
        den = den + e
    multi = jnp.zeros((rows, LANES), F32)
    for sel in sels:
        multi = multi + jnp.where(sel, 1.0, 0.0)
    r2 = lax.broadcasted_iota(I32, (rows, rows), 0)
    c2 = lax.broadcasted_iota(I32, (rows, rows), 1)
    lower = jnp.where(r2 > c2, 1.0, 0.0).astype(BF16)
    before = _dot(lower, multi.astype(BF16)) + carry_ref[...]
    te = jnp.zeros((rows, LANES), F32)
    tw = jnp.zeros((rows, LANES), F32)
    rk = jnp.zeros((rows, LANES), F32)
    for k in range(TOP_K):
        rank_k = jnp.sum(jnp.where(sels[k], before, 0.0), axis=-1, keepdims=True)
        te = jnp.where(lane == k, idxs[k], te)
        tw = jnp.where(lane == k, ex[k] / den, tw)
        rk = jnp.where(lane == k, rank_k, rk)
    te_ref[...] = te.astype(I32)
    tw_ref[...] = tw
    rk_ref[...] = rk.astype(I32)
    carry_ref[...] = carry_ref[...] + jnp.sum(multi, axis=0, keepdims=True)
    cnt_ref[...] = carry_ref[...].astype(I32)


def _ln_router(h, g, b, w_router, b_router, *, tm=256):
    s, d = h.shape
    ne = w_router.shape[1]
    tm = min(tm, s)
    wr = jnp.pad(w_router, ((0, 0), (0, LANES - ne)))
    br = jnp.pad(b_router, (0, LANES - ne), constant_values=NEG_BIG).reshape(1, LANES)
    row = lambda width: pl.BlockSpec((tm, width), lambda i: (i, 0))
    const = lambda shape: pl.BlockSpec(shape, lambda i: (0,) * len(shape))
    return pl.pallas_call(
        _ln_router_kernel,
        out_shape=(jax.ShapeDtypeStruct((s, d), F32), jax.ShapeDtypeStruct((s, d // 2), U32),
                   jax.ShapeDtypeStruct((s, LANES), I32), jax.ShapeDtypeStruct((s, LANES), F32),
                   jax.ShapeDtypeStruct((s, LANES), I32), jax.ShapeDtypeStruct((1, LANES), I32)),
        grid=(s // tm,),
        in_specs=[row(d), const((1, d)), const((1, d)), const((d, LANES)), const((1, LANES))],
        out_specs=(row(d), row(d // 2), row(LANES), row(LANES), row(LANES), const((1, LANES))),
        scratch_shapes=[pltpu.VMEM((1, LANES), F32)],
        compiler_params=_cparams("arbitrary"),
        name="ln_router",
    )(h, g.reshape(1, d), b.reshape(1, d), wr, br)


def _row_copy(src_hbm, dst_hbm, sem, src_row, dst_row):
    return pltpu.make_async_copy(src_hbm.at[pl.ds(src_row, 1)], dst_hbm.at[pl.ds(dst_row, 1)], sem)


def _dispatch_kernel(dest_ref, end_ref, cnt_ref, x_ref, o_hbm, zero_ref, sem, zsem, *, tokens):
    @pl.when(pl.program_id(0) == 0)
    def _():
        zero_ref[...] = jnp.zeros_like(zero_ref)

        def fill(e):
            start = pl.multiple_of(end_ref[e] - EXPERT_ROWS, EXPERT_ROWS)
            return pltpu.make_async_copy(zero_ref, o_hbm.at[pl.ds(start, EXPERT_ROWS)], zsem)

        for e in range(end_ref.shape[0]):
            @pl.when(cnt_ref[e] > 0)
            def _():
                fill(e).start()

        for e in range(end_ref.shape[0]):
            @pl.when(cnt_ref[e] > 0)
            def _():
                fill(e).wait()

        def tail(b):
            return pltpu.make_async_copy(
                zero_ref, o_hbm.at[pl.ds(pl.multiple_of(b * EXPERT_ROWS, EXPERT_ROWS), EXPERT_ROWS)], zsem)

        n_used = end_ref[end_ref.shape[0] - 1] // EXPERT_ROWS
        n_blocks = o_hbm.shape[0] // EXPERT_ROWS
        lax.fori_loop(n_used, n_blocks, lambda b, c: (tail(b).start(), c)[1], 0)
        lax.fori_loop(n_used, n_blocks, lambda b, c: (tail(b).wait(), c)[1], 0)

    def issue(t, c):
        for k in range(TOP_K):
            _row_copy(x_ref, o_hbm, sem, t, dest_ref[0, 0, t * TOP_K + k]).start()
        return c

    lax.fori_loop(0, tokens, issue, 0)

    for _ in range(TOP_K):
        pltpu.make_async_copy(x_ref, o_hbm.at[pl.ds(0, tokens)], sem).wait()


def _dispatch(xp, dest, pad_end, counts, n_rows, *, tokens=256):
    s, w = xp.shape
    tokens = min(tokens, s)
    steps = s // tokens
    kern = functools.partial(_dispatch_kernel, tokens=tokens)
    return pl.pallas_call(
        kern,
        out_shape=jax.ShapeDtypeStruct((n_rows, w), xp.dtype),
        grid=(steps,),
        in_specs=[pl.BlockSpec((1, 1, tokens * TOP_K), lambda i: (i, 0, 0), memory_space=pltpu.SMEM),
                  pl.BlockSpec(memory_space=pltpu.SMEM),
                  pl.BlockSpec(memory_space=pltpu.SMEM),
                  pl.BlockSpec((tokens, w), lambda i: (i, 0))],
        out_specs=pl.BlockSpec(memory_space=pl.ANY),
        scratch_shapes=[pltpu.VMEM((EXPERT_ROWS, w), xp.dtype), pltpu.SemaphoreType.DMA,
                        pltpu.SemaphoreType.DMA],
        compiler_params=_cparams("arbitrary"),
        name="dispatch",
    )(dest.reshape(steps, 1, tokens * TOP_K), pad_end, counts, xp)


def _w1_prep_kernel(w_ref, o_ref):
    gw = 2 * LANES
    c = lax.broadcasted_iota(I32, (gw, gw), 0)
    j = lax.broadcasted_iota(I32, (gw, gw), 1)
    src = jnp.where(j < LANES, 2 * j, 2 * (j - LANES) + 1)
    perm = jnp.where(c == src, 1.0, 0.0).astype(BF16)
    for q in range(w_ref.shape[-1] // gw):
        cs = slice(q * gw, (q + 1) * gw)
        o_ref[0, :, cs] = _dot(w_ref[0, :, cs].astype(BF16), perm).astype(BF16)


def _regroup_bias(b1):
    ne, f2 = b1.shape
    return b1.reshape(ne, f2 // (2 * LANES), LANES, 2).transpose(0, 1, 3, 2).reshape(ne, 1, f2)


def _expert_kernel(be_ref, nu_ref, x_ref, w1_ref, b1_ref, w2_ref, b2_ref, o_ref):
    del be_ref
    used = pl.program_id(0) < nu_ref[0]

    @pl.when(used)
    def _():
        lo, hi = _unpack_halves(x_ref[...])
        half = lo.shape[1]
        gu = _dot(lo, w1_ref[0, :half, :]) + _dot(hi, w1_ref[0, half:, :]) + b1_ref[0]
        acts = []
        for j in range(gu.shape[1] // (2 * LANES)):
            glu = jnp.minimum(gu[:, 2 * j * LANES:(2 * j + 1) * LANES], SWIGLU_LIMIT)
            lin = jnp.clip(gu[:, (2 * j + 1) * LANES:(2 * j + 2) * LANES], -SWIGLU_LIMIT, SWIGLU_LIMIT)
            acts.append(glu * _sigmoid(SWIGLU_ALPHA * glu) * (lin + 1.0))
        act = jnp.concatenate(acts, axis=1)
        o_ref[...] = _pack_halves(_dot(act.astype(BF16), w2_ref[0]) + b2_ref[0])

    @pl.when(jnp.logical_not(used))
    def _():
        o_ref[...] = jnp.zeros_like(o_ref)


def _experts(xs, block_e, n_used, w1p, b1p, w2b, b2):
    n_rows, half = xs.shape
    ne, f, d = w2b.shape
    n_blocks = n_rows // EXPERT_ROWS

    def blk(b, be, nu):
        return jnp.minimum(b, nu[0] - 1)

    grid_spec = pltpu.PrefetchScalarGridSpec(
        num_scalar_prefetch=2,
        grid=(n_blocks,),
        in_specs=[pl.BlockSpec((EXPERT_ROWS, half), lambda b, be, nu: (blk(b, be, nu), 0)),
                  pl.BlockSpec((1, d, 2 * f), lambda b, be, nu: (be[blk(b, be, nu)], 0, 0)),
                  pl.BlockSpec((1, 1, 2 * f), lambda b, be, nu: (be[blk(b, be, nu)], 0, 0)),
                  pl.BlockSpec((1, f, d), lambda b, be, nu: (be[blk(b, be, nu)], 0, 0)),
                  pl.BlockSpec((1, 1, d), lambda b, be, nu: (be[blk(b, be, nu)], 0, 0))],
        out_specs=pl.BlockSpec((EXPERT_ROWS, d // 2), lambda b, be, nu: (b, 0)),
    )
    return pl.pallas_call(
        _expert_kernel,
        out_shape=jax.ShapeDtypeStruct((n_rows, d // 2), U32),
        grid_spec=grid_spec,
        compiler_params=_cparams("arbitrary"),
        name="experts",
    )(block_e, n_used, xs, w1p, b1p, w2b, b2.reshape(ne, 1, d))


def _combine_kernel(dcur_ref, dnxt_ref, tw_ref, x_ref, g_ref, b_ref, ys_hbm, o_ref, ob_ref, buf_ref, sem, *, alpha):
    i = pl.program_id(0)
    tokens, d = x_ref.shape
    half = d // 2
    slot = i % 2

    def copy(s, t, k, src_row):
        return pltpu.make_async_copy(ys_hbm.at[pl.ds(src_row, 1)], buf_ref.at[s, k, pl.ds(t, 1)], sem.at[s])

    def request(d_ref, s):
        def body(t, c):
            for k in range(TOP_K):
                copy(s, t, k, d_ref[0, 0, t * TOP_K + k]).start()
            return c

        lax.fori_loop(0, tokens, body, 0)

    @pl.when(i == 0)
    def _():
        request(dcur_ref, 0)

    @pl.when(i + 1 < pl.num_programs(0))
    def _():
        request(dnxt_ref, 1 - slot)

    for k in range(TOP_K):
        pltpu.make_async_copy(ys_hbm.at[pl.ds(0, tokens)], buf_ref.at[slot, k], sem.at[slot]).wait()
    tw = tw_ref[...]
    x = x_ref[...]
    y_lo = alpha * x[:, :half]
    y_hi = alpha * x[:, half:]
    for k in range(TOP_K):
        u = buf_ref[slot, k]
        wk = tw[:, k:k + 1]
        y_lo = y_lo + wk * pltpu.bitcast(u << 16, F32)
        y_hi = y_hi + wk * pltpu.bitcast(u & jnp.uint32(0xFFFF0000), F32)
    out = _ln_rows(jnp.concatenate([y_lo, y_hi], axis=1), g_ref[...], b_ref[...])
    o_ref[...] = out
    ob_ref[...] = out.astype(BF16)


def _combine(ys, dest, tw, x, g, b, alpha, *, tokens=128):
    s, d = x.shape
    tokens = min(tokens, s)
    steps = s // tokens
    kern = functools.partial(_combine_kernel, alpha=alpha)
    dest3 = dest.reshape(steps, 1, tokens * TOP_K)
    row = lambda width: pl.BlockSpec((tokens, width), lambda i: (i, 0))
    const = lambda shape: pl.BlockSpec(shape, lambda i: (0,) * len(shape))
    return pl.pallas_call(
        kern,
        out_shape=(jax.ShapeDtypeStruct((s, d), F32), jax.ShapeDtypeStruct((s, d), BF16)),
        grid=(steps,),
        in_specs=[pl.BlockSpec((1, 1, tokens * TOP_K), lambda i: (i, 0, 0), memory_space=pltpu.SMEM),
                  pl.BlockSpec((1, 1, tokens * TOP_K), lambda i: (jnp.minimum(i + 1, steps - 1), 0, 0),
                               memory_space=pltpu.SMEM),
                  row(LANES), row(d), const((1, d)), const((1, d)),
                  pl.BlockSpec(memory_space=pl.ANY)],
        out_specs=(row(d), row(d)),
        scratch_shapes=[pltpu.VMEM((2, TOP_K, tokens, d // 2), U32), pltpu.SemaphoreType.DMA((2,))],
        compiler_params=_cparams("arbitrary"),
        name="combine",
    )(dest3, dest3, tw, x, g.reshape(1, d), b.reshape(1, d), ys)


def _mm_rows_kernel(starts_ref, a_ref, wt_ref, *refs, n_casts):
    del starts_ref
    cast_in, o_ref, cast_out, wb_ref = refs[:n_casts], refs[n_casts], refs[n_casts + 1:-1], refs[-1]

    @pl.when(pl.program_id(1) == 0)
    def _():
        wb_ref[...] = wt_ref[0].astype(BF16)

    o_ref[...] = lax.dot_general(a_ref[...], wb_ref[...], (((1,), (1,)), ((), ())),
                                 preferred_element_type=F32).astype(o_ref.dtype)

    for src, dst in zip(cast_in, cast_out):
        dst[...] = src[...].astype(BF16)


def _matmul_rows(a, wt, layer, starts, tn, *, tm, name, casts=()):
    m, k = a.shape
    tm = min(tm, m)
    n_tiles, m_tiles = len(starts), m // tm
    in_specs = [pl.BlockSpec((tm, k), lambda j, i, st: (i, 0)),
                pl.BlockSpec((pl.Element(1), pl.Element(tn), pl.Element(k)),
                             lambda j, i, st: (layer, pl.multiple_of(st[j], SUBLANES), 0))]
    out_specs = [pl.BlockSpec((tm, tn), lambda j, i, st: (i, j))]
    out_shape = [jax.ShapeDtypeStruct((m, n_tiles * tn), F32)]
    args = [jnp.asarray(starts, I32), a, wt]
    for arr, rows, rb in casts:
        nb = rows // rb
        assert rows % rb == 0 and nb <= n_tiles * m_tiles
        cols = arr.shape[1]
        in_specs.append(pl.BlockSpec(
            (rb, cols), lambda j, i, st, nb=nb: (layer * nb + jnp.minimum(j * m_tiles + i, nb - 1), 0)))
        out_specs.append(pl.BlockSpec(
            (rb, cols), lambda j, i, st, nb=nb: (jnp.minimum(j * m_tiles + i, nb - 1), 0)))
        out_shape.append(jax.ShapeDtypeStruct((rows, cols), BF16))
        args.append(arr)
    grid_spec = pltpu.PrefetchScalarGridSpec(
        num_scalar_prefetch=1,
        grid=(n_tiles, m_tiles),
        in_specs=in_specs,
        out_specs=tuple(out_specs),
        scratch_shapes=[pltpu.VMEM((tn, k), BF16)],
    )
    return pl.pallas_call(
        functools.partial(_mm_rows_kernel, n_casts=len(casts)),
        out_shape=tuple(out_shape),
        grid_spec=grid_spec,
        compiler_params=_cparams("arbitrary", "arbitrary", vmem=IN_PROJ_VMEM_LIMIT),
        name=name,
    )(*args)


def _in_proj(xb, w_in, layer, sizes, inner, side_weights):
    z, xbc, dt, u, v, gate_c, x_c, gates = sizes
    src = {"z": 0, "xs": z, "bc": z + inner, "dt": z + xbc}
    src["u"] = src["dt"] + dt
    src["v"] = src["u"] + u
    src["gate_c"] = src["v"] + v
    src["x_c"] = src["gate_c"] + gate_c
    src["gates"] = src["x_c"] + x_c
    width = {"z": z, "xs": inner, "u": u, "v": v, "gate_c": gate_c, "x_c": x_c, "bc": xbc - inner, "gates": gates}
    tn = max(t for t in (1024, 512, 256, 128) if all(w % t == 0 for w in width.values()))
    assert dt <= LANES and all(s % SUBLANES == 0 for s in src.values())
    lay, starts, off = {}, [], 0
    for name in ("z", "xs", "u", "v", "gate_c", "x_c", "bc", "gates"):
        lay[name] = off
        starts += [src[name] + t for t in range(0, width[name], tn)]
        off += width[name]
    wt = jnp.swapaxes(w_in, 1, 2)
    tm = min(512, xb.shape[0])
    m_tiles = xb.shape[0] // tm

    def cast_job(arr, steps):
        rows = math.prod(arr.shape[1:-1])
        rb = 2 * SUBLANES
        while rows // rb > steps or rows % rb:
            rb *= 2
        return arr.reshape(-1, arr.shape[-1]), rows, rb

    proj, big_copy = _matmul_rows(xb, wt, layer, starts, tn, tm=tm, name="in_proj",
                                  casts=[cast_job(side_weights[0], len(starts) * m_tiles)])
    dt_raw, *copies = _matmul_rows(xb, wt, layer, [src["dt"]], LANES, tm=tm, name="dt_proj",
                                   casts=[cast_job(w, m_tiles) for w in side_weights[1:]])
    return proj, dt_raw, lay, [big_copy] + copies


def _moe_plan(te, rk, cnt, ne, n_blocks):
    counts = cnt[0, :ne]
    padded = (counts + EXPERT_ROWS - 1) // EXPERT_ROWS * EXPERT_ROWS
    pad_end = jnp.cumsum(padded)
    pad_start = pad_end - padded
    picked = te[:, :TOP_K, None] == jnp.arange(ne, dtype=I32)[None, None, :]
    dest = (jnp.sum(jnp.where(picked, pad_start[None, None, :], 0), axis=-1) + rk[:, :TOP_K]).astype(I32)
    dest = dest.reshape(-1)
    block_start = jnp.arange(n_blocks, dtype=I32) * EXPERT_ROWS
    block_e = jnp.sum(pad_end[None, :] <= block_start[:, None], axis=1)
    block_e = jnp.minimum(block_e, ne - 1).astype(I32)
    n_used = (pad_end[-1] // EXPERT_ROWS).astype(I32).reshape(1)
    return dest, block_e, n_used, pad_end.astype(I32), counts.astype(I32)


def kernel(x, ln_emb_g, ln_emb_b, w_in, b_merge, conv_a_w, conv_a_b, dt_bias, a_log, d_skip, norm_a_g,
           ln_v_g, ln_v_b, w_spatial, b_spatial, conv_c_w, conv_c_b, w_rg_a, b_rg_a, w_rg_x, b_rg_x, lam,
           p_a, p_b, p_c, w_o, ln_mix_g, ln_mix_b, w_router, b_router, w1, b1, w2, b2, ln_ffn_g, ln_ffn_b):
    bsz, seq, d = x.shape
    depth = w_in.shape[0]
    alpha = (2 * depth) ** 0.25
    heads = dt_bias.shape[1]
    inner = norm_a_g.shape[1]
    sizes = (inner, conv_a_w.shape[2], heads, ln_v_g.shape[1], ln_v_g.shape[1], lam.shape[1], lam.shape[1],
             3 * d)
    ne = w_router.shape[2]
    s = bsz * seq
    n_rows = s * TOP_K + ne * EXPERT_ROWS
    n_blocks = n_rows // EXPERT_ROWS

    xf, xb = _ln0(x.reshape(s, d), ln_emb_g, ln_emb_b)
    for l in range(depth):
        proj, dt_raw, lay, (w2b, pab, pbb, pcb) = _in_proj(xb, w_in, l, sizes, inner, (w2, p_a, p_b, p_c))
        y_a = _ssd(proj, dt_raw, lay, conv_a_w[l], conv_a_b[l], dt_bias[l], a_log[l], d_skip[l], norm_a_g[l])
        y_b = _gmlp(proj, lay, ln_v_g[l], ln_v_b[l], w_spatial[l], b_spatial[l])
        y_c, w1p = _lru(proj, lay, conv_c_w[l], conv_c_b[l], w_rg_a[l], b_rg_a[l], w_rg_x[l], b_rg_x[l], lam[l],
                        w1, l)
        merged = _merge(y_a, y_b, y_c, pab, pbb, pcb, proj, lay, b_merge[l])
        h = _matmul(merged, w_o, l, F32, tm=1024, tn=512, res=xf, alpha=alpha, name="w_o")
        xm, xp, te, tw, rk, cnt = _ln_router(h, ln_mix_g[l], ln_mix_b[l], w_router[l], b_router[l])
        dest, block_e, n_used, pad_end, counts = _moe_plan(te, rk, cnt, ne, n_blocks)
        xs = _dispatch(xp, dest, pad_end, counts, n_rows)
        ys = _experts(xs, block_e, n_used, w1p, _regroup_bias(b1[l]), w2b.reshape(w2.shape[1:]), b2[l])
        xf, xb = _combine(ys, dest, tw, xm, ln_ffn_g[l], ln_ffn_b[l], alpha)
    return xf.reshape(bsz, seq, d)
```

```python
import functools
import math

import jax
import jax.numpy as jnp
from jax import lax
from jax.experimental import pallas as pl
from jax.experimental.pallas import tpu as pltpu

F32 = jnp.float32
BF16 = jnp.bfloat16
U32 = jnp.uint32
I32 = jnp.int32

SSD_GROUPS = 4
SSD_STATE = 128
CHUNK = 128
GMLP_GROUPS = 8
LRU_C = 8.0
TOP_K = 4
SWIGLU_LIMIT = 7.0
SWIGLU_ALPHA = 1.702
LN_EPS = 1e-5
RMS_EPS = 1e-5

LANES = 128
SUBLANES = 8
VMEM_LIMIT = 56 * 1024 * 1024
BIG_TILE_VMEM_LIMIT = 60 * 1024 * 1024
EXPERT_ROWS = 256
WINDOW_BLOCK = 1024
NEG_BIG = -1e30


def _cparams(*sem, vmem=VMEM_LIMIT):
    return pltpu.CompilerParams(dimension_semantics=sem, vmem_limit_bytes=vmem)


def _dot(a, b):
    return jnp.dot(a, b, preferred_element_type=F32)


def _split3(x):
    hi = x.astype(BF16)
    r = x - hi.astype(F32)
    mid = r.astype(BF16)
    lo = (r - mid.astype(F32)).astype(BF16)
    return hi, mid, lo


def _exact_right(x, m):
    hi, mid, lo = _split3(x)
    return _dot(hi, m) + (_dot(mid, m) + _dot(lo, m))


def _exact_left(m, x):
    hi, mid, lo = _split3(x)
    return _dot(m, hi) + (_dot(m, mid) + _dot(m, lo))


def _sigmoid(x):
    return 1.0 / (1.0 + jnp.exp(-x))


def _silu(x):
    return x * _sigmoid(x)


def _softplus(x):
    return jnp.maximum(x, 0.0) + jnp.log1p(jnp.exp(-jnp.abs(x)))


def _gelu(x):
    c = math.sqrt(2.0 / math.pi)
    return x * (0.5 * (1.0 + jnp.tanh(c * (x + 0.044715 * (x * x * x)))))


def _ln_rows(x, g, b):
    mu = jnp.mean(x, axis=-1, keepdims=True)
    xc = x - mu
    var = jnp.mean(xc * xc, axis=-1, keepdims=True)
    return xc * lax.rsqrt(var + LN_EPS) * g + b


def _causal_conv(x, carry, w, b):
    kw = w.shape[0]
    rows = x.shape[0]
    xp = jnp.concatenate([carry, x], axis=0)
    y = x * w[kw - 1:kw, :] + b
    for k in range(kw - 1):
        s = SUBLANES - (kw - 1) + k
        y = y + xp[s:s + rows, :] * w[k:k + 1, :]
    return y


def _pack_halves(y):
    half = y.shape[1] // 2
    lo = pltpu.bitcast(y[:, :half].astype(BF16).astype(F32), U32)
    hi = pltpu.bitcast(y[:, half:].astype(BF16).astype(F32), U32)
    return (hi & jnp.uint32(0xFFFF0000)) | (lo >> 16)


def _unpack_halves(u):
    lo = pltpu.bitcast(u << 16, F32).astype(BF16)
    hi = pltpu.bitcast(u & jnp.uint32(0xFFFF0000), F32).astype(BF16)
    return lo, hi


def _window_specs(off, width, rows, blk):
    first, shift = divmod(off, blk)
    nblk = pl.cdiv(shift + width, blk)
    specs = [pl.BlockSpec((rows, blk), lambda i, q=q: (i, first + q)) for q in range(nblk)]
    return specs, shift


def _window_load(refs, shift, width, rows=slice(None)):
    blk = refs[0].shape[1]
    first, lane_shift = divmod(shift, LANES)
    n_out = width // LANES

    def lane_block(k):
        q, o = divmod(k * LANES, blk)
        return refs[q][rows, o:o + LANES].astype(F32)

    if lane_shift == 0:
        return jnp.concatenate([lane_block(first + k) for k in range(n_out)], axis=1)
    rolled = [pltpu.roll(lane_block(first + k), LANES - lane_shift, 1) for k in range(n_out + 1)]
    lane = lax.broadcasted_iota(I32, rolled[0].shape, 1)
    outs = [jnp.where(lane < LANES - lane_shift, rolled[k], rolled[k + 1]) for k in range(n_out)]
    return jnp.concatenate(outs, axis=1)


def _ln0_kernel(x_ref, g_ref, b_ref, o_ref, ob_ref):
    y = _ln_rows(x_ref[...], g_ref[...], b_ref[...])
    o_ref[...] = y
    ob_ref[...] = y.astype(BF16)


def _ln0(x, g, b):
    s, d = x.shape
    tm = min(256, s)
    return pl.pallas_call(
        _ln0_kernel,
        out_shape=(jax.ShapeDtypeStruct((s, d), F32), jax.ShapeDtypeStruct((s, d), BF16)),
        grid=(s // tm,),
        in_specs=[pl.BlockSpec((tm, d), lambda i: (i, 0)),
                  pl.BlockSpec((1, d), lambda i: (0, 0)),
                  pl.BlockSpec((1, d), lambda i: (0, 0))],
        out_specs=(pl.BlockSpec((tm, d), lambda i: (i, 0)),
                   pl.BlockSpec((tm, d), lambda i: (i, 0))),
        compiler_params=_cparams("parallel"),
        name="ln0",
    )(x, g.reshape(1, d), b.reshape(1, d))


def _cast_weight_tile(w_ref, wb_ref, n_valid):
    tn = w_ref.shape[1]
    col = pl.program_id(0) * tn + lax.broadcasted_iota(I32, w_ref.shape, 1)
    wb_ref[...] = jnp.where(col < n_valid, w_ref[...], 0.0).astype(BF16)


def _mm_kernel(a_ref, w_ref, o_ref, wb_ref, *, n_valid):
    @pl.when(pl.program_id(1) == 0)
    def _():
        _cast_weight_tile(w_ref, wb_ref, n_valid)

    o_ref[...] = _dot(a_ref[...], wb_ref[...]).astype(o_ref.dtype)


def _mm_res_kernel(a_ref, w_ref, r_ref, o_ref, wb_ref, *, n_valid, alpha):
    @pl.when(pl.program_id(1) == 0)
    def _():
        _cast_weight_tile(w_ref, wb_ref, n_valid)

    o_ref[...] = (alpha * r_ref[...] + _dot(a_ref[...], wb_ref[...])).astype(o_ref.dtype)


def _matmul(a, w, layer, out_dtype, *, tm, tn, res=None, alpha=1.0, name="mm"):
    m, k = a.shape
    n = w.shape[2]
    tm = min(tm, m)
    tn = min(tn, n)
    n_tiles = pl.cdiv(n, tn)
    in_specs = [pl.BlockSpec((tm, k), lambda j, i: (i, 0)),
                pl.BlockSpec((None, k, tn), lambda j, i: (layer, 0, j))]
    args = [a, w]
    if res is None:
        kern = functools.partial(_mm_kernel, n_valid=n)
    else:
        kern = functools.partial(_mm_res_kernel, n_valid=n, alpha=alpha)
        in_specs.append(pl.BlockSpec((tm, tn), lambda j, i: (i, j)))
        args.append(res)
    return pl.pallas_call(
        kern,
        out_shape=jax.ShapeDtypeStruct((m, n_tiles * tn), out_dtype),
        grid=(n_tiles, m // tm),
        in_specs=in_specs,
        out_specs=pl.BlockSpec((tm, tn), lambda j, i: (i, j)),
        scratch_shapes=[pltpu.VMEM((k, tn), BF16)],
        compiler_params=_cparams("parallel", "arbitrary", vmem=BIG_TILE_VMEM_LIMIT),
        name=name,
    )(*args)


def _ssd_kernel(z_ref, xs_ref, bc_ref, dt_ref, cwx_ref, cbx_ref, cwb_ref, cbb_ref, dtb_ref, alog_ref,
                dskip_ref, ng_ref, e_ref, o_ref, cx_ref, cb_ref, st_ref, *, heads, head_dim):
    i = pl.program_id(0)

    @pl.when(i == 0)
    def _():
        cx_ref[...] = jnp.zeros_like(cx_ref)
        cb_ref[...] = jnp.zeros_like(cb_ref)
        st_ref[...] = jnp.zeros_like(st_ref)

    n = SSD_STATE
    gw = (heads // SSD_GROUPS) * head_dim
    hpg = heads // SSD_GROUPS
    pair = LANES // head_dim

    xs_raw = xs_ref[...].astype(F32)
    bc_raw = bc_ref[...].astype(F32)
    xs = _silu(_causal_conv(xs_raw, cx_ref[...], cwx_ref[...], cbx_ref[...]))
    bc = _silu(_causal_conv(bc_raw, cb_ref[...], cwb_ref[...], cbb_ref[...]))
    cx_ref[...] = xs_raw[CHUNK - SUBLANES:, :]
    cb_ref[...] = bc_raw[CHUNK - SUBLANES:, :]

    row = lax.broadcasted_iota(I32, (CHUNK, CHUNK), 0)
    col = lax.broadcasted_iota(I32, (CHUNK, CHUNK), 1)
    dt = _softplus(jnp.where(col < heads, dt_ref[...], 0.0) + dtb_ref[...])
    da = dt * (-jnp.exp(alog_ref[...]))
    causal = row >= col
    ltri = jnp.where(causal, 1.0, 0.0).astype(BF16)
    cum = _exact_left(ltri, da)
    cum_t = cum.T
    expand = e_ref[...]
    cum_e = _exact_right(cum, expand)
    dt_e = _exact_right(dt, expand)
    last_e = cum_e[CHUNK - 1:CHUNK, :]
    decay_out = jnp.exp(cum_e)
    decay_st = jnp.exp(last_e - cum_e)
    chunk_decay = jnp.exp(last_e)
    xdt = xs * dt_e
    xdt_b = xdt.astype(BF16)
    xst_b = (xdt * decay_st).astype(BF16)
    lane = lax.broadcasted_iota(I32, (CHUNK, LANES), 1)

    for g in range(SSD_GROUPS):
        bg = bc[:, g * n:(g + 1) * n].astype(BF16)
        cg = bc[:, (SSD_GROUPS + g) * n:(SSD_GROUPS + g + 1) * n].astype(BF16)
        cb = lax.dot_general(cg, bg, (((1,), (1,)), ((), ())), preferred_element_type=F32)
        slabs = []
        for j in range(hpg // pair):
            h0 = g * hpg + j * pair
            sc = []
            for h in range(h0, h0 + pair):
                seg = cum[:, h:h + 1] - cum_t[h:h + 1, :]
                dec = jnp.exp(jnp.where(causal, seg, -jnp.inf))
                sc.append((cb * dec).astype(BF16))
            xp = xdt_b[:, h0 * head_dim:h0 * head_dim + LANES]
            blocks = [jnp.where((lane >= q * head_dim) & (lane < (q + 1) * head_dim), xp, jnp.zeros_like(xp))
                      for q in range(pair)]
            slabs.append(_dot(jnp.concatenate(sc, axis=1), jnp.concatenate(blocks, axis=0)))
        y_diag = jnp.concatenate(slabs, axis=1)
        cs = slice(g * gw, (g + 1) * gw)
        st = st_ref[g]
        y_off = _dot(cg, st.astype(BF16)) * decay_out[:, cs]
        st_new = lax.dot_general(bg, xst_b[:, cs], (((0,), (0,)), ((), ())), preferred_element_type=F32)
        st_ref[g] = st * chunk_decay[:, cs] + st_new
        y = y_diag + y_off + xs[:, cs] * dskip_ref[:, cs]
        yg = y * _silu(z_ref[:, cs].astype(F32))
        ms = jnp.mean(yg * yg, axis=-1, keepdims=True)
        o_ref[:, cs] = (yg * lax.rsqrt(ms + RMS_EPS) * ng_ref[:, cs]).astype(o_ref.dtype)


def _ssd(proj, dt_raw, lay, conv_w, conv_b, dt_bias, a_log, d_skip, norm_g):
    s = proj.shape[0]
    heads = dt_bias.shape[0]
    inner = norm_g.shape[0]
    head_dim = inner // heads
    bcw = 2 * SSD_GROUPS * SSD_STATE
    gw = inner // SSD_GROUPS
    pad = LANES - heads
    assert lay["z"] % inner == 0 and lay["xs"] % inner == 0 and lay["bc"] % bcw == 0
    expand = (jnp.arange(LANES)[:, None] == (jnp.arange(inner) // head_dim)[None, :]).astype(BF16)
    kern = functools.partial(_ssd_kernel, heads=heads, head_dim=head_dim)
    const = lambda shape: pl.BlockSpec(shape, lambda i: (0,) * len(shape))
    return pl.pallas_call(
        kern,
        out_shape=jax.ShapeDtypeStruct((s, inner), BF16),
        grid=(s // CHUNK,),
        in_specs=[pl.BlockSpec((CHUNK, inner), lambda i: (i, lay["z"] // inner)),
                  pl.BlockSpec((CHUNK, inner), lambda i: (i, lay["xs"] // inner)),
                  pl.BlockSpec((CHUNK, bcw), lambda i: (i, lay["bc"] // bcw)),
                  pl.BlockSpec((CHUNK, LANES), lambda i: (i, 0)),
                  const((conv_w.shape[0], inner)), const((1, inner)),
                  const((conv_w.shape[0], bcw)), const((1, bcw)),
                  const((1, LANES)), const((1, LANES)),
                  const((1, inner)), const((1, inner)), const((LANES, inner))],
        out_specs=pl.BlockSpec((CHUNK, inner), lambda i: (i, 0)),
        scratch_shapes=[pltpu.VMEM((SUBLANES, inner), F32), pltpu.VMEM((SUBLANES, bcw), F32),
                        pltpu.VMEM((SSD_GROUPS, SSD_STATE, gw), F32)],
        compiler_params=_cparams("arbitrary"),
        name="ssd",
    )(proj, proj, proj, dt_raw,
      conv_w[:, :inner], conv_b[:inner].reshape(1, inner),
      conv_w[:, inner:], conv_b[inner:].reshape(1, bcw),
      jnp.pad(dt_bias, (0, pad)).reshape(1, LANES), jnp.pad(a_log, (0, pad)).reshape(1, LANES),
      jnp.repeat(d_skip, head_dim).reshape(1, inner), norm_g.reshape(1, inner), expand)


def _gmlp_kernel(*refs, chunks, nu, nv, su, sv):
    u_refs, v_refs = refs[:nu], refs[nu:nu + nv]
    lg_ref, lb_ref, ws_ref, be_ref, o_ref = refs[nu + nv:]
    width = o_ref.shape[1]
    gw = width // GMLP_GROUPS
    row = lax.broadcasted_iota(I32, (CHUNK, CHUNK), 0)
    col = lax.broadcasted_iota(I32, (CHUNK, CHUNK), 1)
    tril = row >= col
    ws = [jnp.where(tril, ws_ref[g], 0.0).astype(BF16) for g in range(GMLP_GROUPS)]
    for c in range(chunks):
        rs = slice(c * CHUNK, (c + 1) * CHUNK)
        v = _ln_rows(_gelu(_window_load(v_refs, sv, width, rs)), lg_ref[...], lb_ref[...]).astype(BF16)
        u = _gelu(_window_load(u_refs, su, width, rs))
        for g in range(GMLP_GROUPS):
            cs = slice(g * gw, (g + 1) * gw)
            mixed = _dot(ws[g], v[:, cs]) + be_ref[:, cs]
            o_ref[rs, cs] = (u[:, cs] * mixed).astype(o_ref.dtype)


def _gmlp(proj, lay, ln_g, ln_b, w_s, b_s, *, rows=256):
    s = proj.shape[0]
    width = ln_g.shape[0]
    gw = width // GMLP_GROUPS
    rows = min(rows, s)
    bias_e = jnp.repeat(b_s.T, gw, axis=1)
    u_specs, su = _window_specs(lay["u"], width, rows, WINDOW_BLOCK)
    v_specs, sv = _window_specs(lay["v"], width, rows, WINDOW_BLOCK)
    kern = functools.partial(_gmlp_kernel, chunks=rows // CHUNK, nu=len(u_specs), nv=len(v_specs), su=su, sv=sv)
    const = lambda shape: pl.BlockSpec(shape, lambda i: (0,) * len(shape))
    return pl.pallas_call(
        kern,
        out_shape=jax.ShapeDtypeStruct((s, width), BF16),
        grid=(s // rows,),
        in_specs=u_specs + v_specs + [const((1, width)), const((1, width)),
                                      const((GMLP_GROUPS, CHUNK, CHUNK)), const((CHUNK, width))],
        out_specs=pl.BlockSpec((rows, width), lambda i: (i, 0)),
        compiler_params=_cparams("parallel"),
        name="gmlp",
    )(*([proj] * (len(u_specs) + len(v_specs))), ln_g.reshape(1, width), ln_b.reshape(1, width), w_s, bias_e)


def _lru_kernel(*refs, ng, nx, sg, sx):
    gate_refs, x_refs = refs[:ng], refs[ng:ng + nx]
    (cw_ref, cb_ref, wa_ref, ba_ref, wx_ref, bx_ref, lam_ref, w1_ref, o_ref, w1p_ref,
     carry_ref, h_ref) = refs[ng + nx:]
    i = pl.program_id(0)

    _w1_prep_kernel(w1_ref, w1p_ref)

    @pl.when(i == 0)
    def _():
        carry_ref[...] = jnp.zeros_like(carry_ref)
        h_ref[...] = jnp.zeros_like(h_ref)

    rows, width = o_ref.shape
    nblk, blk = wa_ref.shape[0], wa_ref.shape[1]
    x_raw = _window_load(x_refs, sx, width)
    xr = _causal_conv(x_raw, carry_ref[...], cw_ref[...], cb_ref[...])
    carry_ref[...] = x_raw[rows - SUBLANES:, :]
    xb = xr.astype(BF16)
    ra = jnp.concatenate([_dot(xb[:, q * blk:(q + 1) * blk], wa_ref[q]) for q in range(nblk)], axis=1)
    ia = jnp.concatenate([_dot(xb[:, q * blk:(q + 1) * blk], wx_ref[q]) for q in range(nblk)], axis=1)
    r = _sigmoid(ra + ba_ref[...])
    ig = _sigmoid(ia + bx_ref[...])
    log_a = (-LRU_C) * r * _softplus(-lam_ref[...])
    a = jnp.exp(log_a)
    b = jnp.sqrt((1.0 - a) * (1.0 + a)) * (ig * xr)
    ridx = lax.broadcasted_iota(I32, (SUBLANES, width), 0)
    h_in = h_ref[...]
    groups = []
    for g in range(rows // SUBLANES):
        gs = slice(g * SUBLANES, (g + 1) * SUBLANES)
        ag, bg = a[gs, :], b[gs, :]
        d = 1
        while d < SUBLANES:
            keep = ridx >= d
            bg = jnp.where(keep, ag * pltpu.roll(bg, d, 0) + bg, bg)
            ag = jnp.where(keep, ag * pltpu.roll(ag, d, 0), ag)
            d *= 2
        hg = ag * h_in + bg
        h_in = hg[SUBLANES - 1:SUBLANES, :]
        groups.append(hg)
    h = jnp.concatenate(groups, axis=0)
    h_ref[...] = h_in
    o_ref[...] = (h * _gelu(_window_load(gate_refs, sg, width))).astype(o_ref.dtype)


def _lru(proj, lay, conv_w, conv_b, w_a, b_a, w_x, b_x, lam, w1, layer, *, rows=128):
    s = proj.shape[0]
    width = lam.shape[0]
    nblk, blk = w_a.shape[0], w_a.shape[1]
    rows = min(rows, s)
    steps = s // rows
    g_specs, sg = _window_specs(lay["gate_c"], width, rows, WINDOW_BLOCK)
    x_specs, sx = _window_specs(lay["x_c"], width, rows, WINDOW_BLOCK)
    _, ne, d, f2 = w1.shape
    gw = 2 * LANES
    groups = f2 // gw
    nw = pl.cdiv(ne * groups, steps)
    assert groups % nw == 0
    per_e = groups // nw
    n_side = ne * per_e

    def side(i):
        h = jnp.minimum(i, n_side - 1)
        return h // per_e, 0, h % per_e

    kern = functools.partial(_lru_kernel, ng=len(g_specs), nx=len(x_specs), sg=sg, sx=sx)
    const = lambda shape: pl.BlockSpec(shape, lambda i: (0,) * len(shape))
    return pl.pallas_call(
        kern,
        out_shape=(jax.ShapeDtypeStruct((s, width), BF16), jax.ShapeDtypeStruct((ne, d, f2), BF16)),
        grid=(steps,),
        in_specs=g_specs + x_specs + [const((conv_w.shape[0], width)), const((1, width)),
                                      const((nblk, blk, blk)), const((1, width)),
                                      const((nblk, blk, blk)), const((1, width)), const((1, width)),
                                      pl.BlockSpec((None, 1, d, nw * gw), lambda i: (layer, *side(i)))],
        out_specs=(pl.BlockSpec((rows, width), lambda i: (i, 0)),
                   pl.BlockSpec((1, d, nw * gw), side)),
        scratch_shapes=[pltpu.VMEM((SUBLANES, width), F32), pltpu.VMEM((1, width), F32)],
        compiler_params=_cparams("arbitrary"),
        name="rglru",
    )(*([proj] * (len(g_specs) + len(x_specs))), conv_w, conv_b.reshape(1, width), w_a.astype(BF16),
      b_a.reshape(1, width), w_x.astype(BF16), b_x.reshape(1, width), lam.reshape(1, width), w1)


def _merge_kernel(*refs, shift, ng):
    y_refs, p_refs = refs[:3], refs[3:6]
    gate_refs = [refs[6 + br * ng:6 + (br + 1) * ng] for br in range(3)]
    b_refs = refs[6 + 3 * ng:9 + 3 * ng]
    o_ref = refs[9 + 3 * ng]
    tn = o_ref.shape[1]
    acc = None
    for br in range(3):
        gate = _sigmoid(_window_load(gate_refs[br], shift, tn) + b_refs[br][...])
        term = gate * _dot(y_refs[br][...], p_refs[br][...])
        acc = term if acc is None else acc + term
    o_ref[...] = acc.astype(o_ref.dtype)


def _merge(y_a, y_b, y_c, p_a, p_b, p_c, proj, lay, b_merge, *, tm=512, tn=1024):
    s, w = y_a.shape
    d = p_a.shape[1]
    tm = min(tm, s)
    tn = min(tn, d)
    g0, shift = divmod(lay["gates"], tn)
    assert shift < LANES
    ng = 2 if shift else 1
    nd = d // tn
    bm = b_merge.reshape(1, 3 * d)
    y_spec = pl.BlockSpec((tm, w), lambda j, i: (i, 0))
    p_spec = pl.BlockSpec((w, tn), lambda j, i: (0, j))
    g_specs = []
    for br in range(3):
        g_specs.append(pl.BlockSpec((tm, tn), lambda j, i, br=br: (i, g0 + br * nd + j)))
        if shift:
            g_specs.append(pl.BlockSpec((tm, LANES), lambda j, i, br=br: (i, (g0 + br * nd + j + 1) * (tn // LANES))))
    b_specs = [pl.BlockSpec((1, tn), lambda j, i, br=br: (0, br * nd + j)) for br in range(3)]
    kern = functools.partial(_merge_kernel, shift=shift, ng=ng)
    return pl.pallas_call(
        kern,
        out_shape=jax.ShapeDtypeStruct((s, d), BF16),
        grid=(d // tn, s // tm),
        in_specs=[y_spec, y_spec, y_spec, p_spec, p_spec, p_spec] + g_specs + b_specs,
        out_specs=pl.BlockSpec((tm, tn), lambda j, i: (i, j)),
        compiler_params=_cparams("parallel", "parallel", vmem=BIG_TILE_VMEM_LIMIT),
        name="merge",
    )(y_a, y_b, y_c, p_a, p_b, p_c, *([proj] * (3 * ng)), bm, bm, bm)


def _ln_router_kernel(h_ref, g_ref, b_ref, wr_ref, br_ref, x_ref, xp_ref, te_ref, tw_ref, rk_ref, cnt_ref,
                      carry_ref):
    i = pl.program_id(0)

    @pl.when(i == 0)
    def _():
        carry_ref[...] = jnp.zeros_like(carry_ref)

    rows = h_ref.shape[0]
    y = _ln_rows(h_ref[...], g_ref[...], b_ref[...])
    x_ref[...] = y
    xp_ref[...] = _pack_halves(y)
    yh = y.astype(BF16)
    yl = (y - yh.astype(F32)).astype(BF16)
    wr = wr_ref[...]
    wh = wr.astype(BF16)
    wl = (wr - wh.astype(F32)).astype(BF16)
    logits = _dot(yh, wh) + (_dot(yh, wl) + _dot(yl, wh)) + br_ref[...]
    lane = lax.broadcasted_iota(I32, (rows, LANES), 1)
    lane_f = lane.astype(F32)
    cur = logits
    vals, sels, idxs = [], [], []
    for _ in range(TOP_K):
        m = jnp.max(cur, axis=-1, keepdims=True)
        idx = jnp.min(jnp.where(cur == m, lane_f, float(LANES)), axis=-1, keepdims=True)
        sel = lane_f == idx
        vals.append(m)
        sels.append(sel)
        idxs.append(idx)
        cur = jnp.where(sel, -jnp.inf, cur)
    ex = [jnp.exp(v - vals[0]) for v in vals]
    den = ex[0]
    for e in ex[1:]:
        den = den + e
    multi = jnp.zeros((rows, LANES), F32)
    for sel in sels:
        multi = multi + jnp.where(sel, 1.0, 0.0)
    r2 = lax.broadcasted_iota(I32, (rows, rows), 0)
    c2 = lax.broadcasted_iota(I32, (rows, rows), 1)
    lower = jnp.where(r2 > c2, 1.0, 0.0).astype(BF16)
    before = _dot(lower, multi.astype(BF16)) + carry_ref[...]
    te = jnp.zeros((rows, LANES), F32)
    tw = jnp.zeros((rows, LANES), F32)
    rk = jnp.zeros((rows, LANES), F32)
    for k in range(TOP_K):
        rank_k = jnp.sum(jnp.where(sels[k], before, 0.0), axis=-1, keepdims=True)
        te = jnp.where(lane == k, idxs[k], te)
        tw = jnp.where(lane == k, ex[k] / den, tw)
        rk = jnp.where(lane == k, rank_k, rk)
    te_ref[...] = te.astype(I32)
    tw_ref[...] = tw
    rk_ref[...] = rk.astype(I32)
    carry_ref[...] = carry_ref[...] + jnp.sum(multi, axis=0, keepdims=True)
    cnt_ref[...] = carry_ref[...].astype(I32)


def _ln_router(h, g, b, w_router, b_router, *, tm=256):
    s, d = h.shape
    ne = w_router.shape[1]
    tm = min(tm, s)
    wr = jnp.pad(w_router, ((0, 0), (0, LANES - ne)))
    br = jnp.pad(b_router, (0, LANES - ne), constant_values=NEG_BIG).reshape(1, LANES)
    row = lambda width: pl.BlockSpec((tm, width), lambda i: (i, 0))
    const = lambda shape: pl.BlockSpec(shape, lambda i: (0,) * len(shape))
    return pl.pallas_call(
        _ln_router_kernel,
        out_shape=(jax.ShapeDtypeStruct((s, d), F32), jax.ShapeDtypeStruct((s, d // 2), U32),
                   jax.ShapeDtypeStruct((s, LANES), I32), jax.ShapeDtypeStruct((s, LANES), F32),
                   jax.ShapeDtypeStruct((s, LANES), I32), jax.ShapeDtypeStruct((1, LANES), I32)),
        grid=(s // tm,),
        in_specs=[row(d), const((1, d)), const((1, d)), const((d, LANES)), const((1, LANES))],
        out_specs=(row(d), row(d // 2), row(LANES), row(LANES), row(LANES), const((1, LANES))),
        scratch_shapes=[pltpu.VMEM((1, LANES), F32)],
        compiler_params=_cparams("arbitrary"),
        name="ln_router",
    )(h, g.reshape(1, d), b.reshape(1, d), wr, br)


def _row_copy(src_hbm, dst_hbm, sem, src_row, dst_row):
    return pltpu.make_async_copy(src_hbm.at[pl.ds(src_row, 1)], dst_hbm.at[pl.ds(dst_row, 1)], sem)


def _dispatch_kernel(dest_ref, end_ref, cnt_ref, x_ref, o_hbm, zero_ref, sem, zsem, *, tokens):
    @pl.when(pl.program_id(0) == 0)
    def _():
        zero_ref[...] = jnp.zeros_like(zero_ref)

        def fill(e):
            start = pl.multiple_of(end_ref[e] - EXPERT_ROWS, EXPERT_ROWS)
            return pltpu.make_async_copy(zero_ref, o_hbm.at[pl.ds(start, EXPERT_ROWS)], zsem)

        for e in range(end_ref.shape[0]):
            @pl.when(cnt_ref[e] > 0)
            def _():
                fill(e).start()

        for e in range(end_ref.shape[0]):
            @pl.when(cnt_ref[e] > 0)
            def _():
                fill(e).wait()

        def tail(b):
            return pltpu.make_async_copy(
                zero_ref, o_hbm.at[pl.ds(pl.multiple_of(b * EXPERT_ROWS, EXPERT_ROWS), EXPERT_ROWS)], zsem)

        n_used = end_ref[end_ref.shape[0] - 1] // EXPERT_ROWS
        n_blocks = o_hbm.shape[0] // EXPERT_ROWS
        lax.fori_loop(n_used, n_blocks, lambda b, c: (tail(b).start(), c)[1], 0)
        lax.fori_loop(n_used, n_blocks, lambda b, c: (tail(b).wait(), c)[1], 0)

    def issue(t, c):
        for k in range(TOP_K):
            _row_copy(x_ref, o_hbm, sem, t, dest_ref[0, 0, t * TOP_K + k]).start()
        return c

    lax.fori_loop(0, tokens, issue, 0)

    for _ in range(TOP_K):
        pltpu.make_async_copy(x_ref, o_hbm.at[pl.ds(0, tokens)], sem).wait()


def _dispatch(xp, dest, pad_end, counts, n_rows, *, tokens=256):
    s, w = xp.shape
    tokens = min(tokens, s)
    steps = s // tokens
    kern = functools.partial(_dispatch_kernel, tokens=tokens)
    return pl.pallas_call(
        kern,
        out_shape=jax.ShapeDtypeStruct((n_rows, w), xp.dtype),
        grid=(steps,),
        in_specs=[pl.BlockSpec((1, 1, tokens * TOP_K), lambda i: (i, 0, 0), memory_space=pltpu.SMEM),
                  pl.BlockSpec(memory_space=pltpu.SMEM),
                  pl.BlockSpec(memory_space=pltpu.SMEM),
                  pl.BlockSpec((tokens, w), lambda i: (i, 0))],
        out_specs=pl.BlockSpec(memory_space=pl.ANY),
        scratch_shapes=[pltpu.VMEM((EXPERT_ROWS, w), xp.dtype), pltpu.SemaphoreType.DMA,
                        pltpu.SemaphoreType.DMA],
        compiler_params=_cparams("arbitrary"),
        name="dispatch",
    )(dest.reshape(steps, 1, tokens * TOP_K), pad_end, counts, xp)


def _w1_prep_kernel(w_ref, o_ref):
    gw = 2 * LANES
    c = lax.broadcasted_iota(I32, (gw, gw), 0)
    j = lax.broadcasted_iota(I32, (gw, gw), 1)
    src = jnp.where(j < LANES, 2 * j, 2 * (j - LANES) + 1)
    perm = jnp.where(c == src, 1.0, 0.0).astype(BF16)
    for q in range(w_ref.shape[-1] // gw):
        cs = slice(q * gw, (q + 1) * gw)
        o_ref[0, :, cs] = _dot(w_ref[0, :, cs].astype(BF16), perm).astype(BF16)


def _regroup_bias(b1):
    ne, f2 = b1.shape
    return b1.reshape(ne, f2 // (2 * LANES), LANES, 2).transpose(0, 1, 3, 2).reshape(ne, 1, f2)


def _expert_kernel(be_ref, nu_ref, x_ref, w1_ref, b1_ref, w2_ref, b2_ref, o_ref):
    del be_ref
    used = pl.program_id(0) < nu_ref[0]

    @pl.when(used)
    def _():
        lo, hi = _unpack_halves(x_ref[...])
        half = lo.shape[1]
        gu = _dot(lo, w1_ref[0, :half, :]) + _dot(hi, w1_ref[0, half:, :]) + b1_ref[0]
        acts = []
        for j in range(gu.shape[1] // (2 * LANES)):
            glu = jnp.minimum(gu[:, 2 * j * LANES:(2 * j + 1) * LANES], SWIGLU_LIMIT)
            lin = jnp.clip(gu[:, (2 * j + 1) * LANES:(2 * j + 2) * LANES], -SWIGLU_LIMIT, SWIGLU_LIMIT)
            acts.append(glu * _sigmoid(SWIGLU_ALPHA * glu) * (lin + 1.0))
        act = jnp.concatenate(acts, axis=1)
        o_ref[...] = _pack_halves(_dot(act.astype(BF16), w2_ref[0]) + b2_ref[0])

    @pl.when(jnp.logical_not(used))
    def _():
        o_ref[...] = jnp.zeros_like(o_ref)


def _experts(xs, block_e, n_used, w1p, b1p, w2b, b2):
    n_rows, half = xs.shape
    ne, f, d = w2b.shape
    n_blocks = n_rows // EXPERT_ROWS

    def blk(b, be, nu):
        return jnp.minimum(b, nu[0] - 1)

    grid_spec = pltpu.PrefetchScalarGridSpec(
        num_scalar_prefetch=2,
        grid=(n_blocks,),
        in_specs=[pl.BlockSpec((EXPERT_ROWS, half), lambda b, be, nu: (blk(b, be, nu), 0)),
                  pl.BlockSpec((1, d, 2 * f), lambda b, be, nu: (be[blk(b, be, nu)], 0, 0)),
                  pl.BlockSpec((1, 1, 2 * f), lambda b, be, nu: (be[blk(b, be, nu)], 0, 0)),
                  pl.BlockSpec((1, f, d), lambda b, be, nu: (be[blk(b, be, nu)], 0, 0)),
                  pl.BlockSpec((1, 1, d), lambda b, be, nu: (be[blk(b, be, nu)], 0, 0))],
        out_specs=pl.BlockSpec((EXPERT_ROWS, d // 2), lambda b, be, nu: (b, 0)),
    )
    return pl.pallas_call(
        _expert_kernel,
        out_shape=jax.ShapeDtypeStruct((n_rows, d // 2), U32),
        grid_spec=grid_spec,
        compiler_params=_cparams("arbitrary"),
        name="experts",
    )(block_e, n_used, xs, w1p, b1p, w2b, b2.reshape(ne, 1, d))


def _combine_kernel(dcur_ref, dnxt_ref, tw_ref, x_ref, g_ref, b_ref, ys_hbm, o_ref, ob_ref, buf_ref, sem, *, alpha):
    i = pl.program_id(0)
    tokens, d = x_ref.shape
    half = d // 2
    slot = i % 2

    def copy(s, t, k, src_row):
        return pltpu.make_async_copy(ys_hbm.at[pl.ds(src_row, 1)], buf_ref.at[s, k, pl.ds(t, 1)], sem.at[s])

    def request(d_ref, s):
        def body(t, c):
            for k in range(TOP_K):
                copy(s, t, k, d_ref[0, 0, t * TOP_K + k]).start()
            return c

        lax.fori_loop(0, tokens, body, 0)

    @pl.when(i == 0)
    def _():
        request(dcur_ref, 0)

    @pl.when(i + 1 < pl.num_programs(0))
    def _():
        request(dnxt_ref, 1 - slot)

    for k in range(TOP_K):
        pltpu.make_async_copy(ys_hbm.at[pl.ds(0, tokens)], buf_ref.at[slot, k], sem.at[slot]).wait()
    tw = tw_ref[...]
    x = x_ref[...]
    y_lo = alpha * x[:, :half]
    y_hi = alpha * x[:, half:]
    for k in range(TOP_K):
        u = buf_ref[slot, k]
        wk = tw[:, k:k + 1]
        y_lo = y_lo + wk * pltpu.bitcast(u << 16, F32)
        y_hi = y_hi + wk * pltpu.bitcast(u & jnp.uint32(0xFFFF0000), F32)
    out = _ln_rows(jnp.concatenate([y_lo, y_hi], axis=1), g_ref[...], b_ref[...])
    o_ref[...] = out
    ob_ref[...] = out.astype(BF16)


def _combine(ys, dest, tw, x, g, b, alpha, *, tokens=128):
    s, d = x.shape
    tokens = min(tokens, s)
    steps = s // tokens
    kern = functools.partial(_combine_kernel, alpha=alpha)
    dest3 = dest.reshape(steps, 1, tokens * TOP_K)
    row = lambda width: pl.BlockSpec((tokens, width), lambda i: (i, 0))
    const = lambda shape: pl.BlockSpec(shape, lambda i: (0,) * len(shape))
    return pl.pallas_call(
        kern,
        out_shape=(jax.ShapeDtypeStruct((s, d), F32), jax.ShapeDtypeStruct((s, d), BF16)),
        grid=(steps,),
        in_specs=[pl.BlockSpec((1, 1, tokens * TOP_K), lambda i: (i, 0, 0), memory_space=pltpu.SMEM),
                  pl.BlockSpec((1, 1, tokens * TOP_K), lambda i: (jnp.minimum(i + 1, steps - 1), 0, 0),
                               memory_space=pltpu.SMEM),
                  row(LANES), row(d), const((1, d)), const((1, d)),
                  pl.BlockSpec(memory_space=pl.ANY)],
        out_specs=(row(d), row(d)),
        scratch_shapes=[pltpu.VMEM((2, TOP_K, tokens, d // 2), U32), pltpu.SemaphoreType.DMA((2,))],
        compiler_params=_cparams("arbitrary"),
        name="combine",
    )(dest3, dest3, tw, x, g.reshape(1, d), b.reshape(1, d), ys)


def _mm_rows_kernel(starts_ref, a_ref, wt_ref, *refs, n_casts):
    del starts_ref
    cast_in, o_ref, cast_out, wb_ref = refs[:n_casts], refs[n_casts], refs[n_casts + 1:-1], refs[-1]

    @pl.when(pl.program_id(1) == 0)
    def _():
        wb_ref[...] = wt_ref[0].astype(BF16)

    o_ref[...] = lax.dot_general(a_ref[...], wb_ref[...], (((1,), (1,)), ((), ())),
                                 preferred_element_type=F32).astype(o_ref.dtype)

    for src, dst in zip(cast_in, cast_out):
        dst[...] = src[...].astype(BF16)


def _matmul_rows(a, wt, layer, starts, tn, *, tm, name, casts=()):
    m, k = a.shape
    tm = min(tm, m)
    n_tiles, m_tiles = len(starts), m // tm
    in_specs = [pl.BlockSpec((tm, k), lambda j, i, st: (i, 0)),
                pl.BlockSpec((pl.Element(1), pl.Element(tn), pl.Element(k)),
                             lambda j, i, st: (layer, pl.multiple_of(st[j], SUBLANES), 0))]
    out_specs = [pl.BlockSpec((tm, tn), lambda j, i, st: (i, j))]
    out_shape = [jax.ShapeDtypeStruct((m, n_tiles * tn), F32)]
    args = [jnp.asarray(starts, I32), a, wt]
    for arr, rows, rb in casts:
        nb = rows // rb
        assert rows % rb == 0 and nb <= n_tiles * m_tiles
        cols = arr.shape[1]
        in_specs.append(pl.BlockSpec(
            (rb, cols), lambda j, i, st, nb=nb: (layer * nb + jnp.minimum(j * m_tiles + i, nb - 1), 0)))
        out_specs.append(pl.BlockSpec(
            (rb, cols), lambda j, i, st, nb=nb: (jnp.minimum(j * m_tiles + i, nb - 1), 0)))
        out_shape.append(jax.ShapeDtypeStruct((rows, cols), BF16))
        args.append(arr)
    grid_spec = pltpu.PrefetchScalarGridSpec(
        num_scalar_prefetch=1,
        grid=(n_tiles, m_tiles),
        in_specs=in_specs,
        out_specs=tuple(out_specs),
        scratch_shapes=[pltpu.VMEM((tn, k), BF16)],
    )
    return pl.pallas_call(
        functools.partial(_mm_rows_kernel, n_casts=len(casts)),
        out_shape=tuple(out_shape),
        grid_spec=grid_spec,
        compiler_params=_cparams("arbitrary", "arbitrary", vmem=BIG_TILE_VMEM_LIMIT),
        name=name,
    )(*args)


def _in_proj(xb, w_in, layer, sizes, inner, side_weights):
    z, xbc, dt, u, v, gate_c, x_c, gates = sizes
    src = {"z": 0, "xs": z, "bc": z + inner, "dt": z + xbc}
    src["u"] = src["dt"] + dt
    src["v"] = src["u"] + u
    src["gate_c"] = src["v"] + v
    src["x_c"] = src["gate_c"] + gate_c
    src["gates"] = src["x_c"] + x_c
    width = {"z": z, "xs": inner, "u": u, "v": v, "gate_c": gate_c, "x_c": x_c, "bc": xbc - inner, "gates": gates}
    tn = max(t for t in (1024, 512, 256, 128) if all(w % t == 0 for w in width.values()))
    assert dt <= LANES and all(s % SUBLANES == 0 for s in src.values())
    lay, starts, off = {}, [], 0
    for name in ("z", "xs", "u", "v", "gate_c", "x_c", "bc", "gates"):
        lay[name] = off
        starts += [src[name] + t for t in range(0, width[name], tn)]
        off += width[name]
    wt = jnp.swapaxes(w_in, 1, 2)
    tm = min(512, xb.shape[0])
    m_tiles = xb.shape[0] // tm

    def cast_job(arr, steps):
        rows = math.prod(arr.shape[1:-1])
        rb = 2 * SUBLANES
        while rows // rb > steps or rows % rb:
            rb *= 2
        return arr.reshape(-1, arr.shape[-1]), rows, rb

    proj, big_copy = _matmul_rows(xb, wt, layer, starts, tn, tm=tm, name="in_proj",
                                  casts=[cast_job(side_weights[0], len(starts) * m_tiles)])
    dt_raw, *copies = _matmul_rows(xb, wt, layer, [src["dt"]], LANES, tm=tm, name="dt_proj",
                                   casts=[cast_job(w, m_tiles) for w in side_weights[1:]])
    return proj, dt_raw, lay, [big_copy] + copies


def _moe_plan(te, rk, cnt, ne, n_blocks):
    counts = cnt[0, :ne]
    padded = (counts + EXPERT_ROWS - 1) // EXPERT_ROWS * EXPERT_ROWS
    pad_end = jnp.cumsum(padded)
    pad_start = pad_end - padded
    picked = te[:, :TOP_K, None] == jnp.arange(ne, dtype=I32)[None, None, :]
    dest = (jnp.sum(jnp.where(picked, pad_start[None, None, :], 0), axis=-1) + rk[:, :TOP_K]).astype(I32)
    dest = dest.reshape(-1)
    block_start = jnp.arange(n_blocks, dtype=I32) * EXPERT_ROWS
    block_e = jnp.sum(pad_end[None, :] <= block_start[:, None], axis=1)
    block_e = jnp.minimum(block_e, ne - 1).astype(I32)
    n_used = (pad_end[-1] // EXPERT_ROWS).astype(I32).reshape(1)
    return dest, block_e, n_used, pad_end.astype(I32), counts.astype(I32)


def kernel(x, ln_emb_g, ln_emb_b, w_in, b_merge, conv_a_w, conv_a_b, dt_bias, a_log, d_skip, norm_a_g,
           ln_v_g, ln_v_b, w_spatial, b_spatial, conv_c_w, conv_c_b, w_rg_a, b_rg_a, w_rg_x, b_rg_x, lam,
           p_a, p_b, p_c, w_o, ln_mix_g, ln_mix_b, w_router, b_router, w1, b1, w2, b2, ln_ffn_g, ln_ffn_b):
    bsz, seq, d = x.shape
    depth = w_in.shape[0]
    alpha = (2 * depth) ** 0.25
    heads = dt_bias.shape[1]
    inner = norm_a_g.shape[1]
    sizes = (inner, conv_a_w.shape[2], heads, ln_v_g.shape[1], ln_v_g.shape[1], lam.shape[1], lam.shape[1],
             3 * d)
    ne = w_router.shape[2]
    s = bsz * seq
    n_rows = s * TOP_K + ne * EXPERT_ROWS
    n_blocks = n_rows // EXPERT_ROWS

    xf, xb = _ln0(x.reshape(s, d), ln_emb_g, ln_emb_b)
    for l in range(depth):
        proj, dt_raw, lay, (w2b, pab, pbb, pcb) = _in_proj(xb, w_in, l, sizes, inner, (w2, p_a, p_b, p_c))
        y_a = _ssd(proj, dt_raw, lay, conv_a_w[l], conv_a_b[l], dt_bias[l], a_log[l], d_skip[l], norm_a_g[l])
        y_b = _gmlp(proj, lay, ln_v_g[l], ln_v_b[l], w_spatial[l], b_spatial[l])
        y_c, w1p = _lru(proj, lay, conv_c_w[l], conv_c_b[l], w_rg_a[l], b_rg_a[l], w_rg_x[l], b_rg_x[l], lam[l],
                        w1, l)
        merged = _merge(y_a, y_b, y_c, pab, pbb, pcb, proj, lay, b_merge[l])
        h = _matmul(merged, w_o, l, F32, tm=512, tn=1024, res=xf, alpha=alpha, name="w_o")
        xm, xp, te, tw, rk, cnt = _ln_router(h, ln_mix_g[l], ln_mix_b[l], w_router[l], b_router[l])
        dest, block_e, n_used, pad_end, counts = _moe_plan(te, rk, cnt, ne, n_blocks)
        xs = _dispatch(xp, dest, pad_end, counts, n_rows)
        ys = _experts(xs, block_e, n_used, w1p, _regroup_bias(b1[l]), w2b.reshape(w2.shape[1:]), b2[l])
        xf, xb = _combine(ys, dest, tw, xm, ln_ffn_g[l], ln_ffn_b[l], alpha)
    return xf.reshape(bsz, seq, d)
```

```python
import functools
import math

import jax
import jax.numpy as jnp
from jax import lax
from jax.experimental import pallas as pl
from jax.experimental.pallas import tpu as pltpu

F32 = jnp.float32
BF16 = jnp.bfloat16
U32 = jnp.uint32
I32 = jnp.int32

SSD_GROUPS = 4
SSD_STATE = 128
CHUNK = 128
GMLP_GROUPS = 8
LRU_C = 8.0
TOP_K = 4
SWIGLU_LIMIT = 7.0
SWIGLU_ALPHA = 1.702
LN_EPS = 1e-5
RMS_EPS = 1e-5

LANES = 128
SUBLANES = 8
VMEM_LIMIT = 56 * 1024 * 1024
BIG_TILE_VMEM_LIMIT = 60 * 1024 * 1024
EXPERT_ROWS = 256
WINDOW_BLOCK = 1024
NEG_BIG = -1e30


def _cparams(*sem, vmem=VMEM_LIMIT):
    return pltpu.CompilerParams(dimension_semantics=sem, vmem_limit_bytes=vmem)


def _dot(a, b):
    return jnp.dot(a, b, preferred_element_type=F32)


def _split3(x):
    hi = x.astype(BF16)
    r = x - hi.astype(F32)
    mid = r.astype(BF16)
    lo = (r - mid.astype(F32)).astype(BF16)
    return hi, mid, lo


def _exact_right(x, m):
    hi, mid, lo = _split3(x)
    return _dot(hi, m) + (_dot(mid, m) + _dot(lo, m))


def _exact_left(m, x):
    hi, mid, lo = _split3(x)
    return _dot(m, hi) + (_dot(m, mid) + _dot(m, lo))


def _sigmoid(x):
    return 1.0 / (1.0 + jnp.exp(-x))


def _silu(x):
    return x * _sigmoid(x)


def _softplus(x):
    return jnp.maximum(x, 0.0) + jnp.log1p(jnp.exp(-jnp.abs(x)))


def _gelu(x):
    c = math.sqrt(2.0 / math.pi)
    return x * (0.5 * (1.0 + jnp.tanh(c * (x + 0.044715 * (x * x * x)))))


def _ln_rows(x, g, b):
    mu = jnp.mean(x, axis=-1, keepdims=True)
    xc = x - mu
    var = jnp.mean(xc * xc, axis=-1, keepdims=True)
    return xc * lax.rsqrt(var + LN_EPS) * g + b


def _causal_conv(x, carry, w, b):
    kw = w.shape[0]
    rows = x.shape[0]
    xp = jnp.concatenate([carry, x], axis=0)
    y = x * w[kw - 1:kw, :] + b
    for k in range(kw - 1):
        s = SUBLANES - (kw - 1) + k
        y = y + xp[s:s + rows, :] * w[k:k + 1, :]
    return y


def _pack_halves(y):
    half = y.shape[1] // 2
    lo = pltpu.bitcast(y[:, :half].astype(BF16).astype(F32), U32)
    hi = pltpu.bitcast(y[:, half:].astype(BF16).astype(F32), U32)
    return (hi & jnp.uint32(0xFFFF0000)) | (lo >> 16)


def _unpack_halves(u):
    lo = pltpu.bitcast(u << 16, F32).astype(BF16)
    hi = pltpu.bitcast(u & jnp.uint32(0xFFFF0000), F32).astype(BF16)
    return lo, hi


def _window_specs(off, width, rows, blk):
    first, shift = divmod(off, blk)
    nblk = pl.cdiv(shift + width, blk)
    specs = [pl.BlockSpec((rows, blk), lambda i, q=q: (i, first + q)) for q in range(nblk)]
    return specs, shift


def _window_load(refs, shift, width, rows=slice(None)):
    blk = refs[0].shape[1]
    first, lane_shift = divmod(shift, LANES)
    n_out = width // LANES

    def lane_block(k):
        q, o = divmod(k * LANES, blk)
        return refs[q][rows, o:o + LANES].astype(F32)

    if lane_shift == 0:
        return jnp.concatenate([lane_block(first + k) for k in range(n_out)], axis=1)
    rolled = [pltpu.roll(lane_block(first + k), LANES - lane_shift, 1) for k in range(n_out + 1)]
    lane = lax.broadcasted_iota(I32, rolled[0].shape, 1)
    outs = [jnp.where(lane < LANES - lane_shift, rolled[k], rolled[k + 1]) for k in range(n_out)]
    return jnp.concatenate(outs, axis=1)


def _ln0_kernel(x_ref, g_ref, b_ref, o_ref, ob_ref):
    y = _ln_rows(x_ref[...], g_ref[...], b_ref[...])
    o_ref[...] = y
    ob_ref[...] = y.astype(BF16)


def _ln0(x, g, b):
    s, d = x.shape
    tm = min(256, s)
    return pl.pallas_call(
        _ln0_kernel,
        out_shape=(jax.ShapeDtypeStruct((s, d), F32), jax.ShapeDtypeStruct((s, d), BF16)),
        grid=(s // tm,),
        in_specs=[pl.BlockSpec((tm, d), lambda i: (i, 0)),
                  pl.BlockSpec((1, d), lambda i: (0, 0)),
                  pl.BlockSpec((1, d), lambda i: (0, 0))],
        out_specs=(pl.BlockSpec((tm, d), lambda i: (i, 0)),
                   pl.BlockSpec((tm, d), lambda i: (i, 0))),
        compiler_params=_cparams("parallel"),
        name="ln0",
    )(x, g.reshape(1, d), b.reshape(1, d))


def _cast_weight_tile(w_ref, wb_ref, n_valid):
    tn = w_ref.shape[1]
    col = pl.program_id(0) * tn + lax.broadcasted_iota(I32, w_ref.shape, 1)
    wb_ref[...] = jnp.where(col < n_valid, w_ref[...], 0.0).astype(BF16)


def _mm_kernel(a_ref, w_ref, o_ref, wb_ref, *, n_valid):
    @pl.when(pl.program_id(1) == 0)
    def _():
        _cast_weight_tile(w_ref, wb_ref, n_valid)

    o_ref[...] = _dot(a_ref[...], wb_ref[...]).astype(o_ref.dtype)


def _mm_res_kernel(a_ref, w_ref, r_ref, o_ref, wb_ref, *, n_valid, alpha):
    @pl.when(pl.program_id(1) == 0)
    def _():
        _cast_weight_tile(w_ref, wb_ref, n_valid)

    o_ref[...] = (alpha * r_ref[...] + _dot(a_ref[...], wb_ref[...])).astype(o_ref.dtype)


def _matmul(a, w, layer, out_dtype, *, tm, tn, res=None, alpha=1.0, name="mm"):
    m, k = a.shape
    n = w.shape[2]
    tm = min(tm, m)
    tn = min(tn, n)
    n_tiles = pl.cdiv(n, tn)
    in_specs = [pl.BlockSpec((tm, k), lambda j, i: (i, 0)),
                pl.BlockSpec((None, k, tn), lambda j, i: (layer, 0, j))]
    args = [a, w]
    if res is None:
        kern = functools.partial(_mm_kernel, n_valid=n)
    else:
        kern = functools.partial(_mm_res_kernel, n_valid=n, alpha=alpha)
        in_specs.append(pl.BlockSpec((tm, tn), lambda j, i: (i, j)))
        args.append(res)
    return pl.pallas_call(
        kern,
        out_shape=jax.ShapeDtypeStruct((m, n_tiles * tn), out_dtype),
        grid=(n_tiles, m // tm),
        in_specs=in_specs,
        out_specs=pl.BlockSpec((tm, tn), lambda j, i: (i, j)),
        scratch_shapes=[pltpu.VMEM((k, tn), BF16)],
        compiler_params=_cparams("parallel", "arbitrary", vmem=BIG_TILE_VMEM_LIMIT),
        name=name,
    )(*args)


def _ssd_kernel(z_ref, xs_ref, bc_ref, dt_ref, cwx_ref, cbx_ref, cwb_ref, cbb_ref, dtb_ref, alog_ref,
                dskip_ref, ng_ref, e_ref, o_ref, cx_ref, cb_ref, st_ref, *, heads, head_dim):
    i = pl.program_id(0)

    @pl.when(i == 0)
    def _():
        cx_ref[...] = jnp.zeros_like(cx_ref)
        cb_ref[...] = jnp.zeros_like(cb_ref)
        st_ref[...] = jnp.zeros_like(st_ref)

    n = SSD_STATE
    gw = (heads // SSD_GROUPS) * head_dim
    hpg = heads // SSD_GROUPS
    pair = LANES // head_dim

    xs_raw = xs_ref[...].astype(F32)
    bc_raw = bc_ref[...].astype(F32)
    xs = _silu(_causal_conv(xs_raw, cx_ref[...], cwx_ref[...], cbx_ref[...]))
    bc = _silu(_causal_conv(bc_raw, cb_ref[...], cwb_ref[...], cbb_ref[...]))
    cx_ref[...] = xs_raw[CHUNK - SUBLANES:, :]
    cb_ref[...] = bc_raw[CHUNK - SUBLANES:, :]

    row = lax.broadcasted_iota(I32, (CHUNK, CHUNK), 0)
    col = lax.broadcasted_iota(I32, (CHUNK, CHUNK), 1)
    dt = _softplus(jnp.where(col < heads, dt_ref[...], 0.0) + dtb_ref[...])
    da = dt * (-jnp.exp(alog_ref[...]))
    causal = row >= col
    ltri = jnp.where(causal, 1.0, 0.0).astype(BF16)
    cum = _exact_left(ltri, da)
    cum_t = cum.T
    expand = e_ref[...]
    cum_e = _exact_right(cum, expand)
    dt_e = _exact_right(dt, expand)
    last_e = cum_e[CHUNK - 1:CHUNK, :]
    decay_out = jnp.exp(cum_e)
    decay_st = jnp.exp(last_e - cum_e)
    chunk_decay = jnp.exp(last_e)
    xdt = xs * dt_e
    xdt_b = xdt.astype(BF16)
    xst_b = (xdt * decay_st).astype(BF16)
    lane = lax.broadcasted_iota(I32, (CHUNK, LANES), 1)

    for g in range(SSD_GROUPS):
        bg = bc[:, g * n:(g + 1) * n].astype(BF16)
        cg = bc[:, (SSD_GROUPS + g) * n:(SSD_GROUPS + g + 1) * n].astype(BF16)
        cb = lax.dot_general(cg, bg, (((1,), (1,)), ((), ())), preferred_element_type=F32)
        slabs = []
        for j in range(hpg // pair):
            h0 = g * hpg + j * pair
            sc = []
            for h in range(h0, h0 + pair):
                seg = cum[:, h:h + 1] - cum_t[h:h + 1, :]
                dec = jnp.exp(jnp.where(causal, seg, -jnp.inf))
                sc.append((cb * dec).astype(BF16))
            xp = xdt_b[:, h0 * head_dim:h0 * head_dim + LANES]
            blocks = [jnp.where((lane >= q * head_dim) & (lane < (q + 1) * head_dim), xp, jnp.zeros_like(xp))
                      for q in range(pair)]
            slabs.append(_dot(jnp.concatenate(sc, axis=1), jnp.concatenate(blocks, axis=0)))
        y_diag = jnp.concatenate(slabs, axis=1)
        cs = slice(g * gw, (g + 1) * gw)
        st = st_ref[g]
        y_off = _dot(cg, st.astype(BF16)) * decay_out[:, cs]
        st_new = lax.dot_general(bg, xst_b[:, cs], (((0,), (0,)), ((), ())), preferred_element_type=F32)
        st_ref[g] = st * chunk_decay[:, cs] + st_new
        y = y_diag + y_off + xs[:, cs] * dskip_ref[:, cs]
        yg = y * _silu(z_ref[:, cs].astype(F32))
        ms = jnp.mean(yg * yg, axis=-1, keepdims=True)
        o_ref[:, cs] = (yg * lax.rsqrt(ms + RMS_EPS) * ng_ref[:, cs]).astype(o_ref.dtype)


def _ssd(proj, dt_raw, lay, conv_w, conv_b, dt_bias, a_log, d_skip, norm_g):
    s = proj.shape[0]
    heads = dt_bias.shape[0]
    inner = norm_g.shape[0]
    head_dim = inner // heads
    bcw = 2 * SSD_GROUPS * SSD_STATE
    gw = inner // SSD_GROUPS
    pad = LANES - heads
    assert lay["z"] % inner == 0 and lay["xs"] % inner == 0 and lay["bc"] % bcw == 0
    expand = (jnp.arange(LANES)[:, None] == (jnp.arange(inner) // head_dim)[None, :]).astype(BF16)
    kern = functools.partial(_ssd_kernel, heads=heads, head_dim=head_dim)
    const = lambda shape: pl.BlockSpec(shape, lambda i: (0,) * len(shape))
    return pl.pallas_call(
        kern,
        out_shape=jax.ShapeDtypeStruct((s, inner), BF16),
        grid=(s // CHUNK,),
        in_specs=[pl.BlockSpec((CHUNK, inner), lambda i: (i, lay["z"] // inner)),
                  pl.BlockSpec((CHUNK, inner), lambda i: (i, lay["xs"] // inner)),
                  pl.BlockSpec((CHUNK, bcw), lambda i: (i, lay["bc"] // bcw)),
                  pl.BlockSpec((CHUNK, LANES), lambda i: (i, 0)),
                  const((conv_w.shape[0], inner)), const((1, inner)),
                  const((conv_w.shape[0], bcw)), const((1, bcw)),
                  const((1, LANES)), const((1, LANES)),
                  const((1, inner)), const((1, inner)), const((LANES, inner))],
        out_specs=pl.BlockSpec((CHUNK, inner), lambda i: (i, 0)),
        scratch_shapes=[pltpu.VMEM((SUBLANES, inner), F32), pltpu.VMEM((SUBLANES, bcw), F32),
                        pltpu.VMEM((SSD_GROUPS, SSD_STATE, gw), F32)],
        compiler_params=_cparams("arbitrary"),
        name="ssd",
    )(proj, proj, proj, dt_raw,
      conv_w[:, :inner], conv_b[:inner].reshape(1, inner),
      conv_w[:, inner:], conv_b[inner:].reshape(1, bcw),
      jnp.pad(dt_bias, (0, pad)).reshape(1, LANES), jnp.pad(a_log, (0, pad)).reshape(1, LANES),
      jnp.repeat(d_skip, head_dim).reshape(1, inner), norm_g.reshape(1, inner), expand)


def _gmlp_kernel(*refs, chunks, nu, nv, su, sv):
    u_refs, v_refs = refs[:nu], refs[nu:nu + nv]
    lg_ref, lb_ref, ws_ref, be_ref, o_ref = refs[nu + nv:]
    width = o_ref.shape[1]
    gw = width // GMLP_GROUPS
    row = lax.broadcasted_iota(I32, (CHUNK, CHUNK), 0)
    col = lax.broadcasted_iota(I32, (CHUNK, CHUNK), 1)
    tril = row >= col
    ws = [jnp.where(tril, ws_ref[g], 0.0).astype(BF16) for g in range(GMLP_GROUPS)]
    for c in range(chunks):
        rs = slice(c * CHUNK, (c + 1) * CHUNK)
        v = _ln_rows(_gelu(_window_load(v_refs, sv, width, rs)), lg_ref[...], lb_ref[...]).astype(BF16)
        u = _gelu(_window_load(u_refs, su, width, rs))
        for g in range(GMLP_GROUPS):
            cs = slice(g * gw, (g + 1) * gw)
            mixed = _dot(ws[g], v[:, cs]) + be_ref[:, cs]
            o_ref[rs, cs] = (u[:, cs] * mixed).astype(o_ref.dtype)


def _gmlp(proj, lay, ln_g, ln_b, w_s, b_s, *, rows=256):
    s = proj.shape[0]
    width = ln_g.shape[0]
    gw = width // GMLP_GROUPS
    rows = min(rows, s)
    bias_e = jnp.repeat(b_s.T, gw, axis=1)
    u_specs, su = _window_specs(lay["u"], width, rows, WINDOW_BLOCK)
    v_specs, sv = _window_specs(lay["v"], width, rows, WINDOW_BLOCK)
    kern = functools.partial(_gmlp_kernel, chunks=rows // CHUNK, nu=len(u_specs), nv=len(v_specs), su=su, sv=sv)
    const = lambda shape: pl.BlockSpec(shape, lambda i: (0,) * len(shape))
    return pl.pallas_call(
        kern,
        out_shape=jax.ShapeDtypeStruct((s, width), BF16),
        grid=(s // rows,),
        in_specs=u_specs + v_specs + [const((1, width)), const((1, width)),
                                      const((GMLP_GROUPS, CHUNK, CHUNK)), const((CHUNK, width))],
        out_specs=pl.BlockSpec((rows, width), lambda i: (i, 0)),
        compiler_params=_cparams("parallel"),
        name="gmlp",
    )(*([proj] * (len(u_specs) + len(v_specs))), ln_g.reshape(1, width), ln_b.reshape(1, width), w_s, bias_e)


def _lru_kernel(*refs, ng, nx, sg, sx):
    gate_refs, x_refs = refs[:ng], refs[ng:ng + nx]
    (cw_ref, cb_ref, wa_ref, ba_ref, wx_ref, bx_ref, lam_ref, w1_ref, o_ref, w1p_ref,
     carry_ref, h_ref) = refs[ng + nx:]
    i = pl.program_id(0)

    _w1_prep_kernel(w1_ref, w1p_ref)

    @pl.when(i == 0)
    def _():
        carry_ref[...] = jnp.zeros_like(carry_ref)
        h_ref[...] = jnp.zeros_like(h_ref)

    rows, width = o_ref.shape
    nblk, blk = wa_ref.shape[0], wa_ref.shape[1]
    x_raw = _window_load(x_refs, sx, width)
    xr = _causal_conv(x_raw, carry_ref[...], cw_ref[...], cb_ref[...])
    carry_ref[...] = x_raw[rows - SUBLANES:, :]
    xb = xr.astype(BF16)
    ra = jnp.concatenate([_dot(xb[:, q * blk:(q + 1) * blk], wa_ref[q]) for q in range(nblk)], axis=1)
    ia = jnp.concatenate([_dot(xb[:, q * blk:(q + 1) * blk], wx_ref[q]) for q in range(nblk)], axis=1)
    r = _sigmoid(ra + ba_ref[...])
    ig = _sigmoid(ia + bx_ref[...])
    log_a = (-LRU_C) * r * _softplus(-lam_ref[...])
    a = jnp.exp(log_a)
    b = jnp.sqrt((1.0 - a) * (1.0 + a)) * (ig * xr)
    ridx = lax.broadcasted_iota(I32, (SUBLANES, width), 0)
    h_in = h_ref[...]
    groups = []
    for g in range(rows // SUBLANES):
        gs = slice(g * SUBLANES, (g + 1) * SUBLANES)
        ag, bg = a[gs, :], b[gs, :]
        d = 1
        while d < SUBLANES:
            keep = ridx >= d
            bg = jnp.where(keep, ag * pltpu.roll(bg, d, 0) + bg, bg)
            ag = jnp.where(keep, ag * pltpu.roll(ag, d, 0), ag)
            d *= 2
        hg = ag * h_in + bg
        h_in = hg[SUBLANES - 1:SUBLANES, :]
        groups.append(hg)
    h = jnp.concatenate(groups, axis=0)
    h_ref[...] = h_in
    o_ref[...] = (h * _gelu(_window_load(gate_refs, sg, width))).astype(o_ref.dtype)


def _lru(proj, lay, conv_w, conv_b, w_a, b_a, w_x, b_x, lam, w1, layer, *, rows=128):
    s = proj.shape[0]
    width = lam.shape[0]
    nblk, blk = w_a.shape[0], w_a.shape[1]
    rows = min(rows, s)
    steps = s // rows
    g_specs, sg = _window_specs(lay["gate_c"], width, rows, WINDOW_BLOCK)
    x_specs, sx = _window_specs(lay["x_c"], width, rows, WINDOW_BLOCK)
    _, ne, d, f2 = w1.shape
    gw = 2 * LANES
    groups = f2 // gw
    nw = pl.cdiv(ne * groups, steps)
    assert groups % nw == 0
    per_e = groups // nw
    n_side = ne * per_e

    def side(i):
        h = jnp.minimum(i, n_side - 1)
        return h // per_e, 0, h % per_e

    kern = functools.partial(_lru_kernel, ng=len(g_specs), nx=len(x_specs), sg=sg, sx=sx)
    const = lambda shape: pl.BlockSpec(shape, lambda i: (0,) * len(shape))
    return pl.pallas_call(
        kern,
        out_shape=(jax.ShapeDtypeStruct((s, width), BF16), jax.ShapeDtypeStruct((ne, d, f2), BF16)),
        grid=(steps,),
        in_specs=g_specs + x_specs + [const((conv_w.shape[0], width)), const((1, width)),
                                      const((nblk, blk, blk)), const((1, width)),
                                      const((nblk, blk, blk)), const((1, width)), const((1, width)),
                                      pl.BlockSpec((None, 1, d, nw * gw), lambda i: (layer, *side(i)))],
        out_specs=(pl.BlockSpec((rows, width), lambda i: (i, 0)),
                   pl.BlockSpec((1, d, nw * gw), side)),
        scratch_shapes=[pltpu.VMEM((SUBLANES, width), F32), pltpu.VMEM((1, width), F32)],
        compiler_params=_cparams("arbitrary"),
        name="rglru",
    )(*([proj] * (len(g_specs) + len(x_specs))), conv_w, conv_b.reshape(1, width), w_a.astype(BF16),
      b_a.reshape(1, width), w_x.astype(BF16), b_x.reshape(1, width), lam.reshape(1, width), w1)


def _merge_kernel(*refs, shift, ng):
    y_refs, p_refs = refs[:3], refs[3:6]
    gate_refs = [refs[6 + br * ng:6 + (br + 1) * ng] for br in range(3)]
    b_refs = refs[6 + 3 * ng:9 + 3 * ng]
    o_ref = refs[9 + 3 * ng]
    tn = o_ref.shape[1]
    acc = None
    for br in range(3):
        gate = _sigmoid(_window_load(gate_refs[br], shift, tn) + b_refs[br][...])
        term = gate * _dot(y_refs[br][...], p_refs[br][...])
        acc = term if acc is None else acc + term
    o_ref[...] = acc.astype(o_ref.dtype)


def _merge(y_a, y_b, y_c, p_a, p_b, p_c, proj, lay, b_merge, *, tm=512, tn=1024):
    s, w = y_a.shape
    d = p_a.shape[1]
    tm = min(tm, s)
    tn = min(tn, d)
    g0, shift = divmod(lay["gates"], tn)
    assert shift < LANES
    ng = 2 if shift else 1
    nd = d // tn
    bm = b_merge.reshape(1, 3 * d)
    y_spec = pl.BlockSpec((tm, w), lambda j, i: (i, 0))
    p_spec = pl.BlockSpec((w, tn), lambda j, i: (0, j))
    g_specs = []
    for br in range(3):
        g_specs.append(pl.BlockSpec((tm, tn), lambda j, i, br=br: (i, g0 + br * nd + j)))
        if shift:
            g_specs.append(pl.BlockSpec((tm, LANES), lambda j, i, br=br: (i, (g0 + br * nd + j + 1) * (tn // LANES))))
    b_specs = [pl.BlockSpec((1, tn), lambda j, i, br=br: (0, br * nd + j)) for br in range(3)]
    kern = functools.partial(_merge_kernel, shift=shift, ng=ng)
    return pl.pallas_call(
        kern,
        out_shape=jax.ShapeDtypeStruct((s, d), BF16),
        grid=(d // tn, s // tm),
        in_specs=[y_spec, y_spec, y_spec, p_spec, p_spec, p_spec] + g_specs + b_specs,
        out_specs=pl.BlockSpec((tm, tn), lambda j, i: (i, j)),
        compiler_params=_cparams("parallel", "parallel", vmem=BIG_TILE_VMEM_LIMIT),
        name="merge",
    )(y_a, y_b, y_c, p_a, p_b, p_c, *([proj] * (3 * ng)), bm, bm, bm)


def _ln_router_kernel(h_ref, g_ref, b_ref, wr_ref, br_ref, x_ref, xp_ref, te_ref, tw_ref, rk_ref, cnt_ref,
                      carry_ref):
    i = pl.program_id(0)

    @pl.when(i == 0)
    def _():
        carry_ref[...] = jnp.zeros_like(carry_ref)

    rows = h_ref.shape[0]
    y = _ln_rows(h_ref[...], g_ref[...], b_ref[...])
    x_ref[...] = y
    xp_ref[...] = _pack_halves(y)
    yh = y.astype(BF16)
    yl = (y - yh.astype(F32)).astype(BF16)
    wr = wr_ref[...]
    wh = wr.astype(BF16)
    wl = (wr - wh.astype(F32)).astype(BF16)
    logits = _dot(yh, wh) + (_dot(yh, wl) + _dot(yl, wh)) + br_ref[...]
    lane = lax.broadcasted_iota(I32, (rows, LANES), 1)
    lane_f = lane.astype(F32)
    cur = logits
    vals, sels, idxs = [], [], []
    for _ in range(TOP_K):
        m = jnp.max(cur, axis=-1, keepdims=True)
        idx = jnp.min(jnp.where(cur == m, lane_f, float(LANES)), axis=-1, keepdims=True)
        sel = lane_f == idx
        vals.append(m)
        sels.append(sel)
        idxs.append(idx)
        cur = jnp.where(sel, -jnp.inf, cur)
    ex = [jnp.exp(v - vals[0]) for v in vals]
    den = ex[0]
    for e in ex[1:]:
        den = den + e
    multi = jnp.zeros((rows, LANES), F32)
    for sel in sels:
        multi = multi + jnp.where(sel, 1.0, 0.0)
    r2 = lax.broadcasted_iota(I32, (rows, rows), 0)
    c2 = lax.broadcasted_iota(I32, (rows, rows), 1)
    lower = jnp.where(r2 > c2, 1.0, 0.0).astype(BF16)
    before = _dot(lower, multi.astype(BF16)) + carry_ref[...]
    te = jnp.zeros((rows, LANES), F32)
    tw = jnp.zeros((rows, LANES), F32)
    rk = jnp.zeros((rows, LANES), F32)
    for k in range(TOP_K):
        rank_k = jnp.sum(jnp.where(sels[k], before, 0.0), axis=-1, keepdims=True)
        te = jnp.where(lane == k, idxs[k], te)
        tw = jnp.where(lane == k, ex[k] / den, tw)
        rk = jnp.where(lane == k, rank_k, rk)
    te_ref[...] = te.astype(I32)
    tw_ref[...] = tw
    rk_ref[...] = rk.astype(I32)
    carry_ref[...] = carry_ref[...] + jnp.sum(multi, axis=0, keepdims=True)
    cnt_ref[...] = carry_ref[...].astype(I32)


def _ln_router(h, g, b, w_router, b_router, *, tm=256):
    s, d = h.shape
    ne = w_router.shape[1]
    tm = min(tm, s)
    wr = jnp.pad(w_router, ((0, 0), (0, LANES - ne)))
    br = jnp.pad(b_router, (0, LANES - ne), constant_values=NEG_BIG).reshape(1, LANES)
    row = lambda width: pl.BlockSpec((tm, width), lambda i: (i, 0))
    const = lambda shape: pl.BlockSpec(shape, lambda i: (0,) * len(shape))
    return pl.pallas_call(
        _ln_router_kernel,
        out_shape=(jax.ShapeDtypeStruct((s, d), F32), jax.ShapeDtypeStruct((s, d // 2), U32),
                   jax.ShapeDtypeStruct((s, LANES), I32), jax.ShapeDtypeStruct((s, LANES), F32),
                   jax.ShapeDtypeStruct((s, LANES), I32), jax.ShapeDtypeStruct((1, LANES), I32)),
        grid=(s // tm,),
        in_specs=[row(d), const((1, d)), const((1, d)), const((d, LANES)), const((1, LANES))],
        out_specs=(row(d), row(d // 2), row(LANES), row(LANES), row(LANES), const((1, LANES))),
        scratch_shapes=[pltpu.VMEM((1, LANES), F32)],
        compiler_params=_cparams("arbitrary"),
        name="ln_router",
    )(h, g.reshape(1, d), b.reshape(1, d), wr, br)


def _row_copy(src_hbm, dst_hbm, sem, src_row, dst_row):
    return pltpu.make_async_copy(src_hbm.at[pl.ds(src_row, 1)], dst_hbm.at[pl.ds(dst_row, 1)], sem)


def _dispatch_kernel(dest_ref, end_ref, cnt_ref, x_ref, o_hbm, zero_ref, sem, zsem, *, tokens):
    @pl.when(pl.program_id(0) == 0)
    def _():
        zero_ref[...] = jnp.zeros_like(zero_ref)

        def fill(e):
            start = pl.multiple_of(end_ref[e] - EXPERT_ROWS, EXPERT_ROWS)
            return pltpu.make_async_copy(zero_ref, o_hbm.at[pl.ds(start, EXPERT_ROWS)], zsem)

        for e in range(end_ref.shape[0]):
            @pl.when(cnt_ref[e] > 0)
            def _():
                fill(e).start()

        for e in range(end_ref.shape[0]):
            @pl.when(cnt_ref[e] > 0)
            def _():
                fill(e).wait()

        def tail(b):
            return pltpu.make_async_copy(
                zero_ref, o_hbm.at[pl.ds(pl.multiple_of(b * EXPERT_ROWS, EXPERT_ROWS), EXPERT_ROWS)], zsem)

        n_used = end_ref[end_ref.shape[0] - 1] // EXPERT_ROWS
        n_blocks = o_hbm.shape[0] // EXPERT_ROWS
        lax.fori_loop(n_used, n_blocks, lambda b, c: (tail(b).start(), c)[1], 0)
        lax.fori_loop(n_used, n_blocks, lambda b, c: (tail(b).wait(), c)[1], 0)

    def issue(t, c):
        for k in range(TOP_K):
            _row_copy(x_ref, o_hbm, sem, t, dest_ref[0, 0, t * TOP_K + k]).start()
        return c

    lax.fori_loop(0, tokens, issue, 0)

    for _ in range(TOP_K):
        pltpu.make_async_copy(x_ref, o_hbm.at[pl.ds(0, tokens)], sem).wait()


def _dispatch(xp, dest, pad_end, counts, n_rows, *, tokens=256):
    s, w = xp.shape
    tokens = min(tokens, s)
    steps = s // tokens
    kern = functools.partial(_dispatch_kernel, tokens=tokens)
    return pl.pallas_call(
        kern,
        out_shape=jax.ShapeDtypeStruct((n_rows, w), xp.dtype),
        grid=(steps,),
        in_specs=[pl.BlockSpec((1, 1, tokens * TOP_K), lambda i: (i, 0, 0), memory_space=pltpu.SMEM),
                  pl.BlockSpec(memory_space=pltpu.SMEM),
                  pl.BlockSpec(memory_space=pltpu.SMEM),
                  pl.BlockSpec((tokens, w), lambda i: (i, 0))],
        out_specs=pl.BlockSpec(memory_space=pl.ANY),
        scratch_shapes=[pltpu.VMEM((EXPERT_ROWS, w), xp.dtype), pltpu.SemaphoreType.DMA,
                        pltpu.SemaphoreType.DMA],
        compiler_params=_cparams("arbitrary"),
        name="dispatch",
    )(dest.reshape(steps, 1, tokens * TOP_K), pad_end, counts, xp)


def _w1_prep_kernel(w_ref, o_ref):
    gw = 2 * LANES
    c = lax.broadcasted_iota(I32, (gw, gw), 0)
    j = lax.broadcasted_iota(I32, (gw, gw), 1)
    src = jnp.where(j < LANES, 2 * j, 2 * (j - LANES) + 1)
    perm = jnp.where(c == src, 1.0, 0.0).astype(BF16)
    for q in range(w_ref.shape[-1] // gw):
        cs = slice(q * gw, (q + 1) * gw)
        o_ref[0, :, cs] = _dot(w_ref[0, :, cs].astype(BF16), perm).astype(BF16)


def _regroup_bias(b1):
    ne, f2 = b1.shape
    return b1.reshape(ne, f2 // (2 * LANES), LANES, 2).transpose(0, 1, 3, 2).reshape(ne, 1, f2)


def _expert_kernel(be_ref, nu_ref, x_ref, w1_ref, b1_ref, w2_ref, b2_ref, o_ref):
    del be_ref
    used = pl.program_id(0) < nu_ref[0]

    @pl.when(used)
    def _():
        lo, hi = _unpack_halves(x_ref[...])
        half = lo.shape[1]
        gu = _dot(lo, w1_ref[0, :half, :]) + _dot(hi, w1_ref[0, half:, :]) + b1_ref[0]
        acts = []
        for j in range(gu.shape[1] // (2 * LANES)):
            glu = jnp.minimum(gu[:, 2 * j * LANES:(2 * j + 1) * LANES], SWIGLU_LIMIT)
            lin = jnp.clip(gu[:, (2 * j + 1) * LANES:(2 * j + 2) * LANES], -SWIGLU_LIMIT, SWIGLU_LIMIT)
            acts.append(glu * _sigmoid(SWIGLU_ALPHA * glu) * (lin + 1.0))
        act = jnp.concatenate(acts, axis=1)
        o_ref[...] = _pack_halves(_dot(act.astype(BF16), w2_ref[0]) + b2_ref[0])

    @pl.when(jnp.logical_not(used))
    def _():
        o_ref[...] = jnp.zeros_like(o_ref)


def _experts(xs, block_e, n_used, w1p, b1p, w2b, b2):
    n_rows, half = xs.shape
    ne, f, d = w2b.shape
    n_blocks = n_rows // EXPERT_ROWS

    def blk(b, be, nu):
        return jnp.minimum(b, nu[0] - 1)

    grid_spec = pltpu.PrefetchScalarGridSpec(
        num_scalar_prefetch=2,
        grid=(n_blocks,),
        in_specs=[pl.BlockSpec((EXPERT_ROWS, half), lambda b, be, nu: (blk(b, be, nu), 0)),
                  pl.BlockSpec((1, d, 2 * f), lambda b, be, nu: (be[blk(b, be, nu)], 0, 0)),
                  pl.BlockSpec((1, 1, 2 * f), lambda b, be, nu: (be[blk(b, be, nu)], 0, 0)),
                  pl.BlockSpec((1, f, d), lambda b, be, nu: (be[blk(b, be, nu)], 0, 0)),
                  pl.BlockSpec((1, 1, d), lambda b, be, nu: (be[blk(b, be, nu)], 0, 0))],
        out_specs=pl.BlockSpec((EXPERT_ROWS, d // 2), lambda b, be, nu: (b, 0)),
    )
    return pl.pallas_call(
        _expert_kernel,
        out_shape=jax.ShapeDtypeStruct((n_rows, d // 2), U32),
        grid_spec=grid_spec,
        compiler_params=_cparams("arbitrary"),
        name="experts",
    )(block_e, n_used, xs, w1p, b1p, w2b, b2.reshape(ne, 1, d))


def _combine_kernel(dcur_ref, dnxt_ref, tw_ref, x_ref, g_ref, b_ref, ys_hbm, o_ref, ob_ref, buf_ref, sem, *, alpha):
    i = pl.program_id(0)
    tokens, d = x_ref.shape
    half = d // 2
    slot = i % 2

    def copy(s, t, k, src_row):
        return pltpu.make_async_copy(ys_hbm.at[pl.ds(src_row, 1)], buf_ref.at[s, k, pl.ds(t, 1)], sem.at[s])

    def request(d_ref, s):
        def body(t, c):
            for k in range(TOP_K):
                copy(s, t, k, d_ref[0, 0, t * TOP_K + k]).start()
            return c

        lax.fori_loop(0, tokens, body, 0)

    def drain(s):
        for k in range(TOP_K):
            pltpu.make_async_copy(ys_hbm.at[pl.ds(0, tokens)], buf_ref.at[s, k], sem.at[s]).wait()

    @pl.when(i == 0)
    def _():
        request(dcur_ref, 0)

    drain(slot)
    for t in range(tokens):
        for k in range(TOP_K):
            copy(1 - slot, t, k, dnxt_ref[0, 0, t * TOP_K + k]).start()
    tw = tw_ref[...]
    x = x_ref[...]
    y_lo = alpha * x[:, :half]
    y_hi = alpha * x[:, half:]
    for k in range(TOP_K):
        u = buf_ref[slot, k]
        wk = tw[:, k:k + 1]
        y_lo = y_lo + wk * pltpu.bitcast(u << 16, F32)
        y_hi = y_hi + wk * pltpu.bitcast(u & jnp.uint32(0xFFFF0000), F32)
    out = _ln_rows(jnp.concatenate([y_lo, y_hi], axis=1), g_ref[...], b_ref[...])
    o_ref[...] = out
    ob_ref[...] = out.astype(BF16)

    @pl.when(i + 1 == pl.num_programs(0))
    def _():
        drain(1 - slot)


def _combine(ys, dest, tw, x, g, b, alpha, *, tokens=128):
    s, d = x.shape
    tokens = min(tokens, s)
    steps = s // tokens
    kern = functools.partial(_combine_kernel, alpha=alpha)
    dest3 = dest.reshape(steps, 1, tokens * TOP_K)
    row = lambda width: pl.BlockSpec((tokens, width), lambda i: (i, 0))
    const = lambda shape: pl.BlockSpec(shape, lambda i: (0,) * len(shape))
    return pl.pallas_call(
        kern,
        out_shape=(jax.ShapeDtypeStruct((s, d), F32), jax.ShapeDtypeStruct((s, d), BF16)),
        grid=(steps,),
        in_specs=[pl.BlockSpec((1, 1, tokens * TOP_K), lambda i: (i, 0, 0), memory_space=pltpu.SMEM),
                  pl.BlockSpec((1, 1, tokens * TOP_K), lambda i: (jnp.minimum(i + 1, steps - 1), 0, 0),
                               memory_space=pltpu.SMEM),
                  row(LANES), row(d), const((1, d)), const((1, d)),
                  pl.BlockSpec(memory_space=pl.ANY)],
        out_specs=(row(d), row(d)),
        scratch_shapes=[pltpu.VMEM((2, TOP_K, tokens, d // 2), U32), pltpu.SemaphoreType.DMA((2,))],
        compiler_params=_cparams("arbitrary"),
        name="combine",
    )(dest3, dest3, tw, x, g.reshape(1, d), b.reshape(1, d), ys)


def _mm_rows_kernel(starts_ref, a_ref, wt_ref, *refs, n_casts):
    del starts_ref
    cast_in, o_ref, cast_out, wb_ref = refs[:n_casts], refs[n_casts], refs[n_casts + 1:-1], refs[-1]

    @pl.when(pl.program_id(1) == 0)
    def _():
        wb_ref[...] = wt_ref[0].astype(BF16)

    o_ref[...] = lax.dot_general(a_ref[...], wb_ref[...], (((1,), (1,)), ((), ())),
                                 preferred_element_type=F32).astype(o_ref.dtype)

    for src, dst in zip(cast_in, cast_out):
        dst[...] = src[...].astype(BF16)


def _matmul_rows(a, wt, layer, starts, tn, *, tm, name, casts=()):
    m, k = a.shape
    tm = min(tm, m)
    n_tiles, m_tiles = len(starts), m // tm
    in_specs = [pl.BlockSpec((tm, k), lambda j, i, st: (i, 0)),
                pl.BlockSpec((pl.Element(1), pl.Element(tn), pl.Element(k)),
                             lambda j, i, st: (layer, pl.multiple_of(st[j], SUBLANES), 0))]
    out_specs = [pl.BlockSpec((tm, tn), lambda j, i, st: (i, j))]
    out_shape = [jax.ShapeDtypeStruct((m, n_tiles * tn), F32)]
    args = [jnp.asarray(starts, I32), a, wt]
    for arr, rows, rb in casts:
        nb = rows // rb
        assert rows % rb == 0 and nb <= n_tiles * m_tiles
        cols = arr.shape[1]
        in_specs.append(pl.BlockSpec(
            (rb, cols), lambda j, i, st, nb=nb: (layer * nb + jnp.minimum(j * m_tiles + i, nb - 1), 0)))
        out_specs.append(pl.BlockSpec(
            (rb, cols), lambda j, i, st, nb=nb: (jnp.minimum(j * m_tiles + i, nb - 1), 0)))
        out_shape.append(jax.ShapeDtypeStruct((rows, cols), BF16))
        args.append(arr)
    grid_spec = pltpu.PrefetchScalarGridSpec(
        num_scalar_prefetch=1,
        grid=(n_tiles, m_tiles),
        in_specs=in_specs,
        out_specs=tuple(out_specs),
        scratch_shapes=[pltpu.VMEM((tn, k), BF16)],
    )
    return pl.pallas_call(
        functools.partial(_mm_rows_kernel, n_casts=len(casts)),
        out_shape=tuple(out_shape),
        grid_spec=grid_spec,
        compiler_params=_cparams("arbitrary", "arbitrary", vmem=BIG_TILE_VMEM_LIMIT),
        name=name,
    )(*args)


def _in_proj(xb, w_in, layer, sizes, inner, side_weights):
    z, xbc, dt, u, v, gate_c, x_c, gates = sizes
    src = {"z": 0, "xs": z, "bc": z + inner, "dt": z + xbc}
    src["u"] = src["dt"] + dt
    src["v"] = src["u"] + u
    src["gate_c"] = src["v"] + v
    src["x_c"] = src["gate_c"] + gate_c
    src["gates"] = src["x_c"] + x_c
    width = {"z": z, "xs": inner, "u": u, "v": v, "gate_c": gate_c, "x_c": x_c, "bc": xbc - inner, "gates": gates}
    tn = max(t for t in (1024, 512, 256, 128) if all(w % t == 0 for w in width.values()))
    assert dt <= LANES and all(s % SUBLANES == 0 for s in src.values())
    lay, starts, off = {}, [], 0
    for name in ("z", "xs", "u", "v", "gate_c", "x_c", "bc", "gates"):
        lay[name] = off
        starts += [src[name] + t for t in range(0, width[name], tn)]
        off += width[name]
    wt = jnp.swapaxes(w_in, 1, 2)
    tm = min(512, xb.shape[0])
    m_tiles = xb.shape[0] // tm

    def cast_job(arr, steps):
        rows = math.prod(arr.shape[1:-1])
        rb = 2 * SUBLANES
        while rows // rb > steps or rows % rb:
            rb *= 2
        return arr.reshape(-1, arr.shape[-1]), rows, rb

    proj, big_copy = _matmul_rows(xb, wt, layer, starts, tn, tm=tm, name="in_proj",
                                  casts=[cast_job(side_weights[0], len(starts) * m_tiles)])
    dt_raw, *copies = _matmul_rows(xb, wt, layer, [src["dt"]], LANES, tm=tm, name="dt_proj",
                                   casts=[cast_job(w, m_tiles) for w in side_weights[1:]])
    return proj, dt_raw, lay, [big_copy] + copies


def _moe_plan(te, rk, cnt, ne, n_blocks):
    counts = cnt[0, :ne]
    padded = (counts + EXPERT_ROWS - 1) // EXPERT_ROWS * EXPERT_ROWS
    pad_end = jnp.cumsum(padded)
    pad_start = pad_end - padded
    picked = te[:, :TOP_K, None] == jnp.arange(ne, dtype=I32)[None, None, :]
    dest = (jnp.sum(jnp.where(picked, pad_start[None, None, :], 0), axis=-1) + rk[:, :TOP_K]).astype(I32)
    dest = dest.reshape(-1)
    block_start = jnp.arange(n_blocks, dtype=I32) * EXPERT_ROWS
    block_e = jnp.sum(pad_end[None, :] <= block_start[:, None], axis=1)
    block_e = jnp.minimum(block_e, ne - 1).astype(I32)
    n_used = (pad_end[-1] // EXPERT_ROWS).astype(I32).reshape(1)
    return dest, block_e, n_used, pad_end.astype(I32), counts.astype(I32)


def kernel(x, ln_emb_g, ln_emb_b, w_in, b_merge, conv_a_w, conv_a_b, dt_bias, a_log, d_skip, norm_a_g,
           ln_v_g, ln_v_b, w_spatial, b_spatial, conv_c_w, conv_c_b, w_rg_a, b_rg_a, w_rg_x, b_rg_x, lam,
           p_a, p_b, p_c, w_o, ln_mix_g, ln_mix_b, w_router, b_router, w1, b1, w2, b2, ln_ffn_g, ln_ffn_b):
    bsz, seq, d = x.shape
    depth = w_in.shape[0]
    alpha = (2 * depth) ** 0.25
    heads = dt_bias.shape[1]
    inner = norm_a_g.shape[1]
    sizes = (inner, conv_a_w.shape[2], heads, ln_v_g.shape[1], ln_v_g.shape[1], lam.shape[1], lam.shape[1],
             3 * d)
    ne = w_router.shape[2]
    s = bsz * seq
    n_rows = s * TOP_K + ne * EXPERT_ROWS
    n_blocks = n_rows // EXPERT_ROWS

    xf, xb = _ln0(x.reshape(s, d), ln_emb_g, ln_emb_b)
    for l in range(depth):
        proj, dt_raw, lay, (w2b, pab, pbb, pcb) = _in_proj(xb, w_in, l, sizes, inner, (w2, p_a, p_b, p_c))
        y_a = _ssd(proj, dt_raw, lay, conv_a_w[l], conv_a_b[l], dt_bias[l], a_log[l], d_skip[l], norm_a_g[l])
        y_b = _gmlp(proj, lay, ln_v_g[l], ln_v_b[l], w_spatial[l], b_spatial[l])
        y_c, w1p = _lru(proj, lay, conv_c_w[l], conv_c_b[l], w_rg_a[l], b_rg_a[l], w_rg_x[l], b_rg_x[l], lam[l],
                        w1, l)
        merged = _merge(y_a, y_b, y_c, pab, pbb, pcb, proj, lay, b_merge[l])
        h = _matmul(merged, w_o, l, F32, tm=512, tn=1024, res=xf, alpha=alpha, name="w_o")
        xm, xp, te, tw, rk, cnt = _ln_router(h, ln_mix_g[l], ln_mix_b[l], w_router[l], b_router[l])
        dest, block_e, n_used, pad_end, counts = _moe_plan(te, rk, cnt, ne, n_blocks)
        xs = _dispatch(xp, dest, pad_end, counts, n_rows)
        ys = _experts(xs, block_e, n_used, w1p, _regroup_bias(b1[l]), w2b.reshape(w2.shape[1:]), b2[l])
        xf, xb = _combine(ys, dest, tw, xm, ln_ffn_g[l], ln_ffn_b[l], alpha)
    return xf.reshape(bsz, seq, d)
```
